```python
import math
import jax
import jax.numpy as jnp
from jax import lax
import numpy as np

D_MODEL = 1024
BATCH = 8
SEQ = 2048
DEPTH = 4
DEC_BATCH = 32
DEC_SEQ = 1
PAST_LEN = 16384
PAGE_SIZE = 128

N_AB = (DEPTH + 1) // 2
N_C = DEPTH // 2
MLA_HEADS = 8
MLA_Q_RANK = 256
MLA_KV_RANK = 128
MLA_NOPE = 64
MLA_ROPE = 32
MLA_V = 64
ROPE_THETA = 10000.0
FOX_HEADS = 8
FOX_DIM = 64
NSA_HEADS = 16
NSA_KV_HEADS = 4
NSA_GROUP = NSA_HEADS // NSA_KV_HEADS
NSA_DIM = 64
CMP_STRIDE = 16
CMP_BLOCK = 2 * CMP_STRIDE
CMP_HIDDEN = 64
SEL_BLOCK = 64
SEL_TOP_N = 16
WINDOW = 512
T5_BUCKETS = 32
T5_MAX_DIST = 128
D_FF = int(math.ceil(8 * D_MODEL / 3 / 256)) * 256
Q_BLOCK = 128
SEL_Q_BLOCK = 16
EPS = 1e-6
FORCE_SCORE = 1e4
NEG_SCORE = -1e9
MLA_SCALE = (MLA_NOPE + MLA_ROPE) ** -0.5
FOX_SCALE = FOX_DIM ** -0.5
NSA_SCALE = NSA_DIM ** -0.5
AB_SPLITS = (MLA_Q_RANK, MLA_KV_RANK, MLA_ROPE, FOX_HEADS * FOX_DIM, FOX_HEADS * FOX_DIM, FOX_HEADS * FOX_DIM, FOX_HEADS)
C_SPLITS = (NSA_HEADS * NSA_DIM,) + (NSA_KV_HEADS * NSA_DIM,) * 6 + (3 * NSA_HEADS,)
AB_COLS = sum(AB_SPLITS)
C_COLS = sum(C_SPLITS)

kernel_name = 'hybrid_mla_fox_nsa_decode_step'


def rmsnorm(x, g):
    x32 = x.astype(jnp.float32)
    y = x32 * lax.rsqrt(jnp.mean(x32 * x32, axis=-1, keepdims=True) + EPS)
    return (y * g.astype(jnp.float32)).astype(x.dtype)


def masked_softmax(s, mask):
    s = jnp.where(mask, s.astype(jnp.float32), -jnp.inf)
    m = jnp.max(s, axis=-1, keepdims=True)
    m = jnp.where(jnp.isfinite(m), m, 0.0)
    e = jnp.where(mask, jnp.exp(s - m), 0.0)
    return e / jnp.maximum(jnp.sum(e, axis=-1, keepdims=True), 1e-30)


def rope(x, pos):
    half = x.shape[-1] // 2
    inv = ROPE_THETA ** (-jnp.arange(half, dtype=jnp.float32) / half)
    ang = pos.astype(jnp.float32)[:, None] * inv[None, :]
    ang = ang.reshape((1, pos.shape[0]) + (1,) * (x.ndim - 3) + (half,))
    cos, sin = jnp.cos(ang), jnp.sin(ang)
    x1 = x[..., :half].astype(jnp.float32)
    x2 = x[..., half:].astype(jnp.float32)
    return jnp.concatenate([x1 * cos - x2 * sin, x2 * cos + x1 * sin], axis=-1).astype(x.dtype)


def t5_bucket(dist):
    n = jnp.maximum(dist, 0)
    exact = T5_BUCKETS // 2
    log_ratio = jnp.log(jnp.maximum(n, 1).astype(jnp.float32) / exact) / math.log(T5_MAX_DIST / exact)
    large = jnp.minimum(exact + (log_ratio * (T5_BUCKETS - exact)).astype(jnp.int32), T5_BUCKETS - 1)
    return jnp.where(n < exact, n, large)


def split_cols(a, sizes):
    out, start = [], 0
    for s in sizes:
        out.append(a[..., start:start + s])
        start += s
    return out


def sweep_query_blocks(fn, block, qpos, *qargs):
    tq = qpos.shape[0]
    qb = math.gcd(tq, block)
    nb = tq // qb

    def split(a):
        return jnp.moveaxis(a.reshape((a.shape[0], nb, qb) + a.shape[2:]), 1, 0)

    out = lax.map(lambda xs: fn(*xs), (qpos.reshape(nb, qb),) + tuple(split(a) for a in qargs))
    out = jnp.moveaxis(out, 0, 1)
    return out.reshape((out.shape[0], tq) + out.shape[3:])


def gather_pages(pool, layer, page_table):
    rows = pool[layer, page_table]
    return rows.reshape((page_table.shape[0], page_table.shape[1] * pool.shape[2]) + pool.shape[3:])


def swiglu(h, w_gate, w_up, w_down):
    a = jnp.einsum('btd,df->btf', h, w_gate)
    b = jnp.einsum('btd,df->btf', h, w_up)
    return jnp.einsum('btf,fd->btd', jax.nn.silu(a) * b, w_down)


def mixer_ab(h, pos, past, w_in, fox_bf, mla_gq, mla_gkv, mla_wuq, mla_wuk, mla_wuv, w_out):
    B, T, _ = h.shape
    c_q, c_kv, k_rope, fq, fk, fv, f_logit = split_cols(jnp.einsum('btd,dc->btc', h, w_in), AB_SPLITS)
    q = jnp.einsum('btr,rhd->bthd', rmsnorm(c_q, mla_gq), mla_wuq)
    q_nope = q[..., :MLA_NOPE]
    q_rope = rope(q[..., MLA_NOPE:], pos)
    ckv = rmsnorm(c_kv, mla_gkv)
    krope = rope(k_rope, pos)
    q_lat = jnp.einsum('bthn,chn->bthc', q_nope, mla_wuk)
    fq = fq.reshape(B, T, FOX_HEADS, FOX_DIM)
    fk = fk.reshape(B, T, FOX_HEADS, FOX_DIM)
    fv = fv.reshape(B, T, FOX_HEADS, FOX_DIM)
    logf = jax.nn.log_sigmoid(f_logit.astype(jnp.float32) + fox_bf.astype(jnp.float32))
    if past is None:
        ckv_all, krope_all, fk_all, fv_all, logf_all = ckv, krope, fk, fv, logf
    else:
        p_ckv, p_krope, p_k, p_v, p_logf = past
        ckv_all = jnp.concatenate([p_ckv, ckv], axis=1)
        krope_all = jnp.concatenate([p_krope, krope], axis=1)
        fk_all = jnp.concatenate([p_k, fk], axis=1)
        fv_all = jnp.concatenate([p_v, fv], axis=1)
        logf_all = jnp.concatenate([p_logf.astype(jnp.float32), logf], axis=1)
    tk = ckv_all.shape[1]
    kpos = jnp.arange(tk)
    F = jnp.cumsum(logf_all.astype(jnp.float32), axis=1)
    F_k = jnp.transpose(F, (0, 2, 1))
    F_q = F[:, tk - T:]

    def mla_block(qp, qlb, qrb):
        s = (jnp.einsum('bthc,bsc->bhts', qlb, ckv_all)
             + jnp.einsum('bthr,bsr->bhts', qrb, krope_all)).astype(jnp.float32) * MLA_SCALE
        p = masked_softmax(s, (kpos[None, :] <= qp[:, None])[None, None])
        o_lat = jnp.einsum('bhts,bsc->bthc', p.astype(ckv_all.dtype), ckv_all)
        return jnp.einsum('bthc,chv->bthv', o_lat, mla_wuv)

    def fox_block(qp, qb, fqb):
        s = (jnp.einsum('bthd,bshd->bhts', qb, fk_all).astype(jnp.float32) * FOX_SCALE
             + jnp.transpose(fqb, (0, 2, 1))[..., None] - F_k[:, :, None, :])
        p = masked_softmax(s, (kpos[None, :] <= qp[:, None])[None, None])
        return jnp.einsum('bhts,bshd->bthd', p.astype(fv_all.dtype), fv_all)

    o_mla = sweep_query_blocks(mla_block, Q_BLOCK, pos, q_lat, q_rope)
    o_fox = sweep_query_blocks(fox_block, Q_BLOCK, pos, fq, F_q)
    o = jnp.concatenate([o_mla.reshape(B, T, -1), o_fox.reshape(B, T, -1)], axis=-1)
    out = jnp.einsum('btf,fd->btd', o, w_out)
    return out, (ckv, krope, fk, fv, logf)


def nsa_compress(rows, pe, w1, w2):
    B, Tp, KH, HD = rows.shape
    chunks = rows.reshape(B, Tp // CMP_STRIDE, CMP_STRIDE, KH, HD)
    w1r = w1.reshape(2, CMP_STRIDE, HD, CMP_HIDDEN)
    first = jnp.einsum('bnrkd,rdc->bnkc', chunks, w1r[0])
    second = jnp.einsum('bnrkd,rdc->bnkc', chunks, w1r[1])
    pe_term = jnp.einsum('pd,pdc->c', pe, w1.reshape(CMP_BLOCK, HD, CMP_HIDDEN))
    hid = jax.nn.gelu(first[:, :-1] + second[:, 1:] + pe_term)
    return jnp.einsum('bnkc,cd->bnkd', hid, w2)


def mixer_c(h, pos, past, w_in, pe_k, w1_k, w2_k, pe_v, w1_v, w2_v, t5_table, w_out):
    B, T, _ = h.shape
    q, kc, vc, ks, vs, kw, vw, g = split_cols(jnp.einsum('btd,dc->btc', h, w_in), C_SPLITS)
    q = q.reshape(B, T, NSA_KV_HEADS, NSA_GROUP, NSA_DIM)
    kc, vc, ks, vs, kw, vw = (a.reshape(B, T, NSA_KV_HEADS, NSA_DIM) for a in (kc, vc, ks, vs, kw, vw))
    gates = jax.nn.sigmoid(g.astype(jnp.float32)).reshape(B, T, NSA_KV_HEADS, NSA_GROUP, 3)
    t5_kg = t5_table.reshape(T5_BUCKETS, NSA_KV_HEADS, NSA_GROUP)
    b_idx = jnp.arange(B)[:, None, None, None, None]
    h_idx = jnp.arange(NSA_KV_HEADS)[None, None, :, None, None]
    blk = jnp.arange(SEL_BLOCK)

    if past is None:
        kc_all, vc_all = kc, vc
    else:
        kc_all = jnp.concatenate([past[0], kc], axis=1)
        vc_all = jnp.concatenate([past[1], vc], axis=1)
    tk = kc_all.shape[1]
    pad = (-tk) % CMP_STRIDE
    padw = ((0, 0), (0, pad), (0, 0), (0, 0))
    kcmp = nsa_compress(jnp.pad(kc_all, padw), pe_k, w1_k, w2_k)
    vcmp = nsa_compress(jnp.pad(vc_all, padw), pe_v, w1_v, w2_v)
    n_cmp = kcmp.shape[1]
    c_start = jnp.arange(n_cmp) * CMP_STRIDE
    c_end = c_start + CMP_BLOCK - 1
    dist_c = pos[:, None] - c_end[None, :]
    bias_c = jnp.transpose(t5_kg[t5_bucket(dist_c)], (0, 2, 3, 1))
    s_c = jnp.einsum('btkgd,bnkd->btkgn', q, kcmp).astype(jnp.float32) * NSA_SCALE + bias_c[None]
    p_c = masked_softmax(s_c, (dist_c >= 0)[None, :, None, None, :])
    o_cmp = jnp.einsum('btkgn,bnkd->btkgd', p_c.astype(vcmp.dtype), vcmp)

    n_sel = -(-tk // SEL_BLOCK)
    s_start = jnp.arange(n_sel) * SEL_BLOCK
    overlap = jnp.clip(jnp.minimum(c_start[:, None] + CMP_BLOCK, s_start[None, :] + SEL_BLOCK)
                       - jnp.maximum(c_start[:, None], s_start[None, :]), 0, None).astype(jnp.float32) / CMP_BLOCK
    imp = jnp.einsum('btkgn,nj->btkj', p_c, overlap)
    cur = pos // SEL_BLOCK
    jb = jnp.arange(n_sel)[None, :]
    forced = (jb == 0) | (jb == cur[:, None]) | (jb == cur[:, None] - 1)
    avail = s_start[None, :] <= pos[:, None]
    score = jnp.where(forced[None, :, None, :], FORCE_SCORE, imp)
    score = jnp.where(avail[None, :, None, :], score, NEG_SCORE)
    top_v, top_i = lax.top_k(score, min(SEL_TOP_N, n_sel))
    top_ok = top_v > 0.5 * NEG_SCORE

    if past is None:
        ks_p = jnp.pad(ks, ((0, 0), (0, n_sel * SEL_BLOCK - tk), (0, 0), (0, 0)))
        vs_p = jnp.pad(vs, ((0, 0), (0, n_sel * SEL_BLOCK - tk), (0, 0), (0, 0)))

        def gather(idx):
            rows = idx[..., None] * SEL_BLOCK + blk
            return ks_p[b_idx, rows, h_idx], vs_p[b_idx, rows, h_idx]
    else:
        pool_ks, pool_vs, layer, page_table = past[2], past[3], past[4], past[5]
        page_size = pool_ks.shape[2]
        n_past_sel = page_table.shape[1] * page_size // SEL_BLOCK
        n_new = -(-T // SEL_BLOCK)
        ks_p = jnp.pad(ks, ((0, 0), (0, n_new * SEL_BLOCK - T), (0, 0), (0, 0)))
        vs_p = jnp.pad(vs, ((0, 0), (0, n_new * SEL_BLOCK - T), (0, 0), (0, 0)))

        def gather(idx):
            rows_p = jnp.clip(idx, 0, n_past_sel - 1)[..., None] * SEL_BLOCK + blk
            page = page_table[b_idx, rows_p // page_size]
            off = rows_p % page_size
            past_k = pool_ks[layer, page, off, h_idx]
            past_v = pool_vs[layer, page, off, h_idx]
            rows_n = jnp.clip(idx - n_past_sel, 0, n_new - 1)[..., None] * SEL_BLOCK + blk
            is_new = (idx >= n_past_sel)[..., None, None]
            return (jnp.where(is_new, ks_p[b_idx, rows_n, h_idx], past_k),
                    jnp.where(is_new, vs_p[b_idx, rows_n, h_idx], past_v))

    def sel_block(qp, qb, ib, okb):
        kb, vb = gather(ib)
        kpos = ib[..., None] * SEL_BLOCK + blk
        dist = qp[None, :, None, None, None] - kpos
        mask = okb[..., None] & (dist >= 0)
        bias = jnp.moveaxis(t5_kg[t5_bucket(dist), h_idx], -1, 3)
        s = jnp.einsum('btkgd,btknjd->btkgnj', qb, kb).astype(jnp.float32) * NSA_SCALE + bias
        shp = s.shape
        p = masked_softmax(s.reshape(shp[:4] + (-1,)), mask.reshape(shp[:3] + (1, -1)))
        return jnp.einsum('btkgnj,btknjd->btkgd', p.reshape(shp).astype(vb.dtype), vb)

    o_slc = sweep_query_blocks(sel_block, SEL_Q_BLOCK, pos, q, top_i, top_ok)

    def window_attend(qp, qb, kw_b, vw_b, kpos):
        dist = qp[:, None] - kpos[None, :]
        bias = jnp.transpose(t5_kg[t5_bucket(dist)], (0, 2, 3, 1))
        s = jnp.einsum('btkgd,bskd->btkgs', qb, kw_b).astype(jnp.float32) * NSA_SCALE + bias[None]
        mask = (kpos[None, :] >= 0) & (dist >= 0) & (dist <= WINDOW)
        p = masked_softmax(s, mask[None, :, None, None, :])
        return jnp.einsum('btkgs,bskd->btkgd', p.astype(vw_b.dtype), vw_b)

    if past is None:
        kw_p = jnp.pad(kw, ((0, 0), (WINDOW, 0), (0, 0), (0, 0)))
        vw_p = jnp.pad(vw, ((0, 0), (WINDOW, 0), (0, 0), (0, 0)))

        def win_block(qp, qb):
            span = qb.shape[1] + WINDOW
            kb = lax.dynamic_slice_in_dim(kw_p, qp[0], span, axis=1)
            vb = lax.dynamic_slice_in_dim(vw_p, qp[0], span, axis=1)
            return window_attend(qp, qb, kb, vb, qp[0] - WINDOW + jnp.arange(span))

        o_win = sweep_query_blocks(win_block, Q_BLOCK, pos, q)
        keep = min(WINDOW, T)
        win_k_new, win_v_new = kw[:, T - keep:], vw[:, T - keep:]
    else:
        win_k, win_v = past[6], past[7]
        nbuf = win_k.shape[1]
        kw_all = jnp.concatenate([win_k, kw], axis=1)
        vw_all = jnp.concatenate([win_v, vw], axis=1)
        o_win = window_attend(pos, q, kw_all, vw_all, pos[0] - nbuf + jnp.arange(nbuf + T))
        win_k_new, win_v_new = kw_all[:, T:], vw_all[:, T:]

    o = (gates[..., 0:1] * o_cmp + gates[..., 1:2] * o_slc + gates[..., 2:3] * o_win).astype(h.dtype)
    out = jnp.einsum('btf,fd->btd', o.reshape(B, T, -1), w_out)
    return out, (kc, vc, ks, vs, win_k_new, win_v_new)


def setup_inputs(seed: int = 0) -> dict:
    key = jax.random.key(seed)
    keys = jax.random.split(key, 64)
    counter = iter(range(64))

    def nrm(shape, scale=1.0):
        return jax.random.normal(keys[next(counter)], shape, jnp.float32) * scale

    def gain(shape):
        return 1.0 + nrm(shape, 0.02)

    n_pages = PAST_LEN // PAGE_SIZE
    n_used = DEC_BATCH * n_pages
    n_pool = n_used + max(1, n_used // 4)
    win_buf = min(WINDOW, PAST_LEN)
    page_table = jax.random.permutation(keys[next(counter)], n_pool)[:n_used].reshape(DEC_BATCH, n_pages).astype(jnp.int32)
    x_prompt = nrm((BATCH, SEQ, D_MODEL))
    x_sample = nrm((DEC_BATCH, DEC_SEQ, D_MODEL))
    cache_mla_ckv = nrm((N_AB, n_pool, PAGE_SIZE, MLA_KV_RANK))
    cache_mla_krope = nrm((N_AB, n_pool, PAGE_SIZE, MLA_ROPE))
    cache_fox_k = nrm((N_AB, n_pool, PAGE_SIZE, FOX_HEADS, FOX_DIM))
    cache_fox_v = nrm((N_AB, n_pool, PAGE_SIZE, FOX_HEADS, FOX_DIM))
    cache_fox_logf = jax.nn.log_sigmoid(2.5 + nrm((N_AB, n_pool, PAGE_SIZE, FOX_HEADS)))
    cache_nsa_kcmp = nrm((N_C, n_pool, PAGE_SIZE, NSA_KV_HEADS, NSA_DIM))
    cache_nsa_vcmp = nrm((N_C, n_pool, PAGE_SIZE, NSA_KV_HEADS, NSA_DIM))
    cache_nsa_kslc = nrm((N_C, n_pool, PAGE_SIZE, NSA_KV_HEADS, NSA_DIM))
    cache_nsa_vslc = nrm((N_C, n_pool, PAGE_SIZE, NSA_KV_HEADS, NSA_DIM))
    state_nsa_kwin = nrm((N_C, DEC_BATCH, win_buf, NSA_KV_HEADS, NSA_DIM))
    state_nsa_vwin = nrm((N_C, DEC_BATCH, win_buf, NSA_KV_HEADS, NSA_DIM))
    ab_out_in = MLA_HEADS * MLA_V + FOX_HEADS * FOX_DIM
    return {
        'x_prompt': x_prompt,
        'x_sample': x_sample,
        'cache_mla_ckv': cache_mla_ckv,
        'cache_mla_krope': cache_mla_krope,
        'cache_fox_k': cache_fox_k,
        'cache_fox_v': cache_fox_v,
        'cache_fox_logf': cache_fox_logf,
        'cache_nsa_kcmp': cache_nsa_kcmp,
        'cache_nsa_vcmp': cache_nsa_vcmp,
        'cache_nsa_kslc': cache_nsa_kslc,
        'cache_nsa_vslc': cache_nsa_vslc,
        'state_nsa_kwin': state_nsa_kwin,
        'state_nsa_vwin': state_nsa_vwin,
        'page_table': page_table,
        'norm_attn': gain((DEPTH, D_MODEL)),
        'norm_ffn': gain((DEPTH, D_MODEL)),
        'norm_final': gain((D_MODEL,)),
        'ab_w_in': nrm((N_AB, D_MODEL, AB_COLS), D_MODEL ** -0.5),
        'ab_fox_bf': 2.5 + nrm((N_AB, FOX_HEADS), 0.5),
        'ab_mla_gq': gain((N_AB, MLA_Q_RANK)),
        'ab_mla_gkv': gain((N_AB, MLA_KV_RANK)),
        'ab_mla_wuq': nrm((N_AB, MLA_Q_RANK, MLA_HEADS, MLA_NOPE + MLA_ROPE), MLA_Q_RANK ** -0.5),
        'ab_mla_wuk': nrm((N_AB, MLA_KV_RANK, MLA_HEADS, MLA_NOPE), MLA_KV_RANK ** -0.5),
        'ab_mla_wuv': nrm((N_AB, MLA_KV_RANK, MLA_HEADS, MLA_V), MLA_KV_RANK ** -0.5),
        'ab_w_out': nrm((N_AB, ab_out_in, D_MODEL), ab_out_in ** -0.5),
        'c_w_in': nrm((N_C, D_MODEL, C_COLS), D_MODEL ** -0.5),
        'c_pe_k': nrm((N_C, CMP_BLOCK, NSA_DIM), 0.5),
        'c_w1_k': nrm((N_C, CMP_BLOCK * NSA_DIM, CMP_HIDDEN), (CMP_BLOCK * NSA_DIM) ** -0.5),
        'c_w2_k': nrm((N_C, CMP_HIDDEN, NSA_DIM), CMP_HIDDEN ** -0.5),
        'c_pe_v': nrm((N_C, CMP_BLOCK, NSA_DIM), 0.5),
        'c_w1_v': nrm((N_C, CMP_BLOCK * NSA_DIM, CMP_HIDDEN), (CMP_BLOCK * NSA_DIM) ** -0.5),
        'c_w2_v': nrm((N_C, CMP_HIDDEN, NSA_DIM), CMP_HIDDEN ** -0.5),
        'c_w_out': nrm((N_C, NSA_HEADS * NSA_DIM, D_MODEL), (NSA_HEADS * NSA_DIM) ** -0.5),
        't5_table': nrm((T5_BUCKETS, NSA_HEADS), 0.5),
        'ffn_w_gate': nrm((DEPTH, D_MODEL, D_FF), D_MODEL ** -0.5),
        'ffn_w_up': nrm((DEPTH, D_MODEL, D_FF), D_MODEL ** -0.5),
        'ffn_w_down': nrm((DEPTH, D_FF, D_MODEL), D_FF ** -0.5),
    }


def reference(x_prompt, x_sample, cache_mla_ckv, cache_mla_krope, cache_fox_k, cache_fox_v, cache_fox_logf,
              cache_nsa_kcmp, cache_nsa_vcmp, cache_nsa_kslc, cache_nsa_vslc, state_nsa_kwin, state_nsa_vwin,
              page_table, norm_attn, norm_ffn, norm_final, ab_w_in, ab_fox_bf, ab_mla_gq, ab_mla_gkv,
              ab_mla_wuq, ab_mla_wuk, ab_mla_wuv, ab_w_out, c_w_in, c_pe_k, c_w1_k, c_w2_k, c_pe_v, c_w1_v,
              c_w2_v, c_w_out, t5_table, ffn_w_gate, ffn_w_up, ffn_w_down):
    t_p = x_prompt.shape[1]
    t_s = x_sample.shape[1]
    past_len = page_table.shape[1] * cache_mla_ckv.shape[2]
    pos_p = jnp.arange(t_p)
    pos_s = past_len + jnp.arange(t_s)
    ab_p, ab_s, c_p, c_s = [], [], [], []
    xp, xs = x_prompt, x_sample
    for l in range(DEPTH):
        i = l // 2
        hp = rmsnorm(xp, norm_attn[l])
        hs = rmsnorm(xs, norm_attn[l])
        if l % 2 == 0:
            w = (ab_w_in[i], ab_fox_bf[i], ab_mla_gq[i], ab_mla_gkv[i], ab_mla_wuq[i], ab_mla_wuk[i],
                 ab_mla_wuv[i], ab_w_out[i])
            past = (gather_pages(cache_mla_ckv, i, page_table), gather_pages(cache_mla_krope, i, page_table),
                    gather_pages(cache_fox_k, i, page_table), gather_pages(cache_fox_v, i, page_table),
                    gather_pages(cache_fox_logf, i, page_table))
            op, rp = mixer_ab(hp, pos_p, None, *w)
            os_, rs = mixer_ab(hs, pos_s, past, *w)
            ab_p.append(rp)
            ab_s.append(rs)
        else:
            w = (c_w_in[i], c_pe_k[i], c_w1_k[i], c_w2_k[i], c_pe_v[i], c_w1_v[i], c_w2_v[i], t5_table, c_w_out[i])
            past = (gather_pages(cache_nsa_kcmp, i, page_table), gather_pages(cache_nsa_vcmp, i, page_table),
                    cache_nsa_kslc, cache_nsa_vslc, i, page_table, state_nsa_kwin[i], state_nsa_vwin[i])
            op, rp = mixer_c(hp, pos_p, None, *w)
            os_, rs = mixer_c(hs, pos_s, past, *w)
            c_p.append(rp)
            c_s.append(rs)
        xp = xp + op
        xs = xs + os_
        xp = xp + swiglu(rmsnorm(xp, norm_ffn[l]), ffn_w_gate[l], ffn_w_up[l], ffn_w_down[l])
        xs = xs + swiglu(rmsnorm(xs, norm_ffn[l]), ffn_w_gate[l], ffn_w_up[l], ffn_w_down[l])
    y_prompt = rmsnorm(xp, norm_final)
    y_sample = rmsnorm(xs, norm_final)

    def stack(rows, j):
        return jnp.stack([r[j] for r in rows])

    return (y_prompt, y_sample,
            stack(ab_p, 0), stack(ab_s, 0), stack(ab_p, 1), stack(ab_s, 1),
            stack(ab_p, 2), stack(ab_s, 2), stack(ab_p, 3), stack(ab_s, 3), stack(ab_p, 4), stack(ab_s, 4),
            stack(c_p, 0), stack(c_s, 0), stack(c_p, 1), stack(c_s, 1), stack(c_p, 2), stack(c_s, 2),
            stack(c_p, 3), stack(c_s, 3), stack(c_p, 4), stack(c_s, 4), stack(c_p, 5), stack(c_s, 5))
```

```python
import functools
import math

import numpy as np
import jax
import jax.numpy as jnp
from jax import lax
from jax.experimental import pallas as pl
from jax.experimental.pallas import tpu as pltpu

F32 = jnp.float32
BF = jnp.bfloat16
I32 = jnp.int32

MLA_NOPE = 64
MLA_ROPE = 32
ROPE_THETA = 10000.0
CMP_STRIDE = 16
CMP_BLOCK = 32
SEL_BLOCK = 64
SEL_TOP_N = 16
WINDOW = 512
T5_BUCKETS = 32
T5_MAX_DIST = 128
EPS = 1e-6
FORCE_SCORE = 1e4
NEG_SCORE = -1e9
NEG_BIG = -1e30
LANES = 128

VMEM_LIMIT = 56 * 1024 * 1024


def _params(sem, vmem=VMEM_LIMIT):
    return pltpu.CompilerParams(dimension_semantics=sem, vmem_limit_bytes=vmem)


def _const_spec(shape):
    nd = len(shape)
    return pl.BlockSpec(shape, lambda *a: (0,) * nd)


def _rms(x, g):
    ms = jnp.mean(x * x, axis=-1, keepdims=True)
    return x * lax.rsqrt(ms + EPS) * g


def _dot(a, b):
    return jnp.dot(a, b, preferred_element_type=F32)


def _dot_nt(a, b):
    return lax.dot_general(a, b, (((1,), (1,)), ((), ())), preferred_element_type=F32)


def _log_sigmoid(x):
    return jnp.minimum(x, 0.0) - jnp.log(1.0 + jnp.exp(-jnp.abs(x)))


def _sigmoid(x):
    return 1.0 / (1.0 + jnp.exp(-x))


def _t5_thresholds():
    exact = T5_BUCKETS // 2
    d = np.arange(0, T5_MAX_DIST + 1)
    lr = np.log(np.maximum(d, 1).astype(np.float64) / exact) / math.log(T5_MAX_DIST / exact)
    large = np.minimum(exact + (lr * (T5_BUCKETS - exact)).astype(np.int64), T5_BUCKETS - 1)
    bucket = np.where(d < exact, d, large)
    return [int(np.argmax(bucket >= j)) for j in range(T5_BUCKETS)]


_T5_THR = _t5_thresholds()


def _t5_bias(dist, values):
    out = jnp.zeros(dist.shape, F32) + values(0)
    for j in range(1, T5_BUCKETS):
        out = jnp.where(dist >= _T5_THR[j], values(j), out)
    return out


def _ab_proj_kernel(x_ref, g_ref, w_ref, gkv_ref, bf_ref, cos_ref, sin_ref,
                    cq_ref, ckv_ref, kr_ref, kcat_ref, fq_ref, fk_ref, fv_ref,
                    fkb_ref, fvb_ref, lf_ref, *, fox_scale, nfh):
    h = _rms(x_ref[...], g_ref[...]).astype(BF)

    def mm(c0, c1):
        return _dot(h, w_ref[:, c0:c1])

    cq_ref[...] = mm(0, 256)
    ckv = _rms(mm(256, 384), gkv_ref[...])
    ckv_ref[...] = ckv
    kr = mm(384, 512) * cos_ref[...] + mm(512, 640) * sin_ref[...]
    kr_ref[...] = kr[:, :MLA_ROPE]
    kcat_ref[:, 0:128] = ckv.astype(BF)
    kcat_ref[:, 128:256] = kr.astype(BF)
    fq_ref[...] = (mm(640, 1152) * fox_scale).astype(BF)
    fk = mm(1152, 1664)
    fk_ref[...] = fk
    fkb_ref[...] = fk.astype(BF)
    fv = mm(1664, 2176)
    fv_ref[...] = fv
    fvb_ref[...] = fv.astype(BF)
    fl = mm(2176, 2304)[:, :nfh] + bf_ref[...]
    lf_ref[...] = _log_sigmoid(fl)


def _ab_proj(x, g, w, gkv, bf, cos_t, sin_t, tm):
    m, d = x.shape
    nt = cos_t.shape[0] // tm
    nfh = bf.shape[1]
    fd = 512
    row = lambda wd: pl.BlockSpec((tm, wd), lambda i: (i, 0))
    tab = pl.BlockSpec((tm, LANES), lambda i: (i % nt, 0))
    outs = [((m, 256), F32), ((m, 128), F32), ((m, MLA_ROPE), F32), ((m, 256), BF),
            ((m, fd), BF), ((m, fd), F32), ((m, fd), F32), ((m, fd), BF), ((m, fd), BF),
            ((m, nfh), F32)]
    return pl.pallas_call(
        functools.partial(_ab_proj_kernel, fox_scale=(fd // nfh) ** -0.5, nfh=nfh),
        grid=(m // tm,),
        in_specs=[row(d), _const_spec(g.shape), _const_spec(w.shape), _const_spec(gkv.shape),
                  _const_spec(bf.shape), tab, tab],
        out_specs=[row(s[1]) for s, _ in outs],
        out_shape=[jax.ShapeDtypeStruct(s, dt) for s, dt in outs],
        compiler_params=_params(("parallel",)),
        name="ab_proj",
    )(x, g, w, gkv, bf, cos_t, sin_t)


def _mla_q_kernel(cq_ref, gq_ref, wuq_ref, wuk_ref, cos_ref, sin_ref, q_ref, *, nh):
    cqn = _rms(cq_ref[...], gq_ref[...]).astype(BF)
    qn = _dot(cqn, wuq_ref[:, 0:nh * MLA_NOPE]).astype(BF)
    qlat = _dot(qn, wuk_ref[...])
    r0 = nh * MLA_NOPE
    r1 = r0 + nh * LANES
    cos = cos_ref[...]
    sin = sin_ref[...]
    for h in range(nh):
        qr = (_dot(cqn, wuq_ref[:, r0 + h * LANES:r0 + (h + 1) * LANES]) * cos
              + _dot(cqn, wuq_ref[:, r1 + h * LANES:r1 + (h + 1) * LANES]) * sin)
        q_ref[:, 256 * h:256 * h + 128] = qlat[:, 128 * h:128 * (h + 1)].astype(BF)
        q_ref[:, 256 * h + 128:256 * (h + 1)] = qr.astype(BF)


def _mla_q(cq, gq, wuq, wukbd, cos_t, sin_t, tm, nh):
    m = cq.shape[0]
    nt = cos_t.shape[0] // tm
    tab = pl.BlockSpec((tm, LANES), lambda i: (i % nt, 0))
    return pl.pallas_call(
        functools.partial(_mla_q_kernel, nh=nh),
        grid=(m // tm,),
        in_specs=[pl.BlockSpec((tm, cq.shape[1]), lambda i: (i, 0)), _const_spec(gq.shape),
                  _const_spec(wuq.shape), _const_spec(wukbd.shape), tab, tab],
        out_specs=pl.BlockSpec((tm, 256 * nh), lambda i: (i, 0)),
        out_shape=jax.ShapeDtypeStruct((m, 256 * nh), BF),
        compiler_params=_params(("parallel",)),
        name="mla_q",
    )(cq, gq, wuq, wukbd, cos_t, sin_t)


def _softmax_step(s, m_prev, l_prev):
    m_new = jnp.maximum(m_prev, jnp.max(s, axis=-1, keepdims=True))
    alpha = jnp.exp(m_prev - m_new)
    p = jnp.exp(s - m_new)
    l_new = alpha * l_prev + jnp.sum(p, axis=-1, keepdims=True)
    return p, alpha, m_new, l_new


def _mla_attn_kernel(q_ref, k_ref, wuv_ref, o_ref, m_ref, l_ref, acc_ref, *, tq, tk, nh, scale):
    qi = pl.program_id(1)
    ki = pl.program_id(2)
    nk = pl.num_programs(2)

    @pl.when(ki == 0)
    def _():
        m_ref[...] = jnp.full(m_ref.shape, NEG_BIG, F32)
        l_ref[...] = jnp.zeros(l_ref.shape, F32)
        acc_ref[...] = jnp.zeros(acc_ref.shape, F32)

    @pl.when(ki * tk <= qi * tq + tq - 1)
    def _():
        k = k_ref[0]
        v = k[:, 0:128]
        rows = qi * tq + lax.broadcasted_iota(I32, (tq, tk), 0)
        cols = ki * tk + lax.broadcasted_iota(I32, (tq, tk), 1)
        mask = cols <= rows
        for h in range(nh):
            q = q_ref[0, :, 256 * h:256 * (h + 1)]
            s = jnp.where(mask, _dot_nt(q, k) * scale, NEG_BIG)
            p, alpha, m_new, l_new = _softmax_step(s, m_ref[h], l_ref[h])
            acc_ref[h] = alpha * acc_ref[h] + _dot(p.astype(BF), v)
            m_ref[h] = m_new
            l_ref[h] = l_new

    @pl.when(ki == nk - 1)
    def _():
        olat = jnp.concatenate([(acc_ref[h] / l_ref[h]).astype(BF) for h in range(nh)], axis=1)
        o_ref[0] = _dot(olat, wuv_ref[...]).astype(BF)


def _mla_attn(qcat, kcat, wuvbd, nh, scale, tq=256, tk=256):
    b, t, _ = qcat.shape
    nq, nk = t // tq, t // tk
    kmap = lambda bi, qi, ki: (bi, jnp.minimum(ki, (qi * tq + tq - 1) // tk), 0)
    return pl.pallas_call(
        functools.partial(_mla_attn_kernel, tq=tq, tk=tk, nh=nh, scale=scale),
        grid=(b, nq, nk),
        in_specs=[pl.BlockSpec((1, tq, 256 * nh), lambda bi, qi, ki: (bi, qi, 0)),
                  pl.BlockSpec((1, tk, 256), kmap), _const_spec(wuvbd.shape)],
        out_specs=pl.BlockSpec((1, tq, wuvbd.shape[1]), lambda bi, qi, ki: (bi, qi, 0)),
        out_shape=jax.ShapeDtypeStruct((b, t, wuvbd.shape[1]), BF),
        scratch_shapes=[pltpu.VMEM((nh, tq, 1), F32), pltpu.VMEM((nh, tq, 1), F32),
                        pltpu.VMEM((nh, tq, 128), F32)],
        compiler_params=_params(("parallel", "parallel", "arbitrary")),
        name="mla_attn",
    )(qcat, kcat, wuvbd)


def _cumsum_kernel(x_ref, o_ref):
    x = x_ref[0]
    n = x.shape[1]
    lane = lax.broadcasted_iota(I32, x.shape, 1)
    k = 1
    while k < n:
        x = x + jnp.where(lane >= k, pltpu.roll(x, k, axis=1), 0.0)
        k *= 2
    o_ref[0] = x


def _cumsum_lanes(x):
    b, h, t = x.shape
    return pl.pallas_call(
        _cumsum_kernel, grid=(b,),
        in_specs=[pl.BlockSpec((1, h, t), lambda i: (i, 0, 0))],
        out_specs=pl.BlockSpec((1, h, t), lambda i: (i, 0, 0)),
        out_shape=jax.ShapeDtypeStruct((b, h, t), F32),
        compiler_params=_params(("parallel",)),
        name="fox_cumsum",
    )(x)


def _fox_attn_kernel(q_ref, k_ref, v_ref, f_ref, ft_ref, o_ref, m_ref, l_ref, acc_ref, *, tq, tk, nh):
    qi = pl.program_id(1)
    ki = pl.program_id(2)
    nk = pl.num_programs(2)

    @pl.when(ki == 0)
    def _():
        m_ref[...] = jnp.full(m_ref.shape, NEG_BIG, F32)
        l_ref[...] = jnp.zeros(l_ref.shape, F32)
        acc_ref[...] = jnp.zeros(acc_ref.shape, F32)

    @pl.when(ki * tk <= qi * tq + tq - 1)
    def _():
        rows = qi * tq + lax.broadcasted_iota(I32, (tq, tk), 0)
        cols = ki * tk + lax.broadcasted_iota(I32, (tq, tk), 1)
        mask = cols <= rows
        lo = lax.broadcasted_iota(I32, (tq, LANES), 1) < 64
        fq = f_ref[0]
        fk = ft_ref[0]
        for hp in range(nh // 2):
            q2 = q_ref[0, :, 128 * hp:128 * (hp + 1)]
            k2 = k_ref[0, :, 128 * hp:128 * (hp + 1)]
            v2 = v_ref[0, :, 128 * hp:128 * (hp + 1)]
            zero = jnp.zeros_like(q2)
            pvs, alphas = [], []
            for e in range(2):
                h = 2 * hp + e
                qm = jnp.where(lo, q2, zero) if e == 0 else jnp.where(lo, zero, q2)
                s = _dot_nt(qm, k2) + (fq[:, h:h + 1] - fk[h:h + 1, :])
                s = jnp.where(mask, s, NEG_BIG)
                p, alpha, m_new, l_new = _softmax_step(s, m_ref[h], l_ref[h])
                m_ref[h] = m_new
                l_ref[h] = l_new
                pvs.append(_dot(p.astype(BF), v2))
                alphas.append(alpha)
            a2 = jnp.where(lo, alphas[0], alphas[1])
            acc_ref[:, 128 * hp:128 * (hp + 1)] = (a2 * acc_ref[:, 128 * hp:128 * (hp + 1)]
                                                   + jnp.where(lo, pvs[0], pvs[1]))

    @pl.when(ki == nk - 1)
    def _():
        lo = lax.broadcasted_iota(I32, (tq, LANES), 1) < 64
        for hp in range(nh // 2):
            l2 = jnp.where(lo, l_ref[2 * hp], l_ref[2 * hp + 1])
            o_ref[0, :, 128 * hp:128 * (hp + 1)] = (acc_ref[:, 128 * hp:128 * (hp + 1)] / l2).astype(BF)


def _fox_attn(fq, fk, fv, f, ft, nh, tq=256, tk=256):
    b, t, w = fq.shape
    nq, nk = t // tq, t // tk
    kclamp = lambda qi, ki: jnp.minimum(ki, (qi * tq + tq - 1) // tk)
    return pl.pallas_call(
        functools.partial(_fox_attn_kernel, tq=tq, tk=tk, nh=nh),
        grid=(b, nq, nk),
        in_specs=[pl.BlockSpec((1, tq, w), lambda bi, qi, ki: (bi, qi, 0)),
                  pl.BlockSpec((1, tk, w), lambda bi, qi, ki: (bi, kclamp(qi, ki), 0)),
                  pl.BlockSpec((1, tk, w), lambda bi, qi, ki: (bi, kclamp(qi, ki), 0)),
                  pl.BlockSpec((1, tq, nh), lambda bi, qi, ki: (bi, qi, 0)),
                  pl.BlockSpec((1, nh, tk), lambda bi, qi, ki: (bi, 0, kclamp(qi, ki)))],
        out_specs=pl.BlockSpec((1, tq, w), lambda bi, qi, ki: (bi, qi, 0)),
        out_shape=jax.ShapeDtypeStruct((b, t, w), BF),
        scratch_shapes=[pltpu.VMEM((nh, tq, 1), F32), pltpu.VMEM((nh, tq, 1), F32),
                        pltpu.VMEM((tq, w), F32)],
        compiler_params=_params(("parallel", "parallel", "arbitrary")),
        name="fox_attn",
    )(fq, fk, fv, f, ft)


def _post_kernel(*refs, n_o, tf, final):
    x_ref = refs[0]
    o_refs = refs[1:1 + n_o]
    wo_refs = refs[1 + n_o:2 + n_o]
    g_ref, wg_ref, wu_ref, wd_ref = refs[2 + n_o:6 + n_o]
    nxt = 6 + n_o
    if final:
        gf_ref = refs[nxt]
        nxt += 1
    out_ref = refs[nxt]
    o = o_refs[0][...] if n_o == 1 else jnp.concatenate([r[...] for r in o_refs], axis=1)
    x1 = x_ref[...] + _dot(o, wo_refs[0][...])
    h = _rms(x1, g_ref[...]).astype(BF)
    acc = x1
    nf = wg_ref.shape[1] // tf
    for f in range(nf):
        a = _dot(h, wg_ref[:, f * tf:(f + 1) * tf])
        u = _dot(h, wu_ref[:, f * tf:(f + 1) * tf])
        t = (a * _sigmoid(a) * u).astype(BF)
        acc = acc + _dot(t, wd_ref[f * tf:(f + 1) * tf, :])
    if final:
        out_ref[...] = _rms(acc, gf_ref[...])
    else:
        out_ref[...] = acc


def _post(x, os_, wo, g, wg, wu, wd, gf, tm):
    m, d = x.shape
    n_o = len(os_)
    final = gf is not None
    row = lambda wd_: pl.BlockSpec((tm, wd_), lambda i: (i, 0))
    single = lambda a: pl.BlockSpec(a.shape, lambda i: (0,) * a.ndim, pipeline_mode=pl.Buffered(1))
    args = [x] + list(os_) + [wo, g, wg, wu, wd] + ([gf] if final else [])
    in_specs = ([row(d)] + [row(o.shape[1]) for o in os_] + [single(wo)]
                + [_const_spec(g.shape), single(wg), single(wu), single(wd)]
                + ([_const_spec(gf.shape)] if final else []))
    return pl.pallas_call(
        functools.partial(_post_kernel, n_o=n_o, tf=256, final=final),
        grid=(m // tm,),
        in_specs=in_specs,
        out_specs=row(d),
        out_shape=jax.ShapeDtypeStruct((m, d), F32),
        compiler_params=_params(("parallel",)),
        name="post_ffn",
    )(*args)


def _c_proj_kernel(x_ref, g_ref, w_ref, q_ref, kc_ref, vc_ref, ks_ref, vs_ref, kw_ref, vw_ref,
                   ksb_ref, vsb_ref, kwb_ref, vwb_ref, gt_ref, *, scale, nq, nkv):
    h = _rms(x_ref[...], g_ref[...]).astype(BF)

    def mm(c0, c1):
        return _dot(h, w_ref[:, c0:c1])

    q_ref[...] = (mm(0, nq) * scale).astype(BF)
    c = nq
    kc_ref[...] = mm(c, c + nkv)
    vc_ref[...] = mm(c + nkv, c + 2 * nkv)
    ks = mm(c + 2 * nkv, c + 3 * nkv)
    ks_ref[...] = ks
    ksb_ref[...] = ks.astype(BF)
    vs = mm(c + 3 * nkv, c + 4 * nkv)
    vs_ref[...] = vs
    vsb_ref[...] = vs.astype(BF)
    kw = mm(c + 4 * nkv, c + 5 * nkv)
    kw_ref[...] = kw
    kwb_ref[...] = kw.astype(BF)
    vw = mm(c + 5 * nkv, c + 6 * nkv)
    vw_ref[...] = vw
    vwb_ref[...] = vw.astype(BF)
    gt_ref[...] = _sigmoid(mm(c + 6 * nkv, c + 6 * nkv + LANES))


def _c_proj(x, g, w, tm, nq, nkv, scale):
    m, d = x.shape
    row = lambda wd: pl.BlockSpec((tm, wd), lambda i: (i, 0))
    outs = ([((m, nq), BF)] + [((m, nkv), F32)] * 6 + [((m, nkv), BF)] * 4 + [((m, LANES), F32)])
    return pl.pallas_call(
        functools.partial(_c_proj_kernel, scale=scale, nq=nq, nkv=nkv),
        grid=(m // tm,),
        in_specs=[row(d), _const_spec(g.shape), _const_spec(w.shape)],
        out_specs=[row(s[1]) for s, _ in outs],
        out_shape=[jax.ShapeDtypeStruct(s, dt) for s, dt in outs],
        compiler_params=_params(("parallel",)),
        name="c_proj",
    )(x, g, w)


def _gelu(x):
    return 0.5 * x * (1.0 + jnp.tanh(math.sqrt(2.0 / math.pi) * (x + 0.044715 * (x * x * x))))


def _compress_tail(y, pe_ref, w1_ref, w2_ref, width):
    nchunk = y.shape[0]
    pe_term = _dot(pe_ref[...], w1_ref[...])[0:1, :]
    second = pltpu.roll(y[:, width:2 * width], nchunk - 1, axis=0)
    hid = _gelu(y[:, 0:width] + second + pe_term).astype(BF)
    return _dot(hid, w2_ref[...]).astype(BF)


def _compress_prompt_kernel(kc0_ref, kc1_ref, vc0_ref, vc1_ref, wk_ref, wv_ref, pek_ref, pev_ref,
                            w1k_ref, w1v_ref, w2k_ref, w2v_ref, ko_ref, vo_ref, *, nchunk, width):
    for x_refs, w_ref, pe_ref, w1_ref, w2_ref, o_ref in (
            ((kc0_ref, kc1_ref), wk_ref, pek_ref, w1k_ref, w2k_ref, ko_ref),
            ((vc0_ref, vc1_ref), wv_ref, pev_ref, w1v_ref, w2v_ref, vo_ref)):
        y = jnp.zeros((nchunk, 2 * width), F32)
        for r in range(CMP_STRIDE):
            for half, x_ref in enumerate(x_refs):
                xr = x_ref[0, pl.ds(r, nchunk, stride=CMP_STRIDE), :].astype(BF)
                y = y + _dot(xr, w_ref[r, half * LANES:(half + 1) * LANES, :])
        o_ref[0] = _compress_tail(y, pe_ref, w1_ref, w2_ref, width)


def _compress_prompt(kc, vc, wk, wv, pek, pev, w1k, w1v, w2k, w2v):
    b, t, width = kc.shape
    assert width == 2 * LANES
    nchunk = t // CMP_STRIDE
    half = lambda j: pl.BlockSpec((1, t, LANES), lambda i: (i, 0, j))
    oblk = pl.BlockSpec((1, nchunk, width), lambda i: (i, 0, 0))
    consts = [wk, wv, pek, pev, w1k, w1v, w2k, w2v]
    return pl.pallas_call(
        functools.partial(_compress_prompt_kernel, nchunk=nchunk, width=width),
        grid=(b,),
        in_specs=[half(0), half(1), half(0), half(1)] + [_const_spec(c.shape) for c in consts],
        out_specs=[oblk, oblk],
        out_shape=[jax.ShapeDtypeStruct((b, nchunk, width), BF)] * 2,
        compiler_params=_params(("parallel",)),
        name="nsa_compress_prompt",
    )(kc, kc, vc, vc, *consts)


def _t5_tiles_kernel(tbl_ref, tiles_ref, bc_ref, *, t, ncmp):
    h = pl.program_id(0)
    val = lambda j: tbl_ref[j, h]
    r = lax.broadcasted_iota(I32, (LANES, LANES), 0)
    c = lax.broadcasted_iota(I32, (LANES, LANES), 1)
    tiles_ref[0, 0] = _t5_bias(r - c, val)
    tiles_ref[0, 1] = _t5_bias(LANES + r - c, val)
    tt = lax.broadcasted_iota(I32, (t, ncmp), 0)
    n = lax.broadcasted_iota(I32, (t, ncmp), 1)
    bc_ref[0] = _t5_bias(tt - (n * CMP_STRIDE + CMP_BLOCK - 1), val)


def _t5_tiles(t5_table, t, ncmp):
    nh = t5_table.shape[1]
    return pl.pallas_call(
        functools.partial(_t5_tiles_kernel, t=t, ncmp=ncmp),
        grid=(nh,),
        in_specs=[pl.BlockSpec(memory_space=pltpu.SMEM)],
        out_specs=[pl.BlockSpec((1, 2, LANES, LANES), lambda i: (i, 0, 0, 0)),
                   pl.BlockSpec((1, t, ncmp), lambda i: (i, 0, 0))],
        out_shape=[jax.ShapeDtypeStruct((nh, 2, LANES, LANES), F32),
                   jax.ShapeDtypeStruct((nh, t, ncmp), F32)],
        compiler_params=_params(("arbitrary",)),
        name="t5_tiles",
    )(t5_table)


def _topk_steps(score, k):
    lane = lax.broadcasted_iota(I32, score.shape, 1).astype(F32)
    big = float(score.shape[1])
    out = []
    for n in range(k):
        m = jnp.max(score, axis=1, keepdims=True)
        idx = jnp.min(jnp.where(score == m, lane, big), axis=1, keepdims=True)
        hit = lane == idx
        out.append((n, hit, idx, m))
        score = jnp.where(hit, -jnp.inf, score)
    return out


def _topk_mask(score, k):
    sel = jnp.zeros(score.shape, F32)
    for _, hit, _, _ in _topk_steps(score, k):
        sel = jnp.where(hit, 1.0, sel)
    return sel > 0.5


def _nsa_attn_kernel(q_ref, kcmp_ref, vcmp_ref, ks_ref, vs_ref, kw_ref, vw_ref, gt_ref, bc_ref,
                     tiles_ref, ov_ref, ex_ref, tc_ref, o_ref, km_ref, *, tq, t, nkv, grp, hd):
    qi = pl.program_id(1)
    ncmp = kcmp_ref.shape[1]
    nsel = t // SEL_BLOCK
    m4 = grp * tq
    trow = qi * tq + lax.broadcasted_iota(I32, (tq, LANES), 0)
    lane = lax.broadcasted_iota(I32, (tq, LANES), 1)
    gates = gt_ref[0]

    def stack(fn):
        return jnp.concatenate([fn(g) for g in range(grp)], axis=0)

    r4 = lax.broadcasted_iota(I32, (m4, LANES), 0) % tq
    c4 = lax.broadcasted_iota(I32, (m4, LANES), 1)

    for k in range(nkv):
        q4 = stack(lambda g: q_ref[0, :, (k * grp + g) * hd:(k * grp + g + 1) * hd])
        kc = kcmp_ref[0, :, k * hd:(k + 1) * hd]
        vc = vcmp_ref[0, :, k * hd:(k + 1) * hd]
        bias_c = stack(lambda g: bc_ref[k * grp + g])
        t4 = qi * tq + r4
        mask_c = (c4 * CMP_STRIDE + CMP_BLOCK - 1) <= t4
        s = jnp.where(mask_c, _dot_nt(q4, kc) + bias_c, NEG_BIG)
        mx = jnp.max(s, axis=1, keepdims=True)
        e = jnp.where(mask_c, jnp.exp(s - mx), 0.0)
        p = e / jnp.maximum(jnp.sum(e, axis=1, keepdims=True), 1e-30)
        pb = p.astype(BF)
        o_cmp = _dot(pb, vc)
        imp4 = _dot(pb, ov_ref[...])
        imp = imp4[0:tq]
        for g in range(1, grp):
            imp = imp + imp4[g * tq:(g + 1) * tq]
        cur = trow // SEL_BLOCK
        forced = (lane == 0) | (lane == cur) | (lane == cur - 1)
        score = jnp.where(forced, FORCE_SCORE, imp)
        score = jnp.where(lane * SEL_BLOCK <= trow, score, NEG_SCORE)
        score = jnp.where(lane < nsel, score, -jnp.inf)
        sel = _topk_mask(score, min(SEL_TOP_N, nsel)) & (score > 0.5 * NEG_SCORE)
        km_ref[...] = _dot(jnp.where(sel, 1.0, 0.0).astype(BF), ex_ref[...])

        def attend(ki, carry, kref, vref, bias, mask_fn):
            m_p, l_p, acc = carry
            start = pl.multiple_of(ki * LANES, LANES)
            kt = kref[0, pl.ds(start, LANES), k * hd:(k + 1) * hd]
            vt = vref[0, pl.ds(start, LANES), k * hd:(k + 1) * hd]
            sc = _dot_nt(q4, kt) + bias
            sc = jnp.where(mask_fn(ki, start), sc, NEG_BIG)
            p_, alpha, m_n, l_n = _softmax_step(sc, m_p, l_p)
            return m_n, l_n, alpha * acc + _dot(p_.astype(BF), vt)

        def const_bias():
            return stack(lambda g: jnp.zeros((tq, LANES), F32) + tc_ref[k * grp + g])

        def tile_bias(which):
            return stack(lambda g: tiles_ref[k * grp + g, which])

        def sel_mask(ki, start):
            km = km_ref[:, pl.ds(start, LANES)]
            km4 = jnp.concatenate([km] * grp, axis=0)
            return (km4 > 0.5) & ((ki * LANES + c4) <= t4)

        def win_mask(ki, start):
            dist = t4 - (ki * LANES + c4)
            return (dist >= 0) & (dist <= WINDOW)

        init = (jnp.full((m4, 1), NEG_BIG, F32), jnp.zeros((m4, 1), F32), jnp.zeros((m4, hd), F32))

        def run(kref, vref, mask_fn, first):
            far = lax.fori_loop(first, jnp.maximum(qi - 1, first),
                                lambda ki, c: attend(ki, c, kref, vref, const_bias(), mask_fn), init)
            near = lax.cond(qi >= 1,
                            lambda c: attend(qi - 1, c, kref, vref, tile_bias(1), mask_fn),
                            lambda c: c, far)
            m_f, l_f, acc = attend(qi, near, kref, vref, tile_bias(0), mask_fn)
            return acc / l_f

        o_slc = run(ks_ref, vs_ref, sel_mask, 0)
        o_win = run(kw_ref, vw_ref, win_mask, jnp.maximum(qi - WINDOW // LANES, 0))
        for g in range(grp):
            hh = k * grp + g
            og = (gates[:, 3 * hh:3 * hh + 1] * o_cmp[g * tq:(g + 1) * tq]
                  + gates[:, 3 * hh + 1:3 * hh + 2] * o_slc[g * tq:(g + 1) * tq]
                  + gates[:, 3 * hh + 2:3 * hh + 3] * o_win[g * tq:(g + 1) * tq])
            o_ref[0, :, hh * hd:(hh + 1) * hd] = og.astype(BF)


def _nsa_attn(q, kcmp, vcmp, ksb, vsb, kwb, vwb, gates, bias_c, tiles, ov, ex, tconst, nkv, grp, hd,
              tq=128):
    b, t, wq = q.shape
    nq = t // tq
    nh = nkv * grp
    full_b = lambda a: pl.BlockSpec((1,) + a.shape[1:], lambda bi, qi: (bi,) + (0,) * (a.ndim - 1))
    return pl.pallas_call(
        functools.partial(_nsa_attn_kernel, tq=tq, t=t, nkv=nkv, grp=grp, hd=hd),
        grid=(b, nq),
        in_specs=[pl.BlockSpec((1, tq, wq), lambda bi, qi: (bi, qi, 0)),
                  full_b(kcmp), full_b(vcmp), full_b(ksb), full_b(vsb), full_b(kwb), full_b(vwb),
                  pl.BlockSpec((1, tq, LANES), lambda bi, qi: (bi, qi, 0)),
                  pl.BlockSpec((nh, tq, LANES), lambda bi, qi: (0, qi, 0)),
                  _const_spec(tiles.shape), _const_spec(ov.shape), _const_spec(ex.shape),
                  pl.BlockSpec(memory_space=pltpu.SMEM)],
        out_specs=pl.BlockSpec((1, tq, wq), lambda bi, qi: (bi, qi, 0)),
        out_shape=jax.ShapeDtypeStruct((b, t, wq), BF),
        scratch_shapes=[pltpu.VMEM((tq, t), F32)],
        compiler_params=_params(("parallel", "arbitrary")),
        name="nsa_attn",
    )(q, kcmp, vcmp, ksb, vsb, kwb, vwb, gates, bias_c, tiles, ov, ex, tconst)


def _page_copies(pt_ref, b, first_page, npages, pools, bufs, sems, layer):
    out = []
    for j in range(npages):
        page = pt_ref[b, first_page + j]
        for pool, buf, sem in zip(pools, bufs, sems):
            out.append(pltpu.make_async_copy(pool.at[layer, page], buf(j), sem))
    return out


def _paged_pipeline(pt_ref, pools, bufs_of_slot, sem_of_slot, layer, npages):
    b = pl.program_id(0)
    s = pl.program_id(1)
    nb = pl.num_programs(0)
    ns = pl.num_programs(1)
    g = b * ns + s
    slot = g % 2

    def copies(bb, ss, sl):
        return _page_copies(pt_ref, bb, ss * npages, npages, pools, bufs_of_slot(sl), sem_of_slot(sl), layer)

    @pl.when(g == 0)
    def _():
        for c in copies(b, s, slot):
            c.start()

    @pl.when(g + 1 < nb * ns)
    def _():
        wrap = s + 1 == ns
        for c in copies(jnp.where(wrap, b + 1, b), jnp.where(wrap, 0, s + 1), 1 - slot):
            c.start()

    for c in copies(b, s, slot):
        c.wait()
    return slot


def _mla_dec_kernel(pt_ref, q_ref, knew_ref, wuv_ref, ckv_hbm, kr_hbm, o_ref,
                    cbuf, rbuf, sem, m_ref, l_ref, acc_ref, *, layer, npages, nh, scale, vdim):
    s_id = pl.program_id(1)
    ns = pl.num_programs(1)
    slot = _paged_pipeline(
        pt_ref, (ckv_hbm, kr_hbm),
        lambda sl: (lambda j: cbuf.at[sl, j], lambda j: rbuf.at[sl, j]),
        lambda sl: (sem.at[0, sl], sem.at[1, sl]), layer, npages)

    @pl.when(s_id == 0)
    def _():
        m_ref[...] = jnp.full(m_ref.shape, NEG_BIG, F32)
        l_ref[...] = jnp.zeros(l_ref.shape, F32)
        acc_ref[...] = jnp.zeros(acc_ref.shape, F32)

    q = q_ref[0]
    page = cbuf.shape[2]
    kp = cbuf[slot].reshape(npages * page, cbuf.shape[3]).astype(BF)
    rp = rbuf[slot].reshape(npages * page, rbuf.shape[3]).astype(BF)
    sc = (_dot_nt(q[:, 0:128].astype(BF), kp) + _dot_nt(q[:, 128:128 + MLA_ROPE].astype(BF), rp)) * scale
    p, alpha, m_new, l_new = _softmax_step(sc, m_ref[...], l_ref[...])
    acc_ref[...] = alpha * acc_ref[...] + _dot(p.astype(BF), kp)
    m_ref[...] = m_new
    l_ref[...] = l_new

    @pl.when(s_id == ns - 1)
    def _():
        kn = knew_ref[0]
        s_new = jnp.sum(q * kn, axis=1, keepdims=True) * scale
        m_p = m_ref[...]
        m_n = jnp.maximum(m_p, s_new)
        a = jnp.exp(m_p - m_n)
        p_new = jnp.exp(s_new - m_n)
        l_f = a * l_ref[...] + p_new
        acc = a * acc_ref[...] + p_new.astype(BF).astype(F32) * kn[:, 0:128]
        olat = (acc / l_f).astype(BF)
        r = _dot(olat, wuv_ref[...])
        row = lax.broadcasted_iota(I32, r.shape, 0)
        col = lax.broadcasted_iota(I32, r.shape, 1)
        o_ref[0] = jnp.sum(jnp.where(col // vdim == row, r, 0.0), axis=0, keepdims=True).astype(BF)


def _mla_dec(page_table, q, knew, wuv_all, ckv_pool, kr_pool, layer, nh, scale, npages=16):
    db, n_pages = page_table.shape
    page = ckv_pool.shape[2]
    vdim = wuv_all.shape[1] // nh
    grid_spec = pltpu.PrefetchScalarGridSpec(
        num_scalar_prefetch=1, grid=(db, n_pages // npages),
        in_specs=[pl.BlockSpec((1, nh, 256), lambda b, s, pt: (b, 0, 0)),
                  pl.BlockSpec((1, 1, 256), lambda b, s, pt: (b, 0, 0)),
                  _const_spec(wuv_all.shape),
                  pl.BlockSpec(memory_space=pl.ANY), pl.BlockSpec(memory_space=pl.ANY)],
        out_specs=pl.BlockSpec((1, 1, wuv_all.shape[1]), lambda b, s, pt: (b, 0, 0)),
        scratch_shapes=[pltpu.VMEM((2, npages, page, ckv_pool.shape[3]), F32),
                        pltpu.VMEM((2, npages, page, kr_pool.shape[3]), F32),
                        pltpu.SemaphoreType.DMA((2, 2)),
                        pltpu.VMEM((nh, 1), F32), pltpu.VMEM((nh, 1), F32), pltpu.VMEM((nh, 128), F32)])
    return pl.pallas_call(
        functools.partial(_mla_dec_kernel, layer=layer, npages=npages, nh=nh, scale=scale, vdim=vdim),
        grid_spec=grid_spec,
        out_shape=jax.ShapeDtypeStruct((db, 1, wuv_all.shape[1]), BF),
        compiler_params=_params(("arbitrary", "arbitrary")),
        name="mla_dec",
    )(page_table, q, knew, wuv_all, ckv_pool, kr_pool)


def _fox_bias_kernel(pt_ref, lfn_ref, lf_hbm, o_ref, lbuf, sem, *, layer, n_pages):
    b = pl.program_id(0)
    page = lbuf.shape[1]

    def copy(p):
        return pltpu.make_async_copy(lf_hbm.at[layer, pt_ref[b, p]], lbuf.at[p], sem.at[0])

    def start(p, c):
        copy(p).start()
        return c

    def wait(p, c):
        copy(p).wait()
        return c

    lax.fori_loop(0, n_pages, start, 0)
    lax.fori_loop(0, n_pages, wait, 0)

    def xpose(p, c):
        o_ref[0, :, pl.ds(pl.multiple_of(p * page, page), page)] = lbuf[p].T
        return c

    lax.fori_loop(0, n_pages, xpose, 0)
    x = o_ref[0]
    n = x.shape[1]
    lane = lax.broadcasted_iota(I32, x.shape, 1)
    y = x
    k = 1
    while k < n:
        y = y + jnp.where(lane + k < n, pltpu.roll(y, n - k, axis=1), 0.0)
        k *= 2
    o_ref[0] = (y - x) + lfn_ref[0]


def _fox_bias(page_table, lf_new, lf_pool, layer):
    db, n_pages = page_table.shape
    page, nh = lf_pool.shape[2], lf_pool.shape[3]
    grid_spec = pltpu.PrefetchScalarGridSpec(
        num_scalar_prefetch=1, grid=(db,),
        in_specs=[pl.BlockSpec((1, nh, 1), lambda b, pt: (b, 0, 0)), pl.BlockSpec(memory_space=pl.ANY)],
        out_specs=pl.BlockSpec((1, nh, n_pages * page), lambda b, pt: (b, 0, 0)),
        scratch_shapes=[pltpu.VMEM((n_pages, page, nh), F32), pltpu.SemaphoreType.DMA((1,))])
    return pl.pallas_call(
        functools.partial(_fox_bias_kernel, layer=layer, n_pages=n_pages),
        grid_spec=grid_spec,
        out_shape=jax.ShapeDtypeStruct((db, nh, n_pages * page), F32),
        compiler_params=_params(("arbitrary",)),
        name="fox_bias",
    )(page_table, lf_new, lf_pool)


def _fox_dec_kernel(pt_ref, q_ref, kn_ref, vn_ref, bias_ref, k_hbm, v_hbm, o_ref,
                    kbuf, vbuf, sem, m_ref, l_ref, acc_ref, *, layer, npages, nh):
    s_id = pl.program_id(1)
    ns = pl.num_programs(1)
    rows = kbuf.shape[1] // npages
    slot = _paged_pipeline(
        pt_ref, (k_hbm, v_hbm),
        lambda sl: (lambda j: kbuf.at[sl, pl.ds(j * rows, rows), :], lambda j: vbuf.at[sl, pl.ds(j * rows, rows), :]),
        lambda sl: (sem.at[0, sl], sem.at[1, sl]), layer, npages)

    @pl.when(s_id == 0)
    def _():
        m_ref[...] = jnp.full(m_ref.shape, NEG_BIG, F32)
        l_ref[...] = jnp.zeros(l_ref.shape, F32)
        acc_ref[...] = jnp.zeros(acc_ref.shape, F32)

    q = q_ref[0]
    nkeys = kbuf.shape[1] // nh
    hrow = lax.broadcasted_iota(I32, q.shape, 0)
    sc = bias_ref[0]
    for h in range(nh):
        kh = kbuf[slot, pl.ds(h, nkeys, stride=nh), :].astype(BF)
        sc = sc + _dot_nt(jnp.where(hrow == h, q, 0.0).astype(BF), kh)
    p, alpha, m_new, l_new = _softmax_step(sc, m_ref[...], l_ref[...])
    prow = lax.broadcasted_iota(I32, p.shape, 0)
    acc = alpha * acc_ref[...]
    for h in range(nh):
        vh = vbuf[slot, pl.ds(h, nkeys, stride=nh), :].astype(BF)
        acc = acc + _dot(jnp.where(prow == h, p, 0.0).astype(BF), vh)
    acc_ref[...] = acc
    m_ref[...] = m_new
    l_ref[...] = l_new

    @pl.when(s_id == ns - 1)
    def _():
        kn = kn_ref[0].astype(BF).astype(F32)
        vn = vn_ref[0].astype(BF).astype(F32)
        s_new = jnp.sum(q * kn, axis=1, keepdims=True)
        m_p = m_ref[...]
        m_n = jnp.maximum(m_p, s_new)
        a = jnp.exp(m_p - m_n)
        p_new = jnp.exp(s_new - m_n)
        l_f = a * l_ref[...] + p_new
        o_ref[0] = ((a * acc_ref[...] + p_new.astype(BF).astype(F32) * vn) / l_f).astype(BF)


def _fox_dec(page_table, q, kn, vn, bias, k_pool, v_pool, layer, nh, npages=8):
    db, n_pages = page_table.shape
    rows, hd = k_pool.shape[2], k_pool.shape[3]
    page = rows // nh
    grid_spec = pltpu.PrefetchScalarGridSpec(
        num_scalar_prefetch=1, grid=(db, n_pages // npages),
        in_specs=[pl.BlockSpec((1, nh, hd), lambda b, s, pt: (b, 0, 0)),
                  pl.BlockSpec((1, nh, hd), lambda b, s, pt: (b, 0, 0)),
                  pl.BlockSpec((1, nh, hd), lambda b, s, pt: (b, 0, 0)),
                  pl.BlockSpec((1, nh, npages * page), lambda b, s, pt: (b, 0, s)),
                  pl.BlockSpec(memory_space=pl.ANY), pl.BlockSpec(memory_space=pl.ANY)],
        out_specs=pl.BlockSpec((1, nh, hd), lambda b, s, pt: (b, 0, 0)),
        scratch_shapes=[pltpu.VMEM((2, npages * rows, hd), F32), pltpu.VMEM((2, npages * rows, hd), F32),
                        pltpu.SemaphoreType.DMA((2, 2)),
                        pltpu.VMEM((nh, 1), F32), pltpu.VMEM((nh, 1), F32), pltpu.VMEM((nh, hd), F32)])
    return pl.pallas_call(
        functools.partial(_fox_dec_kernel, layer=layer, npages=npages, nh=nh),
        grid_spec=grid_spec,
        out_shape=jax.ShapeDtypeStruct((db, nh, hd), BF),
        compiler_params=_params(("arbitrary", "arbitrary")),
        name="fox_dec",
    )(page_table, q, kn, vn, bias, k_pool, v_pool)


def _rows(fn, n):
    return jnp.concatenate([fn(g) for g in range(n)], axis=0)


def _nsa_dec_a_kernel(pt_ref, q_ref, w1k_ref, w1v_ref, pek_ref, pev_ref, w1fk_ref, w1fv_ref, w2k_ref,
                      w2v_ref, ov_ref, tbl_ref, kc_hbm, vc_hbm, oc_ref, ti_ref, ok_ref,
                      kbuf, vbuf, sem, fsk_ref, fsv_ref, *, layer, npages, nkv, grp, hd, pos, nsel):
    s_id = pl.program_id(1)
    ns = pl.num_programs(1)
    rows = kbuf.shape[1] // npages
    slot = _paged_pipeline(
        pt_ref, (kc_hbm, vc_hbm),
        lambda sl: (lambda j: kbuf.at[sl, pl.ds(j * rows, rows), :], lambda j: vbuf.at[sl, pl.ds(j * rows, rows), :]),
        lambda sl: (sem.at[0, sl], sem.at[1, sl]), layer, npages)
    nchunk = kbuf.shape[1] // (nkv * CMP_STRIDE)
    base = pl.multiple_of(s_id * nchunk, nchunk)
    for buf, w_ref, fs_ref in ((kbuf, w1k_ref, fsk_ref), (vbuf, w1v_ref, fsv_ref)):
        for k in range(nkv):
            y = jnp.zeros((nchunk, 2 * hd), F32)
            for r in range(CMP_STRIDE):
                xr = buf[slot, pl.ds(nkv * r + k, nchunk, stride=nkv * CMP_STRIDE), :].astype(BF)
                y = y + _dot(xr, w_ref[r])
            fs_ref[k, pl.ds(base, nchunk), :] = y

    @pl.when(s_id == ns - 1)
    def _():
        ncmp = fsk_ref.shape[1]
        n_valid = (pos - (CMP_BLOCK - 1)) // CMP_STRIDE + 1
        lane = lax.broadcasted_iota(I32, (1, ncmp), 1)
        dist_c = pos - (lane * CMP_STRIDE + CMP_BLOCK - 1)
        pe_k = _dot(pek_ref[...], w1fk_ref[...])[0:1, :]
        pe_v = _dot(pev_ref[...], w1fv_ref[...])[0:1, :]
        imps = []
        for k in range(nkv):
            def cmp_rows(fs_ref, pe, w2_ref):
                fs = fs_ref[k]
                second = pltpu.roll(fs, ncmp - 1, axis=0)[:, hd:2 * hd]
                hid = _gelu(fs[:, 0:hd] + second + pe).astype(BF)
                return _dot(hid, w2_ref[...]).astype(BF)

            kcmp = cmp_rows(fsk_ref, pe_k, w2k_ref)
            vcmp = cmp_rows(fsv_ref, pe_v, w2v_ref)
            qk = q_ref[0, k * grp:(k + 1) * grp, :].astype(BF)
            bias = _rows(lambda g: _t5_bias(dist_c, lambda j: tbl_ref[j, k * grp + g]), grp)
            valid = lane < n_valid
            s = jnp.where(valid, _dot_nt(qk, kcmp) + bias, NEG_BIG)
            mx = jnp.max(s, axis=1, keepdims=True)
            e = jnp.where(valid, jnp.exp(s - mx), 0.0)
            p = (e / jnp.maximum(jnp.sum(e, axis=1, keepdims=True), 1e-30)).astype(BF)
            oc_ref[0, k * grp:(k + 1) * grp, :] = _dot(p, vcmp)
            imps.append(jnp.sum(_dot(p, ov_ref[...]), axis=0, keepdims=True))
        imp = jnp.concatenate(imps, axis=0)
        jl = lax.broadcasted_iota(I32, imp.shape, 1)
        cur = pos // SEL_BLOCK
        forced = (jl == 0) | (jl == cur) | (jl == cur - 1)
        score = jnp.where(forced, FORCE_SCORE, imp)
        score = jnp.where(jl * SEL_BLOCK <= pos, score, NEG_SCORE)
        score = jnp.where(jl < nsel, score, -jnp.inf)
        ol = lax.broadcasted_iota(I32, (nkv, LANES), 1)
        ti = jnp.zeros((nkv, LANES), I32)
        ok = jnp.zeros((nkv, LANES), I32)
        for n, _, idx, val in _topk_steps(score, min(SEL_TOP_N, nsel)):
            ti = jnp.where(ol == n, idx.astype(I32), ti)
            ok = jnp.where(ol == n, (val > 0.5 * NEG_SCORE).astype(I32), ok)
        ti_ref[0] = ti
        ok_ref[0] = ok


def _nsa_dec_a(page_table, q, w1k, w1v, pek, pev, w1fk, w1fv, w2k, w2v, ov, t5_table, kc_pool, vc_pool,
               layer, nkv, grp, hd, pos, nsel, npages=16):
    db, n_pages = page_table.shape
    rows = kc_pool.shape[2]
    page = rows // nkv
    ncmp = n_pages * page // CMP_STRIDE
    consts = [w1k, w1v, pek, pev, w1fk, w1fv, w2k, w2v, ov]
    grid_spec = pltpu.PrefetchScalarGridSpec(
        num_scalar_prefetch=1, grid=(db, n_pages // npages),
        in_specs=([pl.BlockSpec((1, nkv * grp, hd), lambda b, s, pt: (b, 0, 0))]
                  + [_const_spec(c.shape) for c in consts]
                  + [pl.BlockSpec(memory_space=pltpu.SMEM),
                     pl.BlockSpec(memory_space=pl.ANY), pl.BlockSpec(memory_space=pl.ANY)]),
        out_specs=[pl.BlockSpec((1, nkv * grp, hd), lambda b, s, pt: (b, 0, 0)),
                   pl.BlockSpec((1, nkv, LANES), lambda b, s, pt: (b, 0, 0)),
                   pl.BlockSpec((1, nkv, LANES), lambda b, s, pt: (b, 0, 0))],
        scratch_shapes=[pltpu.VMEM((2, npages * rows, hd), F32), pltpu.VMEM((2, npages * rows, hd), F32),
                        pltpu.SemaphoreType.DMA((2, 2)),
                        pltpu.VMEM((nkv, ncmp, 2 * hd), F32), pltpu.VMEM((nkv, ncmp, 2 * hd), F32)])
    return pl.pallas_call(
        functools.partial(_nsa_dec_a_kernel, layer=layer, npages=npages, nkv=nkv, grp=grp, hd=hd,
                          pos=pos, nsel=nsel),
        grid_spec=grid_spec,
        out_shape=[jax.ShapeDtypeStruct((db, nkv * grp, hd), F32),
                   jax.ShapeDtypeStruct((db, nkv, LANES), I32),
                   jax.ShapeDtypeStruct((db, nkv, LANES), I32)],
        compiler_params=_params(("arbitrary", "arbitrary")),
        name="nsa_dec_cmp",
    )(page_table, q, *consts, t5_table, kc_pool, vc_pool)


def _nsa_dec_b_kernel(pt_ref, ti_ref, ok_ref, q_ref, ksn_ref, vsn_ref, kwn_ref, vwn_ref, wk_ref, wv_ref,
                      gt_ref, oc_ref, tbl_ref, ks_hbm, vs_hbm, o_ref, kb, vb, sem,
                      *, layer, nkv, grp, hd, pos, npast_sel, blk_rows, blocks_per_page, nwin):
    b = pl.program_id(0)
    ntop = kb.shape[1] // blk_rows

    def sel_dma(k, n, wait):
        blk = ti_ref[b, k * ntop + n]

        @pl.when(blk < npast_sel)
        def _():
            pg = pt_ref[b, blk // blocks_per_page]
            off = pl.multiple_of((blk % blocks_per_page) * blk_rows, blk_rows)
            for pool, buf, sm in ((ks_hbm, kb, sem.at[0]), (vs_hbm, vb, sem.at[1])):
                c = pltpu.make_async_copy(pool.at[layer, pg, pl.ds(off, blk_rows), :],
                                          buf.at[k, pl.ds(n * blk_rows, blk_rows), :], sm)
                if wait:
                    c.wait()
                else:
                    c.start()

        if not wait:
            @pl.when(blk >= npast_sel)
            def _():
                for buf, new_ref in ((kb, ksn_ref), (vb, vsn_ref)):
                    buf[k, pl.ds(n * blk_rows, blk_rows), :] = jnp.zeros((blk_rows, hd), F32)
                    buf[k, pl.ds(n * blk_rows + k, 1), :] = new_ref[0, k:k + 1, :]

    for k in range(nkv):
        for n in range(ntop):
            sel_dma(k, n, False)

    def head_col(fn):
        r = lax.broadcasted_iota(I32, (grp, 1), 0)
        out = jnp.zeros((grp, 1), F32)
        for g in range(grp):
            out = jnp.where(r == g, fn(g), out)
        return out

    def bf_round(x):
        return x.astype(BF).astype(F32)

    o_wins = []
    jw = lax.broadcasted_iota(I32, (1, nwin), 1)
    dist_w = nwin - jw
    for k in range(nkv):
        qk = q_ref[0, k * grp:(k + 1) * grp, :]
        kw = wk_ref[0, 0, pl.ds(k, nwin, stride=nkv), :].astype(BF)
        vw = wv_ref[0, 0, pl.ds(k, nwin, stride=nkv), :].astype(BF)
        bias = _rows(lambda g: _t5_bias(dist_w, lambda j: tbl_ref[j, k * grp + g]), grp)
        s = _dot_nt(qk.astype(BF), kw) + bias
        s_new = (jnp.sum(qk * bf_round(kwn_ref[0, k:k + 1, :]), axis=1, keepdims=True)
                 + head_col(lambda g: tbl_ref[0, k * grp + g]))
        mx = jnp.maximum(jnp.max(s, axis=1, keepdims=True), s_new)
        e = jnp.exp(s - mx)
        e_new = jnp.exp(s_new - mx)
        l = jnp.sum(e, axis=1, keepdims=True) + e_new
        o_wins.append((_dot(e.astype(BF), vw) + bf_round(e_new) * bf_round(vwn_ref[0, k:k + 1, :])) / l)

    for k in range(nkv):
        for n in range(ntop):
            sel_dma(k, n, True)

    nkeys = ntop * SEL_BLOCK
    lane = lax.broadcasted_iota(I32, (1, nkeys), 1)
    slot_id = lane // SEL_BLOCK
    for k in range(nkv):
        qk = q_ref[0, k * grp:(k + 1) * grp, :]
        ks = kb[k, pl.ds(k, nkeys, stride=nkv), :].astype(BF)
        vs = vb[k, pl.ds(k, nkeys, stride=nkv), :].astype(BF)
        blkv = jnp.zeros((1, nkeys), I32)
        okv = jnp.zeros((1, nkeys), I32)
        for n in range(ntop):
            blkv = jnp.where(slot_id == n, ti_ref[b, k * ntop + n], blkv)
            okv = jnp.where(slot_id == n, ok_ref[b, k * ntop + n], okv)
        dist = pos - (blkv * SEL_BLOCK + lane % SEL_BLOCK)
        mask = (okv > 0) & (dist >= 0)
        bias = _rows(lambda g: _t5_bias(dist, lambda j: tbl_ref[j, k * grp + g]), grp)
        s = jnp.where(mask, _dot_nt(qk.astype(BF), ks) + bias, NEG_BIG)
        mx = jnp.max(s, axis=1, keepdims=True)
        e = jnp.where(mask, jnp.exp(s - mx), 0.0)
        l = jnp.maximum(jnp.sum(e, axis=1, keepdims=True), 1e-30)
        o_slc = _dot(e.astype(BF), vs) / l
        gt = gt_ref[0, k * grp:(k + 1) * grp, :]
        o = (gt[:, 0:1] * oc_ref[0, k * grp:(k + 1) * grp, :] + gt[:, 1:2] * o_slc + gt[:, 2:3] * o_wins[k])
        o_ref[0, k * grp:(k + 1) * grp, :] = o.astype(BF)


def _nsa_dec_b(page_table, top_i, top_ok, q, ksn, vsn, kwn, vwn, win_k, win_v, gates, o_cmp, t5_table,
               ks_pool, vs_pool, layer, nkv, grp, hd, pos, npast_sel):
    db = q.shape[0]
    nh = nkv * grp
    ntop = top_i.shape[1] // nkv
    page_rows = ks_pool.shape[2]
    blk_rows = SEL_BLOCK * nkv
    nwin = win_k.shape[2] // nkv
    m3 = lambda b, *_: (b, 0, 0)
    grid_spec = pltpu.PrefetchScalarGridSpec(
        num_scalar_prefetch=3, grid=(db,),
        in_specs=[pl.BlockSpec((1, nh, hd), m3)] + [pl.BlockSpec((1, nkv, hd), m3)] * 4
                 + [pl.BlockSpec((1, 1) + win_k.shape[2:], lambda b, *_: (layer, b, 0, 0))] * 2
                 + [pl.BlockSpec((1, nh, 3), m3), pl.BlockSpec((1, nh, hd), m3),
                    pl.BlockSpec(memory_space=pltpu.SMEM),
                    pl.BlockSpec(memory_space=pl.ANY), pl.BlockSpec(memory_space=pl.ANY)],
        out_specs=pl.BlockSpec((1, nh, hd), m3),
        scratch_shapes=[pltpu.VMEM((nkv, ntop * blk_rows, hd), F32),
                        pltpu.VMEM((nkv, ntop * blk_rows, hd), F32),
                        pltpu.SemaphoreType.DMA((2,))])
    return pl.pallas_call(
        functools.partial(_nsa_dec_b_kernel, layer=layer, nkv=nkv, grp=grp, hd=hd, pos=pos,
                          npast_sel=npast_sel, blk_rows=blk_rows,
                          blocks_per_page=page_rows // blk_rows, nwin=nwin),
        grid_spec=grid_spec,
        out_shape=jax.ShapeDtypeStruct((db, nh, hd), BF),
        compiler_params=_params(("arbitrary",)),
        name="nsa_dec_sel_win",
    )(page_table, top_i, top_ok, q, ksn, vsn, kwn, vwn, win_k, win_v, gates, o_cmp, t5_table,
      ks_pool, vs_pool)


def _step_pipeline(copies):
    b = pl.program_id(0)
    s = pl.program_id(1)
    nb = pl.num_programs(0)
    ns = pl.num_programs(1)
    g = b * ns + s
    slot = g % 2

    @pl.when(g == 0)
    def _():
        for c in copies(b, s, slot):
            c.start()

    @pl.when(g + 1 < nb * ns)
    def _():
        wrap = s + 1 == ns
        for c in copies(jnp.where(wrap, b + 1, b), jnp.where(wrap, 0, s + 1), 1 - slot):
            c.start()

    for c in copies(b, s, slot):
        c.wait()
    return slot


def _diag_rows(r, width):
    row = lax.broadcasted_iota(I32, r.shape, 0)
    col = lax.broadcasted_iota(I32, r.shape, 1)
    return jnp.sum(jnp.where(col // width == row, r, 0.0), axis=0, keepdims=True)


def _bf_round(x):
    return x.astype(BF).astype(F32)


def _mla_dec2_kernel(pt_ref, q_ref, knew_ref, wuv_ref, ckv_hbm, krt_hbm, o_ref,
                     cbuf, rbuf, sem, m_ref, l_ref, acc_ref, *, layer, npages, scale, vdim):
    s_id = pl.program_id(1)
    ns = pl.num_programs(1)
    page = cbuf.shape[2]

    def copies(bb, ss, sl):
        out = []
        for j in range(npages):
            pg = pt_ref[bb, ss * npages + j]
            out.append(pltpu.make_async_copy(ckv_hbm.at[layer, pg], cbuf.at[sl, j], sem.at[0, sl]))
            out.append(pltpu.make_async_copy(krt_hbm.at[layer, pg], rbuf.at[sl, :, pl.ds(j * page, page)],
                                             sem.at[1, sl]))
        return out

    slot = _step_pipeline(copies)

    @pl.when(s_id == 0)
    def _():
        m_ref[...] = jnp.full(m_ref.shape, NEG_BIG, F32)
        l_ref[...] = jnp.zeros(l_ref.shape, F32)
        acc_ref[...] = jnp.zeros(acc_ref.shape, F32)

    q = q_ref[0]
    kp = cbuf[slot].reshape(npages * page, cbuf.shape[3]).astype(BF)
    sc = (_dot_nt(q[:, 0:128].astype(BF), kp)
          + _dot(q[:, 128:128 + MLA_ROPE].astype(BF), rbuf[slot].astype(BF))) * scale
    p, alpha, m_new, l_new = _softmax_step(sc, m_ref[...], l_ref[...])
    acc_ref[...] = alpha * acc_ref[...] + _dot(p.astype(BF), kp)
    m_ref[...] = m_new
    l_ref[...] = l_new

    @pl.when(s_id == ns - 1)
    def _():
        kn = knew_ref[0]
        s_new = jnp.sum(q * kn, axis=1, keepdims=True) * scale
        m_p = m_ref[...]
        m_n = jnp.maximum(m_p, s_new)
        a = jnp.exp(m_p - m_n)
        p_new = jnp.exp(s_new - m_n)
        l_f = a * l_ref[...] + p_new
        acc = a * acc_ref[...] + _bf_round(p_new) * kn[:, 0:128]
        olat = (acc / l_f).astype(BF)
        o_ref[0] = _diag_rows(_dot(olat, wuv_ref[...]), vdim).astype(BF)


def _mla_dec2(page_table, q, knew, wuv_all, ckv_pool, krt_pool, layer, nh, scale, npages=16):
    db, n_pages = page_table.shape
    page = ckv_pool.shape[2]
    vdim = wuv_all.shape[1] // nh
    grid_spec = pltpu.PrefetchScalarGridSpec(
        num_scalar_prefetch=1, grid=(db, n_pages // npages),
        in_specs=[pl.BlockSpec((1, nh, 256), lambda b, s, pt: (b, 0, 0)),
                  pl.BlockSpec((1, 1, 256), lambda b, s, pt: (b, 0, 0)),
                  _const_spec(wuv_all.shape),
                  pl.BlockSpec(memory_space=pl.ANY), pl.BlockSpec(memory_space=pl.ANY)],
        out_specs=pl.BlockSpec((1, 1, wuv_all.shape[1]), lambda b, s, pt: (b, 0, 0)),
        scratch_shapes=[pltpu.VMEM((2, npages, page, ckv_pool.shape[3]), F32),
                        pltpu.VMEM((2, krt_pool.shape[2], npages * page), F32),
                        pltpu.SemaphoreType.DMA((2, 2)),
                        pltpu.VMEM((nh, 1), F32), pltpu.VMEM((nh, 1), F32), pltpu.VMEM((nh, 128), F32)])
    return pl.pallas_call(
        functools.partial(_mla_dec2_kernel, layer=layer, npages=npages, scale=scale, vdim=vdim),
        grid_spec=grid_spec,
        out_shape=jax.ShapeDtypeStruct((db, 1, wuv_all.shape[1]), BF),
        compiler_params=_params(("arbitrary", "arbitrary")),
        name="mla_dec",
    )(page_table, q, knew, wuv_all, ckv_pool, krt_pool)


def _suffix_sum_lanes(x):
    n = x.shape[1]
    lane = lax.broadcasted_iota(I32, x.shape, 1)
    y = x
    k = 1
    while k < n:
        y = y + jnp.where(lane + k < n, pltpu.roll(y, n - k, axis=1), 0.0)
        k *= 2
    return y


def _fox_dec2_kernel(pt_ref, q_ref, kn_ref, vn_ref, lfn_ref, kt_hbm, vt_hbm, lft_hbm, o_ref,
                     kbuf, vbuf, lbuf, sem, m_ref, l_ref, acc_ref, carry_ref, *, layer, npages, hd):
    s_id = pl.program_id(1)
    ns = pl.num_programs(1)
    page = kbuf.shape[2] // npages

    def copies(bb, ss, sl):
        out = []
        first = (ns - 1 - ss) * npages
        for j in range(npages):
            pg = pt_ref[bb, first + j]
            win = pl.ds(j * page, page)
            out.append(pltpu.make_async_copy(kt_hbm.at[layer, pg], kbuf.at[sl, :, win], sem.at[0, sl]))
            out.append(pltpu.make_async_copy(vt_hbm.at[layer, pg], vbuf.at[sl, :, win], sem.at[1, sl]))
            out.append(pltpu.make_async_copy(lft_hbm.at[layer, pg], lbuf.at[sl, :, win], sem.at[2, sl]))
        return out

    slot = _step_pipeline(copies)

    @pl.when(s_id == 0)
    def _():
        m_ref[...] = jnp.full(m_ref.shape, NEG_BIG, F32)
        l_ref[...] = jnp.zeros(l_ref.shape, F32)
        acc_ref[...] = jnp.zeros(acc_ref.shape, F32)
        carry_ref[...] = lfn_ref[0]

    q = q_ref[0]
    lf = lbuf[slot]
    incl = _suffix_sum_lanes(lf)
    carry = carry_ref[...]
    sc = _dot(q.astype(BF), kbuf[slot].astype(BF)) + ((incl - lf) + carry)
    carry_ref[...] = carry + incl[:, 0:1]
    p, alpha, m_new, l_new = _softmax_step(sc, m_ref[...], l_ref[...])
    acc_ref[...] = alpha * acc_ref[...] + _dot_nt(p.astype(BF), vbuf[slot].astype(BF))
    m_ref[...] = m_new
    l_ref[...] = l_new

    @pl.when(s_id == ns - 1)
    def _():
        s_new = jnp.sum(q * _bf_round(kn_ref[0]), axis=1, keepdims=True)
        m_p = m_ref[...]
        m_n = jnp.maximum(m_p, s_new)
        a = jnp.exp(m_p - m_n)
        p_new = jnp.exp(s_new - m_n)
        l_f = a * l_ref[...] + p_new
        acc = a * acc_ref[...] + _bf_round(p_new) * _bf_round(vn_ref[0])
        o_ref[0] = _diag_rows(acc / l_f, hd).astype(BF)


def _fox_dec2(page_table, qblk, kn, vn, lf_new, kt_pool, vt_pool, lft_pool, layer, nh, hd, npages=8):
    db, n_pages = page_table.shape
    page = kt_pool.shape[3]
    w = nh * hd
    m3 = lambda b, s, pt: (b, 0, 0)
    grid_spec = pltpu.PrefetchScalarGridSpec(
        num_scalar_prefetch=1, grid=(db, n_pages // npages),
        in_specs=[pl.BlockSpec((1, nh, w), m3), pl.BlockSpec((1, 1, w), m3), pl.BlockSpec((1, 1, w), m3),
                  pl.BlockSpec((1, nh, 1), m3),
                  pl.BlockSpec(memory_space=pl.ANY), pl.BlockSpec(memory_space=pl.ANY),
                  pl.BlockSpec(memory_space=pl.ANY)],
        out_specs=pl.BlockSpec((1, 1, w), m3),
        scratch_shapes=[pltpu.VMEM((2, w, npages * page), F32), pltpu.VMEM((2, w, npages * page), F32),
                        pltpu.VMEM((2, nh, npages * page), F32), pltpu.SemaphoreType.DMA((3, 2)),
                        pltpu.VMEM((nh, 1), F32), pltpu.VMEM((nh, 1), F32), pltpu.VMEM((nh, w), F32),
                        pltpu.VMEM((nh, 1), F32)])
    return pl.pallas_call(
        functools.partial(_fox_dec2_kernel, layer=layer, npages=npages, hd=hd),
        grid_spec=grid_spec,
        out_shape=jax.ShapeDtypeStruct((db, 1, w), BF),
        compiler_params=_params(("arbitrary", "arbitrary")),
        name="fox_dec",
    )(page_table, qblk, kn, vn, lf_new, kt_pool, vt_pool, lft_pool)


def _head_rows(fn, nh):
    return jnp.concatenate([fn(h) for h in range(nh)], axis=0)


def _nsa_cmp_dec_kernel(pt_ref, q_ref, perm_ref, wk_ref, wv_ref, pek_ref, pev_ref, w1k_ref, w1v_ref,
                        w2k_ref, w2v_ref, ov_ref, tbl_ref, kc_hbm, vc_hbm, oc_ref, ti_ref, ok_ref,
                        kbuf, vbuf, sem, xpk_ref, xpv_ref, fsk_ref, fsv_ref,
                        *, layer, npages, nkv, grp, pos, nsel):
    s_id = pl.program_id(1)
    ns = pl.num_programs(1)
    width = kbuf.shape[2]
    per_page = kbuf.shape[3] // CMP_STRIDE
    nchunk = npages * per_page

    def copies(bb, ss, sl):
        out = []
        for j in range(npages):
            pg = pt_ref[bb, ss * npages + j]
            out.append(pltpu.make_async_copy(kc_hbm.at[layer, pg], kbuf.at[sl, j], sem.at[0, sl]))
            out.append(pltpu.make_async_copy(vc_hbm.at[layer, pg], vbuf.at[sl, j], sem.at[1, sl]))
        return out

    slot = _step_pipeline(copies)
    base = pl.multiple_of(s_id * nchunk, nchunk)
    for buf, xp_ref, w_ref, fs_ref in ((kbuf, xpk_ref, wk_ref, fsk_ref), (vbuf, xpv_ref, wv_ref, fsv_ref)):
        for j in range(npages):
            xp = _dot_nt(perm_ref[...], buf[slot, j].astype(BF))
            for r in range(CMP_STRIDE):
                xp_ref[r, j * per_page:(j + 1) * per_page, :] = xp[r * per_page:(r + 1) * per_page, :]
        y = jnp.zeros((nchunk, 2 * width), F32)
        for r in range(CMP_STRIDE):
            y = y + _dot(xp_ref[r].astype(BF), w_ref[r])
        fs_ref[pl.ds(base, nchunk), :] = y

    @pl.when(s_id == ns - 1)
    def _():
        ncmp = fsk_ref.shape[0]
        nh = nkv * grp
        n_valid = (pos - (CMP_BLOCK - 1)) // CMP_STRIDE + 1
        lane = lax.broadcasted_iota(I32, (1, ncmp), 1)
        dist_c = pos - (lane * CMP_STRIDE + CMP_BLOCK - 1)
        kcmp = _compress_tail(fsk_ref[...], pek_ref, w1k_ref, w2k_ref, width)
        vcmp = _compress_tail(fsv_ref[...], pev_ref, w1v_ref, w2v_ref, width)
        bias = _head_rows(lambda h: _t5_bias(dist_c, lambda j: tbl_ref[j, h]), nh)
        valid = lane < n_valid
        s = jnp.where(valid, _dot_nt(q_ref[0].astype(BF), kcmp) + bias, NEG_BIG)
        mx = jnp.max(s, axis=1, keepdims=True)
        e = jnp.where(valid, jnp.exp(s - mx), 0.0)
        p = (e / jnp.maximum(jnp.sum(e, axis=1, keepdims=True), 1e-30)).astype(BF)
        oc_ref[0] = _dot(p, vcmp)
        imp_h = _dot(p, ov_ref[...])
        imp = _head_rows(lambda k: sum(imp_h[k * grp + g:k * grp + g + 1, :] for g in range(grp)), nkv)
        jl = lax.broadcasted_iota(I32, imp.shape, 1)
        cur = pos // SEL_BLOCK
        forced = (jl == 0) | (jl == cur) | (jl == cur - 1)
        score = jnp.where(forced, FORCE_SCORE, imp)
        score = jnp.where(jl * SEL_BLOCK <= pos, score, NEG_SCORE)
        score = jnp.where(jl < nsel, score, -jnp.inf)
        ol = lax.broadcasted_iota(I32, (nkv, LANES), 1)
        ti = jnp.zeros((nkv, LANES), I32)
        ok = jnp.zeros((nkv, LANES), I32)
        for n, _, idx, val in _topk_steps(score, min(SEL_TOP_N, nsel)):
            ti = jnp.where(ol == n, idx.astype(I32), ti)
            ok = jnp.where(ol == n, (val > 0.5 * NEG_SCORE).astype(I32), ok)
        ti_ref[0] = ti
        ok_ref[0] = ok


def _nsa_cmp_dec(page_table, qblk, perm, wk, wv, pek, pev, w1k, w1v, w2k, w2v, ov, t5_table,
                 kc_pool, vc_pool, layer, nkv, grp, pos, nsel, npages=16):
    db, n_pages = page_table.shape
    width, page = kc_pool.shape[2], kc_pool.shape[3]
    nh = nkv * grp
    ncmp = n_pages * page // CMP_STRIDE
    nchunk = npages * page // CMP_STRIDE
    consts = [perm, wk, wv, pek, pev, w1k, w1v, w2k, w2v, ov]
    m3 = lambda b, s, pt: (b, 0, 0)
    grid_spec = pltpu.PrefetchScalarGridSpec(
        num_scalar_prefetch=1, grid=(db, n_pages // npages),
        in_specs=([pl.BlockSpec((1, nh, width), m3)] + [_const_spec(c.shape) for c in consts]
                  + [pl.BlockSpec(memory_space=pltpu.SMEM),
                     pl.BlockSpec(memory_space=pl.ANY), pl.BlockSpec(memory_space=pl.ANY)]),
        out_specs=[pl.BlockSpec((1, nh, width), m3), pl.BlockSpec((1, nkv, LANES), m3),
                   pl.BlockSpec((1, nkv, LANES), m3)],
        scratch_shapes=[pltpu.VMEM((2, npages, width, page), F32), pltpu.VMEM((2, npages, width, page), F32),
                        pltpu.SemaphoreType.DMA((2, 2)),
                        pltpu.VMEM((CMP_STRIDE, nchunk, width), F32), pltpu.VMEM((CMP_STRIDE, nchunk, width), F32),
                        pltpu.VMEM((ncmp, 2 * width), F32), pltpu.VMEM((ncmp, 2 * width), F32)])
    return pl.pallas_call(
        functools.partial(_nsa_cmp_dec_kernel, layer=layer, npages=npages, nkv=nkv, grp=grp, pos=pos, nsel=nsel),
        grid_spec=grid_spec,
        out_shape=[jax.ShapeDtypeStruct((db, nh, width), F32),
                   jax.ShapeDtypeStruct((db, nkv, LANES), I32),
                   jax.ShapeDtypeStruct((db, nkv, LANES), I32)],
        compiler_params=_params(("arbitrary", "arbitrary")),
        name="nsa_dec_cmp",
    )(page_table, qblk, *consts, t5_table, kc_pool, vc_pool)


def _nsa_sel_win_dec_kernel(pt_ref, ti_ref, ok_ref, q_ref, ksn_ref, vsn_ref, kwn_ref, vwn_ref, wk_ref, wv_ref,
                            gt_ref, oc_ref, tbl_ref, ks_hbm, vs_hbm, o_ref, kb, vb, sem,
                            *, layer, nkv, grp, pos, npast_sel, page, nwin):
    b = pl.program_id(0)
    nh = nkv * grp
    ntop = kb.shape[2] // page
    bpp = page // SEL_BLOCK

    def sel_dma(k, n, wait):
        blk = ti_ref[b, k * ntop + n]
        win = pl.ds(n * page, page)

        @pl.when(blk < npast_sel)
        def _():
            pg = pt_ref[b, blk // bpp]
            for pool, buf, sm in ((ks_hbm, kb, sem.at[0]), (vs_hbm, vb, sem.at[1])):
                c = pltpu.make_async_copy(pool.at[layer, pg], buf.at[k, :, win], sm)
                if wait:
                    c.wait()
                else:
                    c.start()

        if not wait:
            @pl.when(blk >= npast_sel)
            def _():
                kb[k, :, win] = jnp.zeros((kb.shape[1], page), F32)
                vb[k, :, win] = jnp.zeros((vb.shape[1], page), F32)

    for k in range(nkv):
        for n in range(ntop):
            sel_dma(k, n, False)

    q = q_ref[0]
    qb = q.astype(BF)
    tbl0 = _head_rows(lambda h: jnp.zeros((1, 1), F32) + tbl_ref[0, h], nh)

    jw = lax.broadcasted_iota(I32, (1, nwin), 1)
    bias_w = _head_rows(lambda h: _t5_bias(nwin - jw, lambda j: tbl_ref[j, h]), nh)
    s = _dot(qb, wk_ref[0, 0].astype(BF)) + bias_w
    s_new = jnp.sum(q * _bf_round(kwn_ref[0]), axis=1, keepdims=True) + tbl0
    mx = jnp.maximum(jnp.max(s, axis=1, keepdims=True), s_new)
    e = jnp.exp(s - mx)
    e_new = jnp.exp(s_new - mx)
    l = jnp.sum(e, axis=1, keepdims=True) + e_new
    o_win = (_dot_nt(e.astype(BF), wv_ref[0, 0].astype(BF)) + _bf_round(e_new) * _bf_round(vwn_ref[0])) / l

    for k in range(nkv):
        for n in range(ntop):
            sel_dma(k, n, True)

    nkeys = ntop * page
    lane = lax.broadcasted_iota(I32, (1, nkeys), 1)
    slot_id = lane // page
    row = lane % page
    gates = gt_ref[0]
    for k in range(nkv):
        rows = slice(k * grp, (k + 1) * grp)
        blkv = jnp.zeros((1, nkeys), I32)
        okv = jnp.zeros((1, nkeys), I32)
        new_sel = jnp.int32(0)
        for n in range(ntop):
            blk = ti_ref[b, k * ntop + n]
            okn = ok_ref[b, k * ntop + n]
            blkv = jnp.where(slot_id == n, blk, blkv)
            okv = jnp.where(slot_id == n, okn, okv)
            new_sel = new_sel + jnp.where((blk >= npast_sel) & (okn > 0), 1, 0)
        kpos = (blkv // bpp) * page + row
        dist = pos - kpos
        mask = (okv > 0) & (blkv < npast_sel) & (row // SEL_BLOCK == blkv % bpp) & (dist >= 0)
        bias = _head_rows(lambda g: _t5_bias(dist, lambda j: tbl_ref[j, k * grp + g]), grp)
        sk = jnp.where(mask, _dot(qb[rows], kb[k].astype(BF)) + bias, NEG_BIG)
        sk_new = jnp.sum(q[rows] * _bf_round(ksn_ref[0]), axis=1, keepdims=True) + tbl0[rows]
        sk_new = jnp.where(new_sel > 0, sk_new, NEG_BIG)
        mx = jnp.maximum(jnp.max(sk, axis=1, keepdims=True), sk_new)
        e = jnp.where(mask, jnp.exp(sk - mx), 0.0)
        e_new = jnp.where(new_sel > 0, jnp.exp(sk_new - mx), 0.0)
        l = jnp.maximum(jnp.sum(e, axis=1, keepdims=True) + e_new, 1e-30)
        o_slc = (_dot_nt(e.astype(BF), vb[k].astype(BF)) + _bf_round(e_new) * _bf_round(vsn_ref[0])) / l
        g3 = gates[rows]
        o_ref[0, rows, :] = g3[:, 0:1] * oc_ref[0, rows, :] + g3[:, 1:2] * o_slc + g3[:, 2:3] * o_win[rows]


def _nsa_sel_win_dec(page_table, top_i, top_ok, qblk, ksn, vsn, kwn, vwn, win_kt, win_vt, gates, o_cmp,
                     t5_table, ks_pool, vs_pool, layer, nkv, grp, pos, npast_sel):
    db = qblk.shape[0]
    nh = nkv * grp
    ntop = top_i.shape[1] // nkv
    width, page = ks_pool.shape[2], ks_pool.shape[3]
    nwin = win_kt.shape[3]
    m3 = lambda b, *_: (b, 0, 0)
    grid_spec = pltpu.PrefetchScalarGridSpec(
        num_scalar_prefetch=3, grid=(db,),
        in_specs=[pl.BlockSpec((1, nh, width), m3)] + [pl.BlockSpec((1, 1, width), m3)] * 4
                 + [pl.BlockSpec((1, 1, width, nwin), lambda b, *_: (layer, b, 0, 0))] * 2
                 + [pl.BlockSpec((1, nh, 3), m3), pl.BlockSpec((1, nh, width), m3),
                    pl.BlockSpec(memory_space=pltpu.SMEM),
                    pl.BlockSpec(memory_space=pl.ANY), pl.BlockSpec(memory_space=pl.ANY)],
        out_specs=pl.BlockSpec((1, nh, width), m3),
        scratch_shapes=[pltpu.VMEM((nkv, width, ntop * page), F32), pltpu.VMEM((nkv, width, ntop * page), F32),
                        pltpu.SemaphoreType.DMA((2,))])
    return pl.pallas_call(
        functools.partial(_nsa_sel_win_dec_kernel, layer=layer, nkv=nkv, grp=grp, pos=pos,
                          npast_sel=npast_sel, page=page, nwin=nwin),
        grid_spec=grid_spec,
        out_shape=jax.ShapeDtypeStruct((db, nh, width), F32),
        compiler_params=_params(("arbitrary",)),
        name="nsa_dec_sel_win",
    )(page_table, top_i, top_ok, qblk, ksn, vsn, kwn, vwn, win_kt, win_vt, gates, o_cmp, t5_table,
      ks_pool, vs_pool)


def _pad_last(a, n):
    return jnp.pad(a, [(0, 0)] * (a.ndim - 1) + [(0, n - a.shape[-1])])


def _swap_halves(a):
    h = a.shape[-1] // 2
    return jnp.concatenate([a[..., h:], a[..., :h]], axis=-1)


def _split_cols(a, sizes):
    out, start = [], 0
    for s in sizes:
        out.append(a[..., start:start + s])
        start += s
    return out


def _block_diag(blocks):
    return jax.scipy.linalg.block_diag(*blocks)


def _rope_tables(pos):
    half = MLA_ROPE // 2
    inv = ROPE_THETA ** (-jnp.arange(half, dtype=F32) / half)
    ang = pos.astype(F32)[:, None] * inv[None, :]
    cos, sin = jnp.cos(ang), jnp.sin(ang)
    z = jnp.zeros((pos.shape[0], LANES - MLA_ROPE), F32)
    return jnp.concatenate([cos, cos, z], axis=1), jnp.concatenate([-sin, sin, z], axis=1)


def _overlap_matrix(ncmp, nsel, ncols):
    c0 = np.arange(ncmp)[:, None] * CMP_STRIDE
    s0 = np.arange(ncols)[None, :] * SEL_BLOCK
    ov = np.clip(np.minimum(c0 + CMP_BLOCK, s0 + SEL_BLOCK) - np.maximum(c0, s0), 0, None) / CMP_BLOCK
    ov = np.where(np.arange(ncols)[None, :] < nsel, ov, 0.0)
    return jnp.asarray(ov, dtype=BF)


def kernel(x_prompt, x_sample, cache_mla_ckv, cache_mla_krope, cache_fox_k, cache_fox_v, cache_fox_logf,
           cache_nsa_kcmp, cache_nsa_vcmp, cache_nsa_kslc, cache_nsa_vslc, state_nsa_kwin, state_nsa_vwin,
           page_table, norm_attn, norm_ffn, norm_final, ab_w_in, ab_fox_bf, ab_mla_gq, ab_mla_gkv,
           ab_mla_wuq, ab_mla_wuk, ab_mla_wuv, ab_w_out, c_w_in, c_pe_k, c_w1_k, c_w2_k, c_pe_v, c_w1_v,
           c_w2_v, c_w_out, t5_table, ffn_w_gate, ffn_w_up, ffn_w_down):
    bsz, t, d = x_prompt.shape
    db, t_s, _ = x_sample.shape
    depth = norm_attn.shape[0]
    n_pool, page = cache_mla_ckv.shape[1], cache_mla_ckv.shape[2]
    n_pages = page_table.shape[1]
    past_len = n_pages * page
    q_rank, mla_h, _ = ab_mla_wuq.shape[1:]
    kv_rank = ab_mla_wuk.shape[1]
    mla_v = ab_mla_wuv.shape[3]
    fox_h = ab_fox_bf.shape[1]
    fox_d = cache_fox_k.shape[4]
    nkv, hd = cache_nsa_kcmp.shape[3], cache_nsa_kcmp.shape[4]
    nsa_h = t5_table.shape[1]
    grp = nsa_h // nkv
    nbuf = state_nsa_kwin.shape[2]
    assert t_s == 1 and (q_rank, kv_rank, fox_h * fox_d) == (256, 128, 512)
    assert past_len % CMP_STRIDE == 0 and nbuf <= WINDOW and past_len >= nbuf
    assert t % 256 == 0 and t >= WINDOW and LANES * CMP_STRIDE == t
    mla_scale = (MLA_NOPE + MLA_ROPE) ** -0.5
    nsa_scale = hd ** -0.5
    m_p = bsz * t
    tm = 512

    pos_p = jnp.arange(t)
    pos_s = jnp.tile(past_len + jnp.arange(t_s), db)
    cos_p, sin_p = _rope_tables(pos_p)
    cos_s, sin_s = _rope_tables(pos_s)

    xp = x_prompt.reshape(m_p, d)
    xs = x_sample.reshape(db * t_s, d)
    tiles, bias_c = _t5_tiles(t5_table, t, t // CMP_STRIDE)
    ov_p = _overlap_matrix(t // CMP_STRIDE, t // SEL_BLOCK, LANES)
    ex_p = jnp.asarray(np.arange(LANES)[:, None] == (np.arange(t)[None, :] // SEL_BLOCK), dtype=BF)
    nsel_s = -(-(past_len + t_s) // SEL_BLOCK)
    ov_s = _overlap_matrix(past_len // CMP_STRIDE, nsel_s, -(-nsel_s // LANES) * LANES)
    tconst = t5_table[T5_BUCKETS - 1]

    def feat_major(a):
        a = jnp.moveaxis(a, 2, -1)
        return a.reshape(a.shape[0], a.shape[1], -1, a.shape[-1])

    krt_pool = feat_major(cache_mla_krope)
    fkt_pool, fvt_pool, lft_pool = feat_major(cache_fox_k), feat_major(cache_fox_v), feat_major(cache_fox_logf)
    kc_pool, vc_pool = feat_major(cache_nsa_kcmp), feat_major(cache_nsa_vcmp)
    ks_pool, vs_pool = feat_major(cache_nsa_kslc), feat_major(cache_nsa_vslc)
    win_kt, win_vt = feat_major(state_nsa_kwin), feat_major(state_nsa_vwin)
    per_page = page // CMP_STRIDE
    perm_np = np.zeros((page, page), np.float32)
    for r_ in range(CMP_STRIDE):
        for n_ in range(per_page):
            perm_np[r_ * per_page + n_, n_ * CMP_STRIDE + r_] = 1.0
    perm = jnp.asarray(perm_np, dtype=BF)
    fox_sel = jnp.eye(fox_h, dtype=F32)[None, :, :, None]
    nsa_sel = jnp.asarray(np.arange(nsa_h)[:, None] // grp == np.arange(nkv)[None, :], dtype=F32)[None, :, :, None]

    ab_p, ab_s, c_p, c_s = [], [], [], []
    for l in range(depth):
        i = l // 2
        g_attn = norm_attn[l][None, :]
        g_ffn = norm_ffn[l][None, :]
        gf = norm_final[None, :] if l == depth - 1 else None
        wg, wu, wd = ffn_w_gate[l].astype(BF), ffn_w_up[l].astype(BF), ffn_w_down[l].astype(BF)
        if l % 2 == 0:
            cq_w, ckv_w, kr_w, fq_w, fk_w, fv_w, fl_w = _split_cols(
                ab_w_in[i], (q_rank, kv_rank, MLA_ROPE, fox_h * fox_d, fox_h * fox_d, fox_h * fox_d, fox_h))
            w_in = jnp.concatenate([cq_w, ckv_w, _pad_last(kr_w, LANES), _pad_last(_swap_halves(kr_w), LANES),
                                    fq_w, fk_w, fv_w, _pad_last(fl_w, LANES)], axis=1).astype(BF)
            wuq = ab_mla_wuq[i]
            rope_w = wuq[:, :, MLA_NOPE:]
            wuq_all = jnp.concatenate([
                wuq[:, :, :MLA_NOPE].reshape(q_rank, mla_h * MLA_NOPE),
                _pad_last(rope_w, LANES).reshape(q_rank, mla_h * LANES),
                _pad_last(_swap_halves(rope_w), LANES).reshape(q_rank, mla_h * LANES)], axis=1).astype(BF)
            wuk_bd = _block_diag([ab_mla_wuk[i][:, h, :].T for h in range(mla_h)]).astype(BF)
            wuv_bd = _block_diag([ab_mla_wuv[i][:, h, :] for h in range(mla_h)]).astype(BF)
            wuv_all = ab_mla_wuv[i].reshape(kv_rank, mla_h * mla_v).astype(BF)
            w_out = ab_w_out[i].astype(BF)
            gq, gkv, bfox = ab_mla_gq[i][None, :], ab_mla_gkv[i][None, :], ab_fox_bf[i][None, :]

            cq, ckv, kr, kcat, fq, fk, fv, fkb, fvb, lf = _ab_proj(xp, g_attn, w_in, gkv, bfox, cos_p, sin_p, tm)
            qcat = _mla_q(cq, gq, wuq_all, wuk_bd, cos_p, sin_p, tm, mla_h)
            o_mla = _mla_attn(qcat.reshape(bsz, t, -1), kcat.reshape(bsz, t, -1), wuv_bd, mla_h, mla_scale)
            lf3 = lf.reshape(bsz, t, fox_h)
            f_t = _cumsum_lanes(jnp.transpose(lf3, (0, 2, 1)))
            f_q = jnp.transpose(f_t, (0, 2, 1))
            r3 = lambda a: a.reshape(bsz, t, -1)
            o_fox = _fox_attn(r3(fq), r3(fkb), r3(fvb), f_q, f_t, fox_h)
            xp = _post(xp, [o_mla.reshape(m_p, -1), o_fox.reshape(m_p, -1)], w_out, g_ffn, wg, wu, wd, gf, tm)
            ab_p.append((ckv.reshape(bsz, t, -1), kr.reshape(bsz, t, -1), fk.reshape(bsz, t, fox_h, fox_d),
                         fv.reshape(bsz, t, fox_h, fox_d), lf3))

            ms = db * t_s
            cq, ckv, kr, kcat, fq, fk, fv, fkb, fvb, lf = _ab_proj(xs, g_attn, w_in, gkv, bfox, cos_s, sin_s, ms)
            qcat = _mla_q(cq, gq, wuq_all, wuk_bd, cos_s, sin_s, ms, mla_h)
            o_mla = _mla_dec2(page_table, qcat.reshape(db, mla_h, 256).astype(F32),
                              kcat.reshape(db, 1, 256).astype(F32), wuv_all, cache_mla_ckv, krt_pool,
                              i, mla_h, mla_scale)
            qblk = (fox_sel * fq.reshape(db, 1, fox_h, fox_d).astype(F32)).reshape(db, fox_h, fox_h * fox_d)
            o_fox = _fox_dec2(page_table, qblk, fk.reshape(db, 1, -1), fv.reshape(db, 1, -1),
                              lf.reshape(db, fox_h, 1), fkt_pool, fvt_pool, lft_pool, i, fox_h, fox_d)
            xs = _post(xs, [o_mla.reshape(ms, -1), o_fox.reshape(ms, -1)], w_out, g_ffn, wg, wu, wd, gf, ms)
            ab_s.append((ckv.reshape(db, t_s, -1), kr.reshape(db, t_s, -1), fk.reshape(db, t_s, fox_h, fox_d),
                         fv.reshape(db, t_s, fox_h, fox_d), lf.reshape(db, t_s, fox_h)))
        else:
            nq, nk = nsa_h * hd, nkv * hd
            parts = _split_cols(c_w_in[i], (nq,) + (nk,) * 6 + (3 * nsa_h,))
            w_in = jnp.concatenate(parts[:7] + [_pad_last(parts[7], LANES)], axis=1).astype(BF)
            w_out = c_w_out[i].astype(BF)
            eye = jnp.eye(nkv, dtype=F32)

            def cmp_weights(pe, w1, w2):
                w1r = w1.reshape(2, CMP_STRIDE, hd, -1)
                w_dec = jnp.concatenate([w1r[0], w1r[1]], axis=-1).astype(BF)
                kron = jax.vmap(lambda m: jnp.kron(eye, m))
                w_bd = jnp.concatenate([kron(w1r[0]), kron(w1r[1])], axis=-1).astype(BF)
                pe_flat = jnp.pad(pe.reshape(1, -1), ((0, 7), (0, 0))).astype(BF)
                return (w_dec, w_bd, pe_flat, w1.astype(BF), jnp.tile(w1, (1, nkv)).astype(BF),
                        w2.astype(BF), jnp.kron(eye, w2).astype(BF))

            kw_ = cmp_weights(c_pe_k[i], c_w1_k[i], c_w2_k[i])
            vw_ = cmp_weights(c_pe_v[i], c_w1_v[i], c_w2_v[i])

            q, kc, vc, ks, vs, kw, vw, ksb, vsb, kwb, vwb, gates = _c_proj(xp, g_attn, w_in, tm, nq, nk, nsa_scale)
            r3 = lambda a: a.reshape(bsz, t, -1)
            kcmp, vcmp = _compress_prompt(r3(kc), r3(vc), kw_[1], vw_[1], kw_[2], vw_[2], kw_[4], vw_[4],
                                          kw_[6], vw_[6])
            o = _nsa_attn(r3(q), kcmp, vcmp, r3(ksb), r3(vsb), r3(kwb), r3(vwb), r3(gates), bias_c, tiles,
                          ov_p, ex_p, tconst, nkv, grp, hd)
            xp = _post(xp, [o.reshape(m_p, -1)], w_out, g_ffn, wg, wu, wd, gf, tm)
            r4 = lambda a: a.reshape(bsz, t, nkv, hd)
            keep = min(WINDOW, t)
            c_p.append((r4(kc), r4(vc), r4(ks), r4(vs), r4(kw)[:, t - keep:], r4(vw)[:, t - keep:]))

            ms = db * t_s
            q, kc, vc, ks, vs, kw, vw, ksb, vsb, kwb, vwb, gates = _c_proj(xs, g_attn, w_in, ms, nq, nk, nsa_scale)
            qblk = (nsa_sel * q.reshape(db, nsa_h, 1, hd).astype(F32)).reshape(db, nsa_h, nkv * hd)
            o_cmp, top_i, top_ok = _nsa_cmp_dec(page_table, qblk, perm, kw_[1], vw_[1], kw_[2], vw_[2], kw_[4],
                                                vw_[4], kw_[6], vw_[6], ov_s, t5_table, kc_pool, vc_pool, i,
                                                nkv, grp, past_len, nsel_s)
            ntop = min(SEL_TOP_N, nsel_s)
            flat_top = lambda a: a[:, :, :ntop].reshape(db, nkv * ntop)
            r1 = lambda a: a.reshape(db, 1, -1)
            o_wide = _nsa_sel_win_dec(page_table, flat_top(top_i), flat_top(top_ok), qblk, r1(ks), r1(vs), r1(kw),
                                      r1(vw), win_kt, win_vt, gates[:, :3 * nsa_h].reshape(db, nsa_h, 3), o_cmp,
                                      t5_table, ks_pool, vs_pool, i, nkv, grp, past_len, past_len // SEL_BLOCK)
            o = jnp.stack([o_wide[:, k * grp:(k + 1) * grp, k * hd:(k + 1) * hd] for k in range(nkv)], axis=1)
            xs = _post(xs, [o.reshape(ms, -1).astype(BF)], w_out, g_ffn, wg, wu, wd, gf, ms)
            r4 = lambda a: a.reshape(db, t_s, nkv, hd)
            c_s.append((r4(kc), r4(vc), r4(ks), r4(vs),
                        jnp.concatenate([state_nsa_kwin[i], r4(kw)], axis=1)[:, t_s:],
                        jnp.concatenate([state_nsa_vwin[i], r4(vw)], axis=1)[:, t_s:]))

    def stack(rows, j):
        return jnp.stack([r[j] for r in rows])

    return (xp.reshape(bsz, t, d), xs.reshape(db, t_s, d),
            stack(ab_p, 0), stack(ab_s, 0), stack(ab_p, 1), stack(ab_s, 1),
            stack(ab_p, 2), stack(ab_s, 2), stack(ab_p, 3), stack(ab_s, 3), stack(ab_p, 4), stack(ab_s, 4),
            stack(c_p, 0), stack(c_s, 0), stack(c_p, 1), stack(c_s, 1), stack(c_p, 2), stack(c_s, 2),
            stack(c_p, 3), stack(c_s, 3), stack(c_p, 4), stack(c_s, 4), stack(c_p, 5), stack(c_s, 5))
```

```python
import functools
import math

import numpy as np
import jax
import jax.numpy as jnp
from jax import lax
from jax.experimental import pallas as pl
from jax.experimental.pallas import tpu as pltpu

F32 = jnp.float32
BF = jnp.bfloat16
I32 = jnp.int32

MLA_NOPE = 64
MLA_ROPE = 32
ROPE_THETA = 10000.0
CMP_STRIDE = 16
CMP_BLOCK = 32
SEL_BLOCK = 64
SEL_TOP_N = 16
WINDOW = 512
T5_BUCKETS = 32
T5_MAX_DIST = 128
EPS = 1e-6
FORCE_SCORE = 1e4
NEG_SCORE = -1e9
NEG_BIG = -1e30
LANES = 128

VMEM_LIMIT = 56 * 1024 * 1024


def _params(sem, vmem=VMEM_LIMIT):
    return pltpu.CompilerParams(dimension_semantics=sem, vmem_limit_bytes=vmem)


def _const_spec(shape):
    nd = len(shape)
    return pl.BlockSpec(shape, lambda *a: (0,) * nd)


def _rms(x, g):
    ms = jnp.mean(x * x, axis=-1, keepdims=True)
    return x * lax.rsqrt(ms + EPS) * g


def _dot(a, b):
    return jnp.dot(a, b, preferred_element_type=F32)


def _dot_nt(a, b):
    return lax.dot_general(a, b, (((1,), (1,)), ((), ())), preferred_element_type=F32)


def _log_sigmoid(x):
    return jnp.minimum(x, 0.0) - jnp.log(1.0 + jnp.exp(-jnp.abs(x)))


def _sigmoid(x):
    return 1.0 / (1.0 + jnp.exp(-x))


def _t5_thresholds():
    exact = T5_BUCKETS // 2
    d = np.arange(0, T5_MAX_DIST + 1)
    lr = np.log(np.maximum(d, 1).astype(np.float64) / exact) / math.log(T5_MAX_DIST / exact)
    large = np.minimum(exact + (lr * (T5_BUCKETS - exact)).astype(np.int64), T5_BUCKETS - 1)
    bucket = np.where(d < exact, d, large)
    return [int(np.argmax(bucket >= j)) for j in range(T5_BUCKETS)]


_T5_THR = _t5_thresholds()


def _t5_bias(dist, values):
    out = jnp.zeros(dist.shape, F32) + values(0)
    for j in range(1, T5_BUCKETS):
        out = jnp.where(dist >= _T5_THR[j], values(j), out)
    return out


def _ab_proj_kernel(x_ref, g_ref, w_ref, gkv_ref, bf_ref, cos_ref, sin_ref,
                    cq_ref, ckv_ref, kr_ref, kcat_ref, fq_ref, fk_ref, fv_ref,
                    fkb_ref, fvb_ref, lf_ref, *, fox_scale, nfh):
    h = _rms(x_ref[...], g_ref[...]).astype(BF)

    def mm(c0, c1):
        return _dot(h, w_ref[:, c0:c1])

    cq_ref[...] = mm(0, 256)
    ckv = _rms(mm(256, 384), gkv_ref[...])
    ckv_ref[...] = ckv
    kr = mm(384, 512) * cos_ref[...] + mm(512, 640) * sin_ref[...]
    kr_ref[...] = kr[:, :MLA_ROPE]
    kcat_ref[:, 0:128] = ckv.astype(BF)
    kcat_ref[:, 128:256] = kr.astype(BF)
    fq_ref[...] = (mm(640, 1152) * fox_scale).astype(BF)
    fk = mm(1152, 1664)
    fk_ref[...] = fk
    fkb_ref[...] = fk.astype(BF)
    fv = mm(1664, 2176)
    fv_ref[...] = fv
    fvb_ref[...] = fv.astype(BF)
    fl = mm(2176, 2304)[:, :nfh] + bf_ref[...]
    lf_ref[...] = _log_sigmoid(fl)


def _ab_proj_prompt_kernel(x_ref, g_ref, w_ref, gkv_ref, bf_ref, cos_ref, sin_ref,
                           cq_ref, ckv_ref, kr_ref, kcat_ref, fk_ref, fv_ref, lf_ref,
                           fqh_ref, fkh_ref, fvt_ref, ckvt_ref, *, q_scale, nfh, hd):
    h = _rms(x_ref[...], g_ref[...]).astype(BF)

    def mm(c0, c1):
        return _dot(h, w_ref[:, c0:c1])

    cq_ref[...] = mm(0, 256)
    ckv = _rms(mm(256, 384), gkv_ref[...])
    ckv_ref[...] = ckv
    kr = mm(384, 512) * cos_ref[...] + mm(512, 640) * sin_ref[...]
    kr_ref[...] = kr[:, :MLA_ROPE]
    kcat_ref[:, 0:128] = ckv.astype(BF)
    kcat_ref[:, 128:256] = kr.astype(BF)
    fq = mm(640, 1152) * q_scale
    fk = mm(1152, 1664)
    fk_ref[...] = fk
    fv = mm(1664, 2176)
    fv_ref[...] = fv
    for i in range(nfh):
        fqh_ref[0, i] = fq[:, i * hd:(i + 1) * hd].astype(BF)
        fkh_ref[0, i] = fk[:, i * hd:(i + 1) * hd].astype(BF)
    fvt_ref[0] = fv.T.astype(BF)
    ckvt_ref[0] = ckv.T.astype(BF)
    fl = mm(2176, 2304)[:, :nfh] + bf_ref[...]
    lf_ref[...] = _log_sigmoid(fl)


def _ab_proj_prompt(x, g, w, gkv, bf, cos_t, sin_t, bsz, t, tm, q_scale):
    m, d = x.shape
    nt = t // tm
    nfh = bf.shape[1]
    fd = 512
    hd = fd // nfh
    row = lambda wd: pl.BlockSpec((tm, wd), lambda i: (i, 0))
    tab = pl.BlockSpec((tm, LANES), lambda i: (i % nt, 0))
    hm = pl.BlockSpec((1, nfh, tm, hd), lambda i: (i // nt, 0, i % nt, 0))
    fm = lambda n: pl.BlockSpec((1, n, tm), lambda i: (i // nt, 0, i % nt))
    out_specs = [row(256), row(128), row(MLA_ROPE), row(256), row(fd), row(fd), row(nfh), hm, hm, fm(fd), fm(128)]
    out_shape = ([jax.ShapeDtypeStruct((m, 256), F32), jax.ShapeDtypeStruct((m, 128), F32),
                  jax.ShapeDtypeStruct((m, MLA_ROPE), F32), jax.ShapeDtypeStruct((m, 256), BF),
                  jax.ShapeDtypeStruct((m, fd), F32), jax.ShapeDtypeStruct((m, fd), F32),
                  jax.ShapeDtypeStruct((m, nfh), F32),
                  jax.ShapeDtypeStruct((bsz, nfh, t, hd), BF), jax.ShapeDtypeStruct((bsz, nfh, t, hd), BF),
                  jax.ShapeDtypeStruct((bsz, fd, t), BF), jax.ShapeDtypeStruct((bsz, 128, t), BF)])
    return pl.pallas_call(
        functools.partial(_ab_proj_prompt_kernel, q_scale=q_scale, nfh=nfh, hd=hd),
        grid=(m // tm,),
        in_specs=[row(d), _const_spec(g.shape), _const_spec(w.shape), _const_spec(gkv.shape),
                  _const_spec(bf.shape), tab, tab],
        out_specs=out_specs, out_shape=out_shape,
        compiler_params=_params(("parallel",)),
        name="ab_proj_prompt",
    )(x, g, w, gkv, bf, cos_t, sin_t)


def _fox_km_kernel(q_ref, k_ref, vt_ref, mask_ref, o_ref, *, nh, hd, tq, tk):
    qi = pl.program_id(1)
    qs = [q_ref[0, h] for h in range(nh)]

    def tile(h, ki, carry, bias):
        m_p, l_p, acc = carry
        start = pl.multiple_of(ki * tk, tk)
        st = _dot_nt(k_ref[0, h, pl.ds(start, tk), :], qs[h])
        if bias is not None:
            st = st + bias
        m_n = jnp.maximum(m_p, jnp.max(st, axis=0, keepdims=True))
        p = jnp.exp2(st - m_n)
        alpha = jnp.exp2(m_p - m_n)
        l_n = alpha * l_p + jnp.sum(p, axis=0, keepdims=True)
        vt = vt_ref[0, h * hd:(h + 1) * hd, pl.ds(start, tk)]
        return m_n, l_n, alpha * acc + _dot(vt, p.astype(BF))

    init = (jnp.full((1, tq), NEG_BIG, F32), jnp.zeros((1, tq), F32), jnp.zeros((hd, tq), F32))
    cs = lax.fori_loop(0, qi, lambda ki, c: tuple(tile(h, ki, c[h], None) for h in range(nh)), (init,) * nh)
    for h in range(nh):
        _, l_f, acc = tile(h, qi, cs[h], mask_ref[...])
        o_ref[0, :, h * hd:(h + 1) * hd] = (acc / l_f).T.astype(BF)


def _fox_attn_km(q_aug, k_aug, vt, hd, tq=256):
    b, nh, t, w = q_aug.shape
    mask = jnp.asarray(np.where(np.arange(tq)[:, None] <= np.arange(tq)[None, :], 0.0, NEG_BIG), dtype=F32)
    full_b = lambda a: pl.BlockSpec((1,) + a.shape[1:], lambda bi, qi: (bi,) + (0,) * (a.ndim - 1))
    return pl.pallas_call(
        functools.partial(_fox_km_kernel, nh=nh, hd=hd, tq=tq, tk=tq),
        grid=(b, t // tq),
        in_specs=[pl.BlockSpec((1, nh, tq, w), lambda bi, qi: (bi, 0, qi, 0)), full_b(k_aug), full_b(vt),
                  _const_spec(mask.shape)],
        out_specs=pl.BlockSpec((1, tq, nh * hd), lambda bi, qi: (bi, qi, 0)),
        out_shape=jax.ShapeDtypeStruct((b, t, nh * hd), BF),
        compiler_params=_params(("parallel", "arbitrary")),
        name="fox_attn",
    )(q_aug, k_aug, vt, mask)


def _mla_q_prompt_kernel(cq_ref, gq_ref, wuq_ref, wuk_ref, cos_ref, sin_ref, q_ref, *, nh, q_scale):
    cqn = _rms(cq_ref[...], gq_ref[...]).astype(BF)
    qn = _dot(cqn, wuq_ref[:, 0:nh * MLA_NOPE]).astype(BF)
    qlat = _dot(qn, wuk_ref[...]) * q_scale
    r0 = nh * MLA_NOPE
    r1 = r0 + nh * LANES
    cos = cos_ref[...] * q_scale
    sin = sin_ref[...] * q_scale
    for h in range(nh):
        qr = (_dot(cqn, wuq_ref[:, r0 + h * LANES:r0 + (h + 1) * LANES]) * cos
              + _dot(cqn, wuq_ref[:, r1 + h * LANES:r1 + (h + 1) * LANES]) * sin)
        q_ref[0, h, :, 0:128] = qlat[:, 128 * h:128 * (h + 1)].astype(BF)
        q_ref[0, h, :, 128:256] = qr.astype(BF)


def _mla_q_prompt(cq, gq, wuq, wukbd, cos_t, sin_t, bsz, t, tm, nh, q_scale):
    nt = t // tm
    tab = pl.BlockSpec((tm, LANES), lambda i: (i % nt, 0))
    return pl.pallas_call(
        functools.partial(_mla_q_prompt_kernel, nh=nh, q_scale=q_scale),
        grid=(cq.shape[0] // tm,),
        in_specs=[pl.BlockSpec((tm, cq.shape[1]), lambda i: (i, 0)), _const_spec(gq.shape),
                  _const_spec(wuq.shape), _const_spec(wukbd.shape), tab, tab],
        out_specs=pl.BlockSpec((1, nh, tm, 256), lambda i: (i // nt, 0, i % nt, 0)),
        out_shape=jax.ShapeDtypeStruct((bsz, nh, t, 256), BF),
        compiler_params=_params(("parallel",)),
        name="mla_q_prompt",
    )(cq, gq, wuq, wukbd, cos_t, sin_t)


def _mla_km_kernel(q_ref, k_ref, vt_ref, mask_ref, wuv_ref, o_ref, *, nh, tq, tk, nchain):
    qi = pl.program_id(1)
    last = qi // (tk // tq)
    odd = qi % (tk // tq)
    hpc = nh // nchain
    qs = [q_ref[0, c * hpc:(c + 1) * hpc].reshape(hpc * tq, q_ref.shape[3]) for c in range(nchain)]
    rank = vt_ref.shape[1]

    def tile(c, ki, carry, bias):
        m_p, l_p, acc = carry
        start = pl.multiple_of(ki * tk, tk)
        st = _dot_nt(k_ref[0, pl.ds(start, tk), :], qs[c])
        if bias is not None:
            st = st + bias
        m_n = jnp.maximum(m_p, jnp.max(st, axis=0, keepdims=True))
        p = jnp.exp2(st - m_n)
        alpha = jnp.exp2(m_p - m_n)
        l_n = alpha * l_p + jnp.sum(p, axis=0, keepdims=True)
        return m_n, l_n, alpha * acc + _dot(vt_ref[0, :, pl.ds(start, tk)], p.astype(BF))

    w = hpc * tq
    init = (jnp.full((1, w), NEG_BIG, F32), jnp.zeros((1, w), F32), jnp.zeros((rank, w), F32))
    cs = lax.fori_loop(0, last, lambda ki, c: tuple(tile(j, ki, c[j], None) for j in range(nchain)),
                       (init,) * nchain)
    mask = jnp.concatenate([mask_ref[odd]] * hpc, axis=1)
    olat = []
    for c in range(nchain):
        _, l_f, acc = tile(c, last, cs[c], mask)
        o_t = acc / l_f
        for j in range(hpc):
            olat.append(o_t[:, j * tq:(j + 1) * tq].T.astype(BF))
    o_ref[0] = _dot(jnp.concatenate(olat, axis=1), wuv_ref[...]).astype(BF)


def _mla_attn_km(q_hm, kcat, ckvt, wuvbd, tq=128, tk=256, nchain=4):
    b, nh, t, _ = q_hm.shape
    per = tk // tq
    d = np.arange(tk)[None, :, None] <= (np.arange(per)[:, None, None] * tq + np.arange(tq)[None, None, :])
    mask = jnp.asarray(np.where(d, 0.0, NEG_BIG), dtype=F32)
    full_b = lambda a: pl.BlockSpec((1,) + a.shape[1:], lambda bi, qi: (bi,) + (0,) * (a.ndim - 1))
    return pl.pallas_call(
        functools.partial(_mla_km_kernel, nh=nh, tq=tq, tk=tk, nchain=nchain),
        grid=(b, t // tq),
        in_specs=[pl.BlockSpec((1, nh, tq, q_hm.shape[3]), lambda bi, qi: (bi, 0, qi, 0)),
                  full_b(kcat), full_b(ckvt), _const_spec(mask.shape), _const_spec(wuvbd.shape)],
        out_specs=pl.BlockSpec((1, tq, wuvbd.shape[1]), lambda bi, qi: (bi, qi, 0)),
        out_shape=jax.ShapeDtypeStruct((b, t, wuvbd.shape[1]), BF),
        compiler_params=_params(("parallel", "arbitrary")),
        name="mla_attn",
    )(q_hm, kcat, ckvt, mask, wuvbd)


def _split3(x):
    rp = lambda a: lax.reduce_precision(a, exponent_bits=8, mantissa_bits=7)
    hi = rp(x)
    mid = rp(x - hi)
    return hi, mid, rp(x - hi - mid)


def _ab_proj(x, g, w, gkv, bf, cos_t, sin_t, tm):
    m, d = x.shape
    nt = cos_t.shape[0] // tm
    nfh = bf.shape[1]
    fd = 512
    row = lambda wd: pl.BlockSpec((tm, wd), lambda i: (i, 0))
    tab = pl.BlockSpec((tm, LANES), lambda i: (i % nt, 0))
    outs = [((m, 256), F32), ((m, 128), F32), ((m, MLA_ROPE), F32), ((m, 256), BF),
            ((m, fd), BF), ((m, fd), F32), ((m, fd), F32), ((m, fd), BF), ((m, fd), BF),
            ((m, nfh), F32)]
    return pl.pallas_call(
        functools.partial(_ab_proj_kernel, fox_scale=(fd // nfh) ** -0.5, nfh=nfh),
        grid=(m // tm,),
        in_specs=[row(d), _const_spec(g.shape), _const_spec(w.shape), _const_spec(gkv.shape),
                  _const_spec(bf.shape), tab, tab],
        out_specs=[row(s[1]) for s, _ in outs],
        out_shape=[jax.ShapeDtypeStruct(s, dt) for s, dt in outs],
        compiler_params=_params(("parallel",)),
        name="ab_proj",
    )(x, g, w, gkv, bf, cos_t, sin_t)


def _mla_q_kernel(cq_ref, gq_ref, wuq_ref, wuk_ref, cos_ref, sin_ref, q_ref, *, nh):
    cqn = _rms(cq_ref[...], gq_ref[...]).astype(BF)
    qn = _dot(cqn, wuq_ref[:, 0:nh * MLA_NOPE]).astype(BF)
    qlat = _dot(qn, wuk_ref[...])
    r0 = nh * MLA_NOPE
    r1 = r0 + nh * LANES
    cos = cos_ref[...]
    sin = sin_ref[...]
    for h in range(nh):
        qr = (_dot(cqn, wuq_ref[:, r0 + h * LANES:r0 + (h + 1) * LANES]) * cos
              + _dot(cqn, wuq_ref[:, r1 + h * LANES:r1 + (h + 1) * LANES]) * sin)
        q_ref[:, 256 * h:256 * h + 128] = qlat[:, 128 * h:128 * (h + 1)].astype(BF)
        q_ref[:, 256 * h + 128:256 * (h + 1)] = qr.astype(BF)


def _mla_q(cq, gq, wuq, wukbd, cos_t, sin_t, tm, nh):
    m = cq.shape[0]
    nt = cos_t.shape[0] // tm
    tab = pl.BlockSpec((tm, LANES), lambda i: (i % nt, 0))
    return pl.pallas_call(
        functools.partial(_mla_q_kernel, nh=nh),
        grid=(m // tm,),
        in_specs=[pl.BlockSpec((tm, cq.shape[1]), lambda i: (i, 0)), _const_spec(gq.shape),
                  _const_spec(wuq.shape), _const_spec(wukbd.shape), tab, tab],
        out_specs=pl.BlockSpec((tm, 256 * nh), lambda i: (i, 0)),
        out_shape=jax.ShapeDtypeStruct((m, 256 * nh), BF),
        compiler_params=_params(("parallel",)),
        name="mla_q",
    )(cq, gq, wuq, wukbd, cos_t, sin_t)


def _softmax_step(s, m_prev, l_prev):
    m_new = jnp.maximum(m_prev, jnp.max(s, axis=-1, keepdims=True))
    alpha = jnp.exp(m_prev - m_new)
    p = jnp.exp(s - m_new)
    l_new = alpha * l_prev + jnp.sum(p, axis=-1, keepdims=True)
    return p, alpha, m_new, l_new


def _mla_attn_kernel(q_ref, k_ref, wuv_ref, o_ref, m_ref, l_ref, acc_ref, *, tq, tk, nh, scale):
    qi = pl.program_id(1)
    ki = pl.program_id(2)
    nk = pl.num_programs(2)

    @pl.when(ki == 0)
    def _():
        m_ref[...] = jnp.full(m_ref.shape, NEG_BIG, F32)
        l_ref[...] = jnp.zeros(l_ref.shape, F32)
        acc_ref[...] = jnp.zeros(acc_ref.shape, F32)

    @pl.when(ki * tk <= qi * tq + tq - 1)
    def _():
        k = k_ref[0]
        v = k[:, 0:128]
        rows = qi * tq + lax.broadcasted_iota(I32, (tq, tk), 0)
        cols = ki * tk + lax.broadcasted_iota(I32, (tq, tk), 1)
        mask = cols <= rows
        for h in range(nh):
            q = q_ref[0, :, 256 * h:256 * (h + 1)]
            s = jnp.where(mask, _dot_nt(q, k) * scale, NEG_BIG)
            p, alpha, m_new, l_new = _softmax_step(s, m_ref[h], l_ref[h])
            acc_ref[h] = alpha * acc_ref[h] + _dot(p.astype(BF), v)
            m_ref[h] = m_new
            l_ref[h] = l_new

    @pl.when(ki == nk - 1)
    def _():
        olat = jnp.concatenate([(acc_ref[h] / l_ref[h]).astype(BF) for h in range(nh)], axis=1)
        o_ref[0] = _dot(olat, wuv_ref[...]).astype(BF)


def _mla_attn(qcat, kcat, wuvbd, nh, scale, tq=256, tk=256):
    b, t, _ = qcat.shape
    nq, nk = t // tq, t // tk
    kmap = lambda bi, qi, ki: (bi, jnp.minimum(ki, (qi * tq + tq - 1) // tk), 0)
    return pl.pallas_call(
        functools.partial(_mla_attn_kernel, tq=tq, tk=tk, nh=nh, scale=scale),
        grid=(b, nq, nk),
        in_specs=[pl.BlockSpec((1, tq, 256 * nh), lambda bi, qi, ki: (bi, qi, 0)),
                  pl.BlockSpec((1, tk, 256), kmap), _const_spec(wuvbd.shape)],
        out_specs=pl.BlockSpec((1, tq, wuvbd.shape[1]), lambda bi, qi, ki: (bi, qi, 0)),
        out_shape=jax.ShapeDtypeStruct((b, t, wuvbd.shape[1]), BF),
        scratch_shapes=[pltpu.VMEM((nh, tq, 1), F32), pltpu.VMEM((nh, tq, 1), F32),
                        pltpu.VMEM((nh, tq, 128), F32)],
        compiler_params=_params(("parallel", "parallel", "arbitrary")),
        name="mla_attn",
    )(qcat, kcat, wuvbd)


def _cumsum_kernel(x_ref, o_ref):
    x = x_ref[0]
    n = x.shape[1]
    lane = lax.broadcasted_iota(I32, x.shape, 1)
    k = 1
    while k < n:
        x = x + jnp.where(lane >= k, pltpu.roll(x, k, axis=1), 0.0)
        k *= 2
    o_ref[0] = x


def _cumsum_lanes(x):
    b, h, t = x.shape
    return pl.pallas_call(
        _cumsum_kernel, grid=(b,),
        in_specs=[pl.BlockSpec((1, h, t), lambda i: (i, 0, 0))],
        out_specs=pl.BlockSpec((1, h, t), lambda i: (i, 0, 0)),
        out_shape=jax.ShapeDtypeStruct((b, h, t), F32),
        compiler_params=_params(("parallel",)),
        name="fox_cumsum",
    )(x)


def _fox_attn_kernel(q_ref, k_ref, v_ref, f_ref, ft_ref, o_ref, m_ref, l_ref, acc_ref, *, tq, tk, nh):
    qi = pl.program_id(1)
    ki = pl.program_id(2)
    nk = pl.num_programs(2)

    @pl.when(ki == 0)
    def _():
        m_ref[...] = jnp.full(m_ref.shape, NEG_BIG, F32)
        l_ref[...] = jnp.zeros(l_ref.shape, F32)
        acc_ref[...] = jnp.zeros(acc_ref.shape, F32)

    @pl.when(ki * tk <= qi * tq + tq - 1)
    def _():
        rows = qi * tq + lax.broadcasted_iota(I32, (tq, tk), 0)
        cols = ki * tk + lax.broadcasted_iota(I32, (tq, tk), 1)
        mask = cols <= rows
        lo = lax.broadcasted_iota(I32, (tq, LANES), 1) < 64
        fq = f_ref[0]
        fk = ft_ref[0]
        for hp in range(nh // 2):
            q2 = q_ref[0, :, 128 * hp:128 * (hp + 1)]
            k2 = k_ref[0, :, 128 * hp:128 * (hp + 1)]
            v2 = v_ref[0, :, 128 * hp:128 * (hp + 1)]
            zero = jnp.zeros_like(q2)
            pvs, alphas = [], []
            for e in range(2):
                h = 2 * hp + e
                qm = jnp.where(lo, q2, zero) if e == 0 else jnp.where(lo, zero, q2)
                s = _dot_nt(qm, k2) + (fq[:, h:h + 1] - fk[h:h + 1, :])
                s = jnp.where(mask, s, NEG_BIG)
                p, alpha, m_new, l_new = _softmax_step(s, m_ref[h], l_ref[h])
                m_ref[h] = m_new
                l_ref[h] = l_new
                pvs.append(_dot(p.astype(BF), v2))
                alphas.append(alpha)
            a2 = jnp.where(lo, alphas[0], alphas[1])
            acc_ref[:, 128 * hp:128 * (hp + 1)] = (a2 * acc_ref[:, 128 * hp:128 * (hp + 1)]
                                                   + jnp.where(lo, pvs[0], pvs[1]))

    @pl.when(ki == nk - 1)
    def _():
        lo = lax.broadcasted_iota(I32, (tq, LANES), 1) < 64
        for hp in range(nh // 2):
            l2 = jnp.where(lo, l_ref[2 * hp], l_ref[2 * hp + 1])
            o_ref[0, :, 128 * hp:128 * (hp + 1)] = (acc_ref[:, 128 * hp:128 * (hp + 1)] / l2).astype(BF)


def _fox_attn(fq, fk, fv, f, ft, nh, tq=256, tk=256):
    b, t, w = fq.shape
    nq, nk = t // tq, t // tk
    kclamp = lambda qi, ki: jnp.minimum(ki, (qi * tq + tq - 1) // tk)
    return pl.pallas_call(
        functools.partial(_fox_attn_kernel, tq=tq, tk=tk, nh=nh),
        grid=(b, nq, nk),
        in_specs=[pl.BlockSpec((1, tq, w), lambda bi, qi, ki: (bi, qi, 0)),
                  pl.BlockSpec((1, tk, w), lambda bi, qi, ki: (bi, kclamp(qi, ki), 0)),
                  pl.BlockSpec((1, tk, w), lambda bi, qi, ki: (bi, kclamp(qi, ki), 0)),
                  pl.BlockSpec((1, tq, nh), lambda bi, qi, ki: (bi, qi, 0)),
                  pl.BlockSpec((1, nh, tk), lambda bi, qi, ki: (bi, 0, kclamp(qi, ki)))],
        out_specs=pl.BlockSpec((1, tq, w), lambda bi, qi, ki: (bi, qi, 0)),
        out_shape=jax.ShapeDtypeStruct((b, t, w), BF),
        scratch_shapes=[pltpu.VMEM((nh, tq, 1), F32), pltpu.VMEM((nh, tq, 1), F32),
                        pltpu.VMEM((tq, w), F32)],
        compiler_params=_params(("parallel", "parallel", "arbitrary")),
        name="fox_attn",
    )(fq, fk, fv, f, ft)


def _post_kernel(*refs, n_o, tf, final):
    x_ref = refs[0]
    o_refs = refs[1:1 + n_o]
    wo_refs = refs[1 + n_o:2 + n_o]
    g_ref, wg_ref, wu_ref, wd_ref = refs[2 + n_o:6 + n_o]
    nxt = 6 + n_o
    if final:
        gf_ref = refs[nxt]
        nxt += 1
    out_ref = refs[nxt]
    o = o_refs[0][...] if n_o == 1 else jnp.concatenate([r[...] for r in o_refs], axis=1)
    x1 = x_ref[...] + _dot(o, wo_refs[0][...])
    h = _rms(x1, g_ref[...]).astype(BF)
    acc = x1
    nf = wg_ref.shape[1] // tf
    for f in range(nf):
        a = _dot(h, wg_ref[:, f * tf:(f + 1) * tf])
        u = _dot(h, wu_ref[:, f * tf:(f + 1) * tf])
        t = (a * _sigmoid(a) * u).astype(BF)
        acc = acc + _dot(t, wd_ref[f * tf:(f + 1) * tf, :])
    if final:
        out_ref[...] = _rms(acc, gf_ref[...])
    else:
        out_ref[...] = acc


def _post(x, os_, wo, g, wg, wu, wd, gf, tm):
    m, d = x.shape
    n_o = len(os_)
    final = gf is not None
    row = lambda wd_: pl.BlockSpec((tm, wd_), lambda i: (i, 0))
    single = lambda a: pl.BlockSpec(a.shape, lambda i: (0,) * a.ndim, pipeline_mode=pl.Buffered(1))
    args = [x] + list(os_) + [wo, g, wg, wu, wd] + ([gf] if final else [])
    in_specs = ([row(d)] + [row(o.shape[1]) for o in os_] + [single(wo)]
                + [_const_spec(g.shape), single(wg), single(wu), single(wd)]
                + ([_const_spec(gf.shape)] if final else []))
    return pl.pallas_call(
        functools.partial(_post_kernel, n_o=n_o, tf=256, final=final),
        grid=(m // tm,),
        in_specs=in_specs,
        out_specs=row(d),
        out_shape=jax.ShapeDtypeStruct((m, d), F32),
        compiler_params=_params(("parallel",)),
        name="post_ffn",
    )(*args)


def _c_proj_kernel(x_ref, g_ref, w_ref, q_ref, kc_ref, vc_ref, ks_ref, vs_ref, kw_ref, vw_ref,
                   ksb_ref, vsb_ref, kwb_ref, vwb_ref, gt_ref, *, scale, nq, nkv):
    h = _rms(x_ref[...], g_ref[...]).astype(BF)

    def mm(c0, c1):
        return _dot(h, w_ref[:, c0:c1])

    q_ref[...] = (mm(0, nq) * scale).astype(BF)
    c = nq
    kc_ref[...] = mm(c, c + nkv)
    vc_ref[...] = mm(c + nkv, c + 2 * nkv)
    ks = mm(c + 2 * nkv, c + 3 * nkv)
    ks_ref[...] = ks
    ksb_ref[...] = ks.astype(BF)
    vs = mm(c + 3 * nkv, c + 4 * nkv)
    vs_ref[...] = vs
    vsb_ref[...] = vs.astype(BF)
    kw = mm(c + 4 * nkv, c + 5 * nkv)
    kw_ref[...] = kw
    kwb_ref[...] = kw.astype(BF)
    vw = mm(c + 5 * nkv, c + 6 * nkv)
    vw_ref[...] = vw
    vwb_ref[...] = vw.astype(BF)
    gt_ref[...] = _sigmoid(mm(c + 6 * nkv, c + 6 * nkv + LANES))


def _c_proj(x, g, w, tm, nq, nkv, scale):
    m, d = x.shape
    row = lambda wd: pl.BlockSpec((tm, wd), lambda i: (i, 0))
    outs = ([((m, nq), BF)] + [((m, nkv), F32)] * 6 + [((m, nkv), BF)] * 4 + [((m, LANES), F32)])
    return pl.pallas_call(
        functools.partial(_c_proj_kernel, scale=scale, nq=nq, nkv=nkv),
        grid=(m // tm,),
        in_specs=[row(d), _const_spec(g.shape), _const_spec(w.shape)],
        out_specs=[row(s[1]) for s, _ in outs],
        out_shape=[jax.ShapeDtypeStruct(s, dt) for s, dt in outs],
        compiler_params=_params(("parallel",)),
        name="c_proj",
    )(x, g, w)


def _c_proj_prompt_kernel(x_ref, g_ref, w_ref, q_ref, kc_ref, vc_ref, ks_ref, vs_ref, kw_ref, vw_ref,
                          ksh_ref, kwh_ref, vst_ref, vwt_ref, gtt_ref, *, scale, nh, nkv, hd):
    h = _rms(x_ref[...], g_ref[...]).astype(BF)
    nq, nk = nh * hd, nkv * hd

    def mm(c0, c1):
        return _dot(h, w_ref[:, c0:c1])

    q = mm(0, nq) * scale
    for i in range(nh):
        q_ref[0, i] = q[:, i * hd:(i + 1) * hd].astype(BF)
    c = nq
    kc_ref[...] = mm(c, c + nk)
    vc_ref[...] = mm(c + nk, c + 2 * nk)
    for j, (f_ref, h_ref, t_ref) in enumerate(((ks_ref, ksh_ref, None), (vs_ref, None, vst_ref),
                                               (kw_ref, kwh_ref, None), (vw_ref, None, vwt_ref))):
        a = mm(c + (2 + j) * nk, c + (3 + j) * nk)
        f_ref[...] = a
        if h_ref is not None:
            for i in range(nkv):
                h_ref[0, i] = a[:, i * hd:(i + 1) * hd].astype(BF)
        if t_ref is not None:
            t_ref[0] = a.T.astype(BF)
    gtt_ref[0] = _sigmoid(mm(c + 6 * nk, c + 6 * nk + LANES)).T


def _c_proj_prompt(x, g, w, bsz, t, tm, nh, nkv, hd, scale):
    m, d = x.shape
    nq, nk = nh * hd, nkv * hd
    nt = t // tm
    row = lambda wd: pl.BlockSpec((tm, wd), lambda i: (i, 0))
    hm = lambda n: pl.BlockSpec((1, n, tm, hd), lambda i: (i // nt, 0, i % nt, 0))
    fm = lambda n: pl.BlockSpec((1, n, tm), lambda i: (i // nt, 0, i % nt))
    out_specs = [hm(nh)] + [row(nk)] * 6 + [hm(nkv), hm(nkv), fm(nk), fm(nk), fm(LANES)]
    out_shape = ([jax.ShapeDtypeStruct((bsz, nh, t, hd), BF)] + [jax.ShapeDtypeStruct((m, nk), F32)] * 6
                 + [jax.ShapeDtypeStruct((bsz, nkv, t, hd), BF)] * 2
                 + [jax.ShapeDtypeStruct((bsz, nk, t), BF)] * 2 + [jax.ShapeDtypeStruct((bsz, LANES, t), F32)])
    return pl.pallas_call(
        functools.partial(_c_proj_prompt_kernel, scale=scale, nh=nh, nkv=nkv, hd=hd),
        grid=(m // tm,),
        in_specs=[row(d), _const_spec(g.shape), _const_spec(w.shape)],
        out_specs=out_specs, out_shape=out_shape,
        compiler_params=_params(("parallel",)),
        name="c_proj_prompt",
    )(x, g, w)


def _gelu(x):
    return 0.5 * x * (1.0 + jnp.tanh(math.sqrt(2.0 / math.pi) * (x + 0.044715 * (x * x * x))))


def _compress_tail(y, pe_ref, w1_ref, w2_ref, width):
    nchunk = y.shape[0]
    pe_term = _dot(pe_ref[...], w1_ref[...])[0:1, :]
    second = pltpu.roll(y[:, width:2 * width], nchunk - 1, axis=0)
    hid = _gelu(y[:, 0:width] + second + pe_term).astype(BF)
    return _dot(hid, w2_ref[...])


def _compress_prompt_kernel(kc0_ref, kc1_ref, vc0_ref, vc1_ref, wk_ref, wv_ref, pek_ref, pev_ref,
                            w1k_ref, w1v_ref, w2k_ref, w2v_ref, ko_ref, vo_ref, *, nchunk, width, nkv):
    outs = []
    for x_refs, w_ref, pe_ref, w1_ref, w2_ref in (
            ((kc0_ref, kc1_ref), wk_ref, pek_ref, w1k_ref, w2k_ref),
            ((vc0_ref, vc1_ref), wv_ref, pev_ref, w1v_ref, w2v_ref)):
        y = jnp.zeros((nchunk, 2 * width), F32)
        for r in range(CMP_STRIDE):
            for half, x_ref in enumerate(x_refs):
                xr = x_ref[0, pl.ds(r, nchunk, stride=CMP_STRIDE), :].astype(BF)
                y = y + _dot(xr, w_ref[r, half * LANES:(half + 1) * LANES, :])
        outs.append(_compress_tail(y, pe_ref, w1_ref, w2_ref, width))
    hd = width // nkv
    for i in range(nkv):
        ko_ref[0, i] = outs[0][:, i * hd:(i + 1) * hd].astype(BF)
    vo_ref[0] = outs[1].T.astype(BF)


def _compress_prompt(kc, vc, wk, wv, pek, pev, w1k, w1v, w2k, w2v, nkv):
    b, t, width = kc.shape
    assert width == 2 * LANES
    nchunk = t // CMP_STRIDE
    half = lambda j: pl.BlockSpec((1, t, LANES), lambda i: (i, 0, j))
    consts = [wk, wv, pek, pev, w1k, w1v, w2k, w2v]
    return pl.pallas_call(
        functools.partial(_compress_prompt_kernel, nchunk=nchunk, width=width, nkv=nkv),
        grid=(b,),
        in_specs=[half(0), half(1), half(0), half(1)] + [_const_spec(c.shape) for c in consts],
        out_specs=[pl.BlockSpec((1, nkv, nchunk, width // nkv), lambda i: (i, 0, 0, 0)),
                   pl.BlockSpec((1, width, nchunk), lambda i: (i, 0, 0))],
        out_shape=[jax.ShapeDtypeStruct((b, nkv, nchunk, width // nkv), BF),
                   jax.ShapeDtypeStruct((b, width, nchunk), BF)],
        compiler_params=_params(("parallel",)),
        name="nsa_compress_prompt",
    )(kc, kc, vc, vc, *consts)


def _t5_tiles_kernel(tbl_ref, tiles_ref, bc_ref, *, t, ncmp):
    h = pl.program_id(0)
    val = lambda j: tbl_ref[j, h]
    r = lax.broadcasted_iota(I32, (LANES, LANES), 0)
    c = lax.broadcasted_iota(I32, (LANES, LANES), 1)
    tiles_ref[0, 0] = _t5_bias(r - c, val)
    tiles_ref[0, 1] = _t5_bias(LANES + r - c, val)
    tt = lax.broadcasted_iota(I32, (t, ncmp), 0)
    n = lax.broadcasted_iota(I32, (t, ncmp), 1)
    bc_ref[0] = _t5_bias(tt - (n * CMP_STRIDE + CMP_BLOCK - 1), val)


def _t5_tiles(t5_table, t, ncmp):
    nh = t5_table.shape[1]
    return pl.pallas_call(
        functools.partial(_t5_tiles_kernel, t=t, ncmp=ncmp),
        grid=(nh,),
        in_specs=[pl.BlockSpec(memory_space=pltpu.SMEM)],
        out_specs=[pl.BlockSpec((1, 2, LANES, LANES), lambda i: (i, 0, 0, 0)),
                   pl.BlockSpec((1, t, ncmp), lambda i: (i, 0, 0))],
        out_shape=[jax.ShapeDtypeStruct((nh, 2, LANES, LANES), F32),
                   jax.ShapeDtypeStruct((nh, t, ncmp), F32)],
        compiler_params=_params(("arbitrary",)),
        name="t5_tiles",
    )(t5_table)


def _topk_steps(score, k):
    lane = lax.broadcasted_iota(I32, score.shape, 1).astype(F32)
    big = float(score.shape[1])
    out = []
    for n in range(k):
        m = jnp.max(score, axis=1, keepdims=True)
        idx = jnp.min(jnp.where(score == m, lane, big), axis=1, keepdims=True)
        hit = lane == idx
        out.append((n, hit, idx, m))
        score = jnp.where(hit, -jnp.inf, score)
    return out


def _topk_mask(score, k):
    sel = jnp.zeros(score.shape, F32)
    for _, hit, _, _ in _topk_steps(score, k):
        sel = jnp.where(hit, 1.0, sel)
    return sel > 0.5


def _nsa_attn_kernel(q_ref, kcmp_ref, vcmp_ref, ks_ref, vs_ref, kw_ref, vw_ref, gt_ref, bc_ref,
                     tiles_ref, ov_ref, ex_ref, tc_ref, o_ref, km_ref, *, tq, t, nkv, grp, hd):
    qi = pl.program_id(1)
    ncmp = kcmp_ref.shape[1]
    nsel = t // SEL_BLOCK
    m4 = grp * tq
    trow = qi * tq + lax.broadcasted_iota(I32, (tq, LANES), 0)
    lane = lax.broadcasted_iota(I32, (tq, LANES), 1)
    gates = gt_ref[0]

    def stack(fn):
        return jnp.concatenate([fn(g) for g in range(grp)], axis=0)

    r4 = lax.broadcasted_iota(I32, (m4, LANES), 0) % tq
    c4 = lax.broadcasted_iota(I32, (m4, LANES), 1)

    for k in range(nkv):
        q4 = stack(lambda g: q_ref[0, :, (k * grp + g) * hd:(k * grp + g + 1) * hd])
        kc = kcmp_ref[0, :, k * hd:(k + 1) * hd]
        vc = vcmp_ref[0, :, k * hd:(k + 1) * hd]
        bias_c = stack(lambda g: bc_ref[k * grp + g])
        t4 = qi * tq + r4
        mask_c = (c4 * CMP_STRIDE + CMP_BLOCK - 1) <= t4
        s = jnp.where(mask_c, _dot_nt(q4, kc) + bias_c, NEG_BIG)
        mx = jnp.max(s, axis=1, keepdims=True)
        e = jnp.where(mask_c, jnp.exp(s - mx), 0.0)
        p = e / jnp.maximum(jnp.sum(e, axis=1, keepdims=True), 1e-30)
        pb = p.astype(BF)
        o_cmp = _dot(pb, vc)
        imp4 = _dot(pb, ov_ref[...])
        imp = imp4[0:tq]
        for g in range(1, grp):
            imp = imp + imp4[g * tq:(g + 1) * tq]
        cur = trow // SEL_BLOCK
        forced = (lane == 0) | (lane == cur) | (lane == cur - 1)
        score = jnp.where(forced, FORCE_SCORE, imp)
        score = jnp.where(lane * SEL_BLOCK <= trow, score, NEG_SCORE)
        score = jnp.where(lane < nsel, score, -jnp.inf)
        sel = _topk_mask(score, min(SEL_TOP_N, nsel)) & (score > 0.5 * NEG_SCORE)
        km_ref[...] = _dot(jnp.where(sel, 1.0, 0.0).astype(BF), ex_ref[...])

        def attend(ki, carry, kref, vref, bias, mask_fn):
            m_p, l_p, acc = carry
            start = pl.multiple_of(ki * LANES, LANES)
            kt = kref[0, pl.ds(start, LANES), k * hd:(k + 1) * hd]
            vt = vref[0, pl.ds(start, LANES), k * hd:(k + 1) * hd]
            sc = _dot_nt(q4, kt) + bias
            sc = jnp.where(mask_fn(ki, start), sc, NEG_BIG)
            p_, alpha, m_n, l_n = _softmax_step(sc, m_p, l_p)
            return m_n, l_n, alpha * acc + _dot(p_.astype(BF), vt)

        def const_bias():
            return stack(lambda g: jnp.zeros((tq, LANES), F32) + tc_ref[k * grp + g])

        def tile_bias(which):
            return stack(lambda g: tiles_ref[k * grp + g, which])

        def sel_mask(ki, start):
            km = km_ref[:, pl.ds(start, LANES)]
            km4 = jnp.concatenate([km] * grp, axis=0)
            return (km4 > 0.5) & ((ki * LANES + c4) <= t4)

        def win_mask(ki, start):
            dist = t4 - (ki * LANES + c4)
            return (dist >= 0) & (dist <= WINDOW)

        init = (jnp.full((m4, 1), NEG_BIG, F32), jnp.zeros((m4, 1), F32), jnp.zeros((m4, hd), F32))

        def run(kref, vref, mask_fn, first):
            far = lax.fori_loop(first, jnp.maximum(qi - 1, first),
                                lambda ki, c: attend(ki, c, kref, vref, const_bias(), mask_fn), init)
            near = lax.cond(qi >= 1,
                            lambda c: attend(qi - 1, c, kref, vref, tile_bias(1), mask_fn),
                            lambda c: c, far)
            m_f, l_f, acc = attend(qi, near, kref, vref, tile_bias(0), mask_fn)
            return acc / l_f

        o_slc = run(ks_ref, vs_ref, sel_mask, 0)
        o_win = run(kw_ref, vw_ref, win_mask, jnp.maximum(qi - WINDOW // LANES, 0))
        for g in range(grp):
            hh = k * grp + g
            og = (gates[:, 3 * hh:3 * hh + 1] * o_cmp[g * tq:(g + 1) * tq]
                  + gates[:, 3 * hh + 1:3 * hh + 2] * o_slc[g * tq:(g + 1) * tq]
                  + gates[:, 3 * hh + 2:3 * hh + 3] * o_win[g * tq:(g + 1) * tq])
            o_ref[0, :, hh * hd:(hh + 1) * hd] = og.astype(BF)


def _nsa_attn(q, kcmp, vcmp, ksb, vsb, kwb, vwb, gates, bias_c, tiles, ov, ex, tconst, nkv, grp, hd,
              tq=128):
    b, t, wq = q.shape
    nq = t // tq
    nh = nkv * grp
    full_b = lambda a: pl.BlockSpec((1,) + a.shape[1:], lambda bi, qi: (bi,) + (0,) * (a.ndim - 1))
    return pl.pallas_call(
        functools.partial(_nsa_attn_kernel, tq=tq, t=t, nkv=nkv, grp=grp, hd=hd),
        grid=(b, nq),
        in_specs=[pl.BlockSpec((1, tq, wq), lambda bi, qi: (bi, qi, 0)),
                  full_b(kcmp), full_b(vcmp), full_b(ksb), full_b(vsb), full_b(kwb), full_b(vwb),
                  pl.BlockSpec((1, tq, LANES), lambda bi, qi: (bi, qi, 0)),
                  pl.BlockSpec((nh, tq, LANES), lambda bi, qi: (0, qi, 0)),
                  _const_spec(tiles.shape), _const_spec(ov.shape), _const_spec(ex.shape),
                  pl.BlockSpec(memory_space=pltpu.SMEM)],
        out_specs=pl.BlockSpec((1, tq, wq), lambda bi, qi: (bi, qi, 0)),
        out_shape=jax.ShapeDtypeStruct((b, t, wq), BF),
        scratch_shapes=[pltpu.VMEM((tq, t), F32)],
        compiler_params=_params(("parallel", "arbitrary")),
        name="nsa_attn",
    )(q, kcmp, vcmp, ksb, vsb, kwb, vwb, gates, bias_c, tiles, ov, ex, tconst)


def _page_copies(pt_ref, b, first_page, npages, pools, bufs, sems, layer):
    out = []
    for j in range(npages):
        page = pt_ref[b, first_page + j]
        for pool, buf, sem in zip(pools, bufs, sems):
            out.append(pltpu.make_async_copy(pool.at[layer, page], buf(j), sem))
    return out


def _paged_pipeline(pt_ref, pools, bufs_of_slot, sem_of_slot, layer, npages):
    b = pl.program_id(0)
    s = pl.program_id(1)
    nb = pl.num_programs(0)
    ns = pl.num_programs(1)
    g = b * ns + s
    slot = g % 2

    def copies(bb, ss, sl):
        return _page_copies(pt_ref, bb, ss * npages, npages, pools, bufs_of_slot(sl), sem_of_slot(sl), layer)

    @pl.when(g == 0)
    def _():
        for c in copies(b, s, slot):
            c.start()

    @pl.when(g + 1 < nb * ns)
    def _():
        wrap = s + 1 == ns
        for c in copies(jnp.where(wrap, b + 1, b), jnp.where(wrap, 0, s + 1), 1 - slot):
            c.start()

    for c in copies(b, s, slot):
        c.wait()
    return slot


def _mla_dec_kernel(pt_ref, q_ref, knew_ref, wuv_ref, ckv_hbm, kr_hbm, o_ref,
                    cbuf, rbuf, sem, m_ref, l_ref, acc_ref, *, layer, npages, nh, scale, vdim):
    s_id = pl.program_id(1)
    ns = pl.num_programs(1)
    slot = _paged_pipeline(
        pt_ref, (ckv_hbm, kr_hbm),
        lambda sl: (lambda j: cbuf.at[sl, j], lambda j: rbuf.at[sl, j]),
        lambda sl: (sem.at[0, sl], sem.at[1, sl]), layer, npages)

    @pl.when(s_id == 0)
    def _():
        m_ref[...] = jnp.full(m_ref.shape, NEG_BIG, F32)
        l_ref[...] = jnp.zeros(l_ref.shape, F32)
        acc_ref[...] = jnp.zeros(acc_ref.shape, F32)

    q = q_ref[0]
    page = cbuf.shape[2]
    kp = cbuf[slot].reshape(npages * page, cbuf.shape[3]).astype(BF)
    rp = rbuf[slot].reshape(npages * page, rbuf.shape[3]).astype(BF)
    sc = (_dot_nt(q[:, 0:128].astype(BF), kp) + _dot_nt(q[:, 128:128 + MLA_ROPE].astype(BF), rp)) * scale
    p, alpha, m_new, l_new = _softmax_step(sc, m_ref[...], l_ref[...])
    acc_ref[...] = alpha * acc_ref[...] + _dot(p.astype(BF), kp)
    m_ref[...] = m_new
    l_ref[...] = l_new

    @pl.when(s_id == ns - 1)
    def _():
        kn = knew_ref[0]
        s_new = jnp.sum(q * kn, axis=1, keepdims=True) * scale
        m_p = m_ref[...]
        m_n = jnp.maximum(m_p, s_new)
        a = jnp.exp(m_p - m_n)
        p_new = jnp.exp(s_new - m_n)
        l_f = a * l_ref[...] + p_new
        acc = a * acc_ref[...] + p_new.astype(BF).astype(F32) * kn[:, 0:128]
        olat = (acc / l_f).astype(BF)
        r = _dot(olat, wuv_ref[...])
        row = lax.broadcasted_iota(I32, r.shape, 0)
        col = lax.broadcasted_iota(I32, r.shape, 1)
        o_ref[0] = jnp.sum(jnp.where(col // vdim == row, r, 0.0), axis=0, keepdims=True).astype(BF)


def _mla_dec(page_table, q, knew, wuv_all, ckv_pool, kr_pool, layer, nh, scale, npages=16):
    db, n_pages = page_table.shape
    page = ckv_pool.shape[2]
    vdim = wuv_all.shape[1] // nh
    grid_spec = pltpu.PrefetchScalarGridSpec(
        num_scalar_prefetch=1, grid=(db, n_pages // npages),
        in_specs=[pl.BlockSpec((1, nh, 256), lambda b, s, pt: (b, 0, 0)),
                  pl.BlockSpec((1, 1, 256), lambda b, s, pt: (b, 0, 0)),
                  _const_spec(wuv_all.shape),
                  pl.BlockSpec(memory_space=pl.ANY), pl.BlockSpec(memory_space=pl.ANY)],
        out_specs=pl.BlockSpec((1, 1, wuv_all.shape[1]), lambda b, s, pt: (b, 0, 0)),
        scratch_shapes=[pltpu.VMEM((2, npages, page, ckv_pool.shape[3]), F32),
                        pltpu.VMEM((2, npages, page, kr_pool.shape[3]), F32),
                        pltpu.SemaphoreType.DMA((2, 2)),
                        pltpu.VMEM((nh, 1), F32), pltpu.VMEM((nh, 1), F32), pltpu.VMEM((nh, 128), F32)])
    return pl.pallas_call(
        functools.partial(_mla_dec_kernel, layer=layer, npages=npages, nh=nh, scale=scale, vdim=vdim),
        grid_spec=grid_spec,
        out_shape=jax.ShapeDtypeStruct((db, 1, wuv_all.shape[1]), BF),
        compiler_params=_params(("arbitrary", "arbitrary")),
        name="mla_dec",
    )(page_table, q, knew, wuv_all, ckv_pool, kr_pool)


def _fox_bias_kernel(pt_ref, lfn_ref, lf_hbm, o_ref, lbuf, sem, *, layer, n_pages):
    b = pl.program_id(0)
    page = lbuf.shape[1]

    def copy(p):
        return pltpu.make_async_copy(lf_hbm.at[layer, pt_ref[b, p]], lbuf.at[p], sem.at[0])

    def start(p, c):
        copy(p).start()
        return c

    def wait(p, c):
        copy(p).wait()
        return c

    lax.fori_loop(0, n_pages, start, 0)
    lax.fori_loop(0, n_pages, wait, 0)

    def xpose(p, c):
        o_ref[0, :, pl.ds(pl.multiple_of(p * page, page), page)] = lbuf[p].T
        return c

    lax.fori_loop(0, n_pages, xpose, 0)
    x = o_ref[0]
    n = x.shape[1]
    lane = lax.broadcasted_iota(I32, x.shape, 1)
    y = x
    k = 1
    while k < n:
        y = y + jnp.where(lane + k < n, pltpu.roll(y, n - k, axis=1), 0.0)
        k *= 2
    o_ref[0] = (y - x) + lfn_ref[0]


def _fox_bias(page_table, lf_new, lf_pool, layer):
    db, n_pages = page_table.shape
    page, nh = lf_pool.shape[2], lf_pool.shape[3]
    grid_spec = pltpu.PrefetchScalarGridSpec(
        num_scalar_prefetch=1, grid=(db,),
        in_specs=[pl.BlockSpec((1, nh, 1), lambda b, pt: (b, 0, 0)), pl.BlockSpec(memory_space=pl.ANY)],
        out_specs=pl.BlockSpec((1, nh, n_pages * page), lambda b, pt: (b, 0, 0)),
        scratch_shapes=[pltpu.VMEM((n_pages, page, nh), F32), pltpu.SemaphoreType.DMA((1,))])
    return pl.pallas_call(
        functools.partial(_fox_bias_kernel, layer=layer, n_pages=n_pages),
        grid_spec=grid_spec,
        out_shape=jax.ShapeDtypeStruct((db, nh, n_pages * page), F32),
        compiler_params=_params(("arbitrary",)),
        name="fox_bias",
    )(page_table, lf_new, lf_pool)


def _fox_dec_kernel(pt_ref, q_ref, kn_ref, vn_ref, bias_ref, k_hbm, v_hbm, o_ref,
                    kbuf, vbuf, sem, m_ref, l_ref, acc_ref, *, layer, npages, nh):
    s_id = pl.program_id(1)
    ns = pl.num_programs(1)
    rows = kbuf.shape[1] // npages
    slot = _paged_pipeline(
        pt_ref, (k_hbm, v_hbm),
        lambda sl: (lambda j: kbuf.at[sl, pl.ds(j * rows, rows), :], lambda j: vbuf.at[sl, pl.ds(j * rows, rows), :]),
        lambda sl: (sem.at[0, sl], sem.at[1, sl]), layer, npages)

    @pl.when(s_id == 0)
    def _():
        m_ref[...] = jnp.full(m_ref.shape, NEG_BIG, F32)
        l_ref[...] = jnp.zeros(l_ref.shape, F32)
        acc_ref[...] = jnp.zeros(acc_ref.shape, F32)

    q = q_ref[0]
    nkeys = kbuf.shape[1] // nh
    hrow = lax.broadcasted_iota(I32, q.shape, 0)
    sc = bias_ref[0]
    for h in range(nh):
        kh = kbuf[slot, pl.ds(h, nkeys, stride=nh), :].astype(BF)
        sc = sc + _dot_nt(jnp.where(hrow == h, q, 0.0).astype(BF), kh)
    p, alpha, m_new, l_new = _softmax_step(sc, m_ref[...], l_ref[...])
    prow = lax.broadcasted_iota(I32, p.shape, 0)
    acc = alpha * acc_ref[...]
    for h in range(nh):
        vh = vbuf[slot, pl.ds(h, nkeys, stride=nh), :].astype(BF)
        acc = acc + _dot(jnp.where(prow == h, p, 0.0).astype(BF), vh)
    acc_ref[...] = acc
    m_ref[...] = m_new
    l_ref[...] = l_new

    @pl.when(s_id == ns - 1)
    def _():
        kn = kn_ref[0].astype(BF).astype(F32)
        vn = vn_ref[0].astype(BF).astype(F32)
        s_new = jnp.sum(q * kn, axis=1, keepdims=True)
        m_p = m_ref[...]
        m_n = jnp.maximum(m_p, s_new)
        a = jnp.exp(m_p - m_n)
        p_new = jnp.exp(s_new - m_n)
        l_f = a * l_ref[...] + p_new
        o_ref[0] = ((a * acc_ref[...] + p_new.astype(BF).astype(F32) * vn) / l_f).astype(BF)


def _fox_dec(page_table, q, kn, vn, bias, k_pool, v_pool, layer, nh, npages=8):
    db, n_pages = page_table.shape
    rows, hd = k_pool.shape[2], k_pool.shape[3]
    page = rows // nh
    grid_spec = pltpu.PrefetchScalarGridSpec(
        num_scalar_prefetch=1, grid=(db, n_pages // npages),
        in_specs=[pl.BlockSpec((1, nh, hd), lambda b, s, pt: (b, 0, 0)),
                  pl.BlockSpec((1, nh, hd), lambda b, s, pt: (b, 0, 0)),
                  pl.BlockSpec((1, nh, hd), lambda b, s, pt: (b, 0, 0)),
                  pl.BlockSpec((1, nh, npages * page), lambda b, s, pt: (b, 0, s)),
                  pl.BlockSpec(memory_space=pl.ANY), pl.BlockSpec(memory_space=pl.ANY)],
        out_specs=pl.BlockSpec((1, nh, hd), lambda b, s, pt: (b, 0, 0)),
        scratch_shapes=[pltpu.VMEM((2, npages * rows, hd), F32), pltpu.VMEM((2, npages * rows, hd), F32),
                        pltpu.SemaphoreType.DMA((2, 2)),
                        pltpu.VMEM((nh, 1), F32), pltpu.VMEM((nh, 1), F32), pltpu.VMEM((nh, hd), F32)])
    return pl.pallas_call(
        functools.partial(_fox_dec_kernel, layer=layer, npages=npages, nh=nh),
        grid_spec=grid_spec,
        out_shape=jax.ShapeDtypeStruct((db, nh, hd), BF),
        compiler_params=_params(("arbitrary", "arbitrary")),
        name="fox_dec",
    )(page_table, q, kn, vn, bias, k_pool, v_pool)


def _rows(fn, n):
    return jnp.concatenate([fn(g) for g in range(n)], axis=0)


def _nsa_dec_a_kernel(pt_ref, q_ref, w1k_ref, w1v_ref, pek_ref, pev_ref, w1fk_ref, w1fv_ref, w2k_ref,
                      w2v_ref, ov_ref, tbl_ref, kc_hbm, vc_hbm, oc_ref, ti_ref, ok_ref,
                      kbuf, vbuf, sem, fsk_ref, fsv_ref, *, layer, npages, nkv, grp, hd, pos, nsel):
    s_id = pl.program_id(1)
    ns = pl.num_programs(1)
    rows = kbuf.shape[1] // npages
    slot = _paged_pipeline(
        pt_ref, (kc_hbm, vc_hbm),
        lambda sl: (lambda j: kbuf.at[sl, pl.ds(j * rows, rows), :], lambda j: vbuf.at[sl, pl.ds(j * rows, rows), :]),
        lambda sl: (sem.at[0, sl], sem.at[1, sl]), layer, npages)
    nchunk = kbuf.shape[1] // (nkv * CMP_STRIDE)
    base = pl.multiple_of(s_id * nchunk, nchunk)
    for buf, w_ref, fs_ref in ((kbuf, w1k_ref, fsk_ref), (vbuf, w1v_ref, fsv_ref)):
        for k in range(nkv):
            y = jnp.zeros((nchunk, 2 * hd), F32)
            for r in range(CMP_STRIDE):
                xr = buf[slot, pl.ds(nkv * r + k, nchunk, stride=nkv * CMP_STRIDE), :].astype(BF)
                y = y + _dot(xr, w_ref[r])
            fs_ref[k, pl.ds(base, nchunk), :] = y

    @pl.when(s_id == ns - 1)
    def _():
        ncmp = fsk_ref.shape[1]
        n_valid = (pos - (CMP_BLOCK - 1)) // CMP_STRIDE + 1
        lane = lax.broadcasted_iota(I32, (1, ncmp), 1)
        dist_c = pos - (lane * CMP_STRIDE + CMP_BLOCK - 1)
        pe_k = _dot(pek_ref[...], w1fk_ref[...])[0:1, :]
        pe_v = _dot(pev_ref[...], w1fv_ref[...])[0:1, :]
        imps = []
        for k in range(nkv):
            def cmp_rows(fs_ref, pe, w2_ref):
                fs = fs_ref[k]
                second = pltpu.roll(fs, ncmp - 1, axis=0)[:, hd:2 * hd]
                hid = _gelu(fs[:, 0:hd] + second + pe).astype(BF)
                return _dot(hid, w2_ref[...]).astype(BF)

            kcmp = cmp_rows(fsk_ref, pe_k, w2k_ref)
            vcmp = cmp_rows(fsv_ref, pe_v, w2v_ref)
            qk = q_ref[0, k * grp:(k + 1) * grp, :].astype(BF)
            bias = _rows(lambda g: _t5_bias(dist_c, lambda j: tbl_ref[j, k * grp + g]), grp)
            valid = lane < n_valid
            s = jnp.where(valid, _dot_nt(qk, kcmp) + bias, NEG_BIG)
            mx = jnp.max(s, axis=1, keepdims=True)
            e = jnp.where(valid, jnp.exp(s - mx), 0.0)
            p = (e / jnp.maximum(jnp.sum(e, axis=1, keepdims=True), 1e-30)).astype(BF)
            oc_ref[0, k * grp:(k + 1) * grp, :] = _dot(p, vcmp)
            imps.append(jnp.sum(_dot(p, ov_ref[...]), axis=0, keepdims=True))
        imp = jnp.concatenate(imps, axis=0)
        jl = lax.broadcasted_iota(I32, imp.shape, 1)
        cur = pos // SEL_BLOCK
        forced = (jl == 0) | (jl == cur) | (jl == cur - 1)
        score = jnp.where(forced, FORCE_SCORE, imp)
        score = jnp.where(jl * SEL_BLOCK <= pos, score, NEG_SCORE)
        score = jnp.where(jl < nsel, score, -jnp.inf)
        ol = lax.broadcasted_iota(I32, (nkv, LANES), 1)
        ti = jnp.zeros((nkv, LANES), I32)
        ok = jnp.zeros((nkv, LANES), I32)
        for n, _, idx, val in _topk_steps(score, min(SEL_TOP_N, nsel)):
            ti = jnp.where(ol == n, idx.astype(I32), ti)
            ok = jnp.where(ol == n, (val > 0.5 * NEG_SCORE).astype(I32), ok)
        ti_ref[0] = ti
        ok_ref[0] = ok


def _nsa_dec_a(page_table, q, w1k, w1v, pek, pev, w1fk, w1fv, w2k, w2v, ov, t5_table, kc_pool, vc_pool,
               layer, nkv, grp, hd, pos, nsel, npages=16):
    db, n_pages = page_table.shape
    rows = kc_pool.shape[2]
    page = rows // nkv
    ncmp = n_pages * page // CMP_STRIDE
    consts = [w1k, w1v, pek, pev, w1fk, w1fv, w2k, w2v, ov]
    grid_spec = pltpu.PrefetchScalarGridSpec(
        num_scalar_prefetch=1, grid=(db, n_pages // npages),
        in_specs=([pl.BlockSpec((1, nkv * grp, hd), lambda b, s, pt: (b, 0, 0))]
                  + [_const_spec(c.shape) for c in consts]
                  + [pl.BlockSpec(memory_space=pltpu.SMEM),
                     pl.BlockSpec(memory_space=pl.ANY), pl.BlockSpec(memory_space=pl.ANY)]),
        out_specs=[pl.BlockSpec((1, nkv * grp, hd), lambda b, s, pt: (b, 0, 0)),
                   pl.BlockSpec((1, nkv, LANES), lambda b, s, pt: (b, 0, 0)),
                   pl.BlockSpec((1, nkv, LANES), lambda b, s, pt: (b, 0, 0))],
        scratch_shapes=[pltpu.VMEM((2, npages * rows, hd), F32), pltpu.VMEM((2, npages * rows, hd), F32),
                        pltpu.SemaphoreType.DMA((2, 2)),
                        pltpu.VMEM((nkv, ncmp, 2 * hd), F32), pltpu.VMEM((nkv, ncmp, 2 * hd), F32)])
    return pl.pallas_call(
        functools.partial(_nsa_dec_a_kernel, layer=layer, npages=npages, nkv=nkv, grp=grp, hd=hd,
                          pos=pos, nsel=nsel),
        grid_spec=grid_spec,
        out_shape=[jax.ShapeDtypeStruct((db, nkv * grp, hd), F32),
                   jax.ShapeDtypeStruct((db, nkv, LANES), I32),
                   jax.ShapeDtypeStruct((db, nkv, LANES), I32)],
        compiler_params=_params(("arbitrary", "arbitrary")),
        name="nsa_dec_cmp",
    )(page_table, q, *consts, t5_table, kc_pool, vc_pool)


def _nsa_dec_b_kernel(pt_ref, ti_ref, ok_ref, q_ref, ksn_ref, vsn_ref, kwn_ref, vwn_ref, wk_ref, wv_ref,
                      gt_ref, oc_ref, tbl_ref, ks_hbm, vs_hbm, o_ref, kb, vb, sem,
                      *, layer, nkv, grp, hd, pos, npast_sel, blk_rows, blocks_per_page, nwin):
    b = pl.program_id(0)
    ntop = kb.shape[1] // blk_rows

    def sel_dma(k, n, wait):
        blk = ti_ref[b, k * ntop + n]

        @pl.when(blk < npast_sel)
        def _():
            pg = pt_ref[b, blk // blocks_per_page]
            off = pl.multiple_of((blk % blocks_per_page) * blk_rows, blk_rows)
            for pool, buf, sm in ((ks_hbm, kb, sem.at[0]), (vs_hbm, vb, sem.at[1])):
                c = pltpu.make_async_copy(pool.at[layer, pg, pl.ds(off, blk_rows), :],
                                          buf.at[k, pl.ds(n * blk_rows, blk_rows), :], sm)
                if wait:
                    c.wait()
                else:
                    c.start()

        if not wait:
            @pl.when(blk >= npast_sel)
            def _():
                for buf, new_ref in ((kb, ksn_ref), (vb, vsn_ref)):
                    buf[k, pl.ds(n * blk_rows, blk_rows), :] = jnp.zeros((blk_rows, hd), F32)
                    buf[k, pl.ds(n * blk_rows + k, 1), :] = new_ref[0, k:k + 1, :]

    for k in range(nkv):
        for n in range(ntop):
            sel_dma(k, n, False)

    def head_col(fn):
        r = lax.broadcasted_iota(I32, (grp, 1), 0)
        out = jnp.zeros((grp, 1), F32)
        for g in range(grp):
            out = jnp.where(r == g, fn(g), out)
        return out

    def bf_round(x):
        return x.astype(BF).astype(F32)

    o_wins = []
    jw = lax.broadcasted_iota(I32, (1, nwin), 1)
    dist_w = nwin - jw
    for k in range(nkv):
        qk = q_ref[0, k * grp:(k + 1) * grp, :]
        kw = wk_ref[0, 0, pl.ds(k, nwin, stride=nkv), :].astype(BF)
        vw = wv_ref[0, 0, pl.ds(k, nwin, stride=nkv), :].astype(BF)
        bias = _rows(lambda g: _t5_bias(dist_w, lambda j: tbl_ref[j, k * grp + g]), grp)
        s = _dot_nt(qk.astype(BF), kw) + bias
        s_new = (jnp.sum(qk * bf_round(kwn_ref[0, k:k + 1, :]), axis=1, keepdims=True)
                 + head_col(lambda g: tbl_ref[0, k * grp + g]))
        mx = jnp.maximum(jnp.max(s, axis=1, keepdims=True), s_new)
        e = jnp.exp(s - mx)
        e_new = jnp.exp(s_new - mx)
        l = jnp.sum(e, axis=1, keepdims=True) + e_new
        o_wins.append((_dot(e.astype(BF), vw) + bf_round(e_new) * bf_round(vwn_ref[0, k:k + 1, :])) / l)

    for k in range(nkv):
        for n in range(ntop):
            sel_dma(k, n, True)

    nkeys = ntop * SEL_BLOCK
    lane = lax.broadcasted_iota(I32, (1, nkeys), 1)
    slot_id = lane // SEL_BLOCK
    for k in range(nkv):
        qk = q_ref[0, k * grp:(k + 1) * grp, :]
        ks = kb[k, pl.ds(k, nkeys, stride=nkv), :].astype(BF)
        vs = vb[k, pl.ds(k, nkeys, stride=nkv), :].astype(BF)
        blkv = jnp.zeros((1, nkeys), I32)
        okv = jnp.zeros((1, nkeys), I32)
        for n in range(ntop):
            blkv = jnp.where(slot_id == n, ti_ref[b, k * ntop + n], blkv)
            okv = jnp.where(slot_id == n, ok_ref[b, k * ntop + n], okv)
        dist = pos - (blkv * SEL_BLOCK + lane % SEL_BLOCK)
        mask = (okv > 0) & (dist >= 0)
        bias = _rows(lambda g: _t5_bias(dist, lambda j: tbl_ref[j, k * grp + g]), grp)
        s = jnp.where(mask, _dot_nt(qk.astype(BF), ks) + bias, NEG_BIG)
        mx = jnp.max(s, axis=1, keepdims=True)
        e = jnp.where(mask, jnp.exp(s - mx), 0.0)
        l = jnp.maximum(jnp.sum(e, axis=1, keepdims=True), 1e-30)
        o_slc = _dot(e.astype(BF), vs) / l
        gt = gt_ref[0, k * grp:(k + 1) * grp, :]
        o = (gt[:, 0:1] * oc_ref[0, k * grp:(k + 1) * grp, :] + gt[:, 1:2] * o_slc + gt[:, 2:3] * o_wins[k])
        o_ref[0, k * grp:(k + 1) * grp, :] = o.astype(BF)


def _nsa_dec_b(page_table, top_i, top_ok, q, ksn, vsn, kwn, vwn, win_k, win_v, gates, o_cmp, t5_table,
               ks_pool, vs_pool, layer, nkv, grp, hd, pos, npast_sel):
    db = q.shape[0]
    nh = nkv * grp
    ntop = top_i.shape[1] // nkv
    page_rows = ks_pool.shape[2]
    blk_rows = SEL_BLOCK * nkv
    nwin = win_k.shape[2] // nkv
    m3 = lambda b, *_: (b, 0, 0)
    grid_spec = pltpu.PrefetchScalarGridSpec(
        num_scalar_prefetch=3, grid=(db,),
        in_specs=[pl.BlockSpec((1, nh, hd), m3)] + [pl.BlockSpec((1, nkv, hd), m3)] * 4
                 + [pl.BlockSpec((1, 1) + win_k.shape[2:], lambda b, *_: (layer, b, 0, 0))] * 2
                 + [pl.BlockSpec((1, nh, 3), m3), pl.BlockSpec((1, nh, hd), m3),
                    pl.BlockSpec(memory_space=pltpu.SMEM),
                    pl.BlockSpec(memory_space=pl.ANY), pl.BlockSpec(memory_space=pl.ANY)],
        out_specs=pl.BlockSpec((1, nh, hd), m3),
        scratch_shapes=[pltpu.VMEM((nkv, ntop * blk_rows, hd), F32),
                        pltpu.VMEM((nkv, ntop * blk_rows, hd), F32),
                        pltpu.SemaphoreType.DMA((2,))])
    return pl.pallas_call(
        functools.partial(_nsa_dec_b_kernel, layer=layer, nkv=nkv, grp=grp, hd=hd, pos=pos,
                          npast_sel=npast_sel, blk_rows=blk_rows,
                          blocks_per_page=page_rows // blk_rows, nwin=nwin),
        grid_spec=grid_spec,
        out_shape=jax.ShapeDtypeStruct((db, nh, hd), BF),
        compiler_params=_params(("arbitrary",)),
        name="nsa_dec_sel_win",
    )(page_table, top_i, top_ok, q, ksn, vsn, kwn, vwn, win_k, win_v, gates, o_cmp, t5_table,
      ks_pool, vs_pool)


KM_TQ = 128
KM_TK = 256


LOG2E = 1.4426950408889634


def _t5_km_kernel(tbl_ref, tiles_ref, bc_ref, *, t, ncmp):
    h = pl.program_id(0)
    far = tbl_ref[T5_BUCKETS - 1, h]
    rel = lambda j: (tbl_ref[j, h] - far) * LOG2E
    c = lax.broadcasted_iota(I32, (KM_TK, KM_TQ), 0)
    r = lax.broadcasted_iota(I32, (KM_TK, KM_TQ), 1)
    for i in range(3):
        dist = i * KM_TQ + r - c
        tiles_ref[0, i] = jnp.where(dist >= 0, _t5_bias(dist, rel), NEG_BIG)
    tiles_ref[0, 3] = jnp.zeros((KM_TK, KM_TQ), F32)
    n = lax.broadcasted_iota(I32, (ncmp, t), 0)
    tt = lax.broadcasted_iota(I32, (ncmp, t), 1)
    bc_ref[0] = _t5_bias(tt - (n * CMP_STRIDE + CMP_BLOCK - 1), lambda j: tbl_ref[j, h] * LOG2E)


def _t5_km(t5_table, t, ncmp):
    nh = t5_table.shape[1]
    return pl.pallas_call(
        functools.partial(_t5_km_kernel, t=t, ncmp=ncmp),
        grid=(nh,),
        in_specs=[pl.BlockSpec(memory_space=pltpu.SMEM)],
        out_specs=[pl.BlockSpec((1, 4, KM_TK, KM_TQ), lambda i: (i, 0, 0, 0)),
                   pl.BlockSpec((1, ncmp, t), lambda i: (i, 0, 0))],
        out_shape=[jax.ShapeDtypeStruct((nh, 4, KM_TK, KM_TQ), F32),
                   jax.ShapeDtypeStruct((nh, ncmp, t), F32)],
        compiler_params=_params(("arbitrary",)),
        name="t5_tiles",
    )(t5_table)


def _lanes(fn, n):
    return jnp.concatenate([fn(g) for g in range(n)], axis=1)


def _nsa_km_kernel(q_ref, kcmp_ref, vcmpt_ref, ks_ref, vst_ref, kw_ref, vwt_ref, gtt_ref, bc_ref,
                   tiles_ref, ovt_ref, o_ref, selt_ref, *, t, nkv, grp, hd):
    qi = pl.program_id(1)
    tq, tk = KM_TQ, KM_TK
    nq4 = grp * tq
    ncmp = kcmp_ref.shape[2]
    nsel = t // SEL_BLOCK
    per_tile = tk // SEL_BLOCK
    tcol = qi * tq + lax.broadcasted_iota(I32, (1, nq4), 1) % tq
    krow = lax.broadcasted_iota(I32, (tk, 1), 0)
    last = qi // 2
    odd = qi % 2

    q4s = [q_ref[0, k * grp:(k + 1) * grp].reshape(nq4, hd) for k in range(nkv)]
    o_cmps = []
    for k in range(nkv):
        q4 = q4s[k]

        nrow = lax.broadcasted_iota(I32, (ncmp, 1), 0)
        mask_c = (nrow * CMP_STRIDE + CMP_BLOCK - 1) <= tcol
        s = _dot_nt(kcmp_ref[0, k], q4) + _lanes(lambda g: bc_ref[k * grp + g], grp)
        s = jnp.where(mask_c, s, NEG_BIG)
        e = jnp.where(mask_c, jnp.exp2(s - jnp.max(s, axis=0, keepdims=True)), 0.0)
        pb = (e / jnp.maximum(jnp.sum(e, axis=0, keepdims=True), 1e-30)).astype(BF)
        o_cmp = _dot(vcmpt_ref[0, k * hd:(k + 1) * hd, :], pb)
        imp4 = _dot(ovt_ref[...], pb)
        imp = imp4[:, 0:tq]
        for g in range(1, grp):
            imp = imp + imp4[:, g * tq:(g + 1) * tq]

        jrow = lax.broadcasted_iota(I32, imp.shape, 0)
        trow = qi * tq + lax.broadcasted_iota(I32, imp.shape, 1)
        cur = trow // SEL_BLOCK
        forced = (jrow == 0) | (jrow == cur) | (jrow == cur - 1)
        score = jnp.where(forced, FORCE_SCORE, imp)
        score = jnp.where(jrow * SEL_BLOCK <= trow, score, NEG_SCORE)
        score = jnp.where(jrow < nsel, score, -jnp.inf)
        ok = score > 0.5 * NEG_SCORE
        jf = jrow.astype(F32)
        sel = jnp.zeros(imp.shape, F32)
        for _ in range(min(SEL_TOP_N, nsel)):
            mx = jnp.max(score, axis=0, keepdims=True)
            idx = jnp.min(jnp.where(score == mx, jf, float(imp.shape[0])), axis=0, keepdims=True)
            hit = jf == idx
            sel = jnp.where(hit, 1.0, sel)
            score = jnp.where(hit, -jnp.inf, score)
        selt_ref[k] = jnp.where(ok & (sel > 0.5), 0.0, NEG_BIG)
        o_cmps.append(o_cmp)

    def tile(k, ki, carry, kref, vtref, bias):
        m_p, l_p, acc = carry
        start = pl.multiple_of(ki * tk, tk)
        st = _dot_nt(kref[0, k, pl.ds(start, tk), :], q4s[k]) + bias
        m_n = jnp.maximum(m_p, jnp.max(st, axis=0, keepdims=True))
        p = jnp.exp2(st - m_n)
        alpha = jnp.exp2(m_p - m_n)
        l_n = alpha * l_p + jnp.sum(p, axis=0, keepdims=True)
        vt = vtref[0, k * hd:(k + 1) * hd, pl.ds(start, tk)]
        return m_n, l_n, alpha * acc + _dot(vt, p.astype(BF))

    def near_bias(k, which):
        return _lanes(lambda g: tiles_ref[k * grp + g, which], grp)

    def sel_bias(k, ki):
        rows = [jnp.broadcast_to(selt_ref[k, pl.ds(ki * per_tile + j, 1), :], (SEL_BLOCK, tq))
                for j in range(per_tile)]
        return jnp.concatenate([jnp.concatenate(rows, axis=0)] * grp, axis=1)

    init = (jnp.full((1, nq4), NEG_BIG, F32), jnp.zeros((1, nq4), F32), jnp.zeros((hd, nq4), F32))

    far = lax.fori_loop(
        0, jnp.maximum(last - 1, 0),
        lambda ki, cs: tuple(tile(k, ki, cs[k], ks_ref, vst_ref, sel_bias(k, ki)) for k in range(nkv)),
        (init,) * nkv)

    k1 = jnp.maximum(last - 1, 0)
    k2 = jnp.maximum(last - 2, 0)
    pen1 = jnp.where(last >= 1, 0.0, NEG_BIG)
    win_far = jnp.where((tcol - (k2 * tk + krow) <= WINDOW) & (last >= 2), 0.0, NEG_BIG)
    for k in range(nkv):
        c = tile(k, k1, far[k], ks_ref, vst_ref, near_bias(k, 2 + odd) + sel_bias(k, k1) + pen1)
        _, l_s, acc_s = tile(k, last, c, ks_ref, vst_ref, near_bias(k, odd) + sel_bias(k, last))
        o_slc = acc_s / l_s

        c = tile(k, k2, init, kw_ref, vwt_ref, win_far)
        c = tile(k, k1, c, kw_ref, vwt_ref, near_bias(k, 2 + odd) + pen1)
        _, l_w, acc_w = tile(k, last, c, kw_ref, vwt_ref, near_bias(k, odd))
        o_win = acc_w / l_w

        gate = lambda br: _lanes(lambda g: gtt_ref[0, pl.ds(3 * (k * grp + g) + br, 1), :], grp)
        o_t = gate(0) * o_cmps[k] + gate(1) * o_slc + gate(2) * o_win
        for g in range(grp):
            hh = k * grp + g
            o_ref[0, :, hh * hd:(hh + 1) * hd] = o_t[:, g * tq:(g + 1) * tq].T.astype(BF)


def _nsa_attn_km(q, kcmp, vcmpt, ksh, vst, kwh, vwt, gatest, bias_ct, tiles, ovt, nkv, grp, hd):
    b, nh, t, _ = q.shape
    tq = KM_TQ
    assert WINDOW == 2 * KM_TK and t % KM_TK == 0
    full_b = lambda a: pl.BlockSpec((1,) + a.shape[1:], lambda bi, qi: (bi,) + (0,) * (a.ndim - 1))
    single = lambda a: pl.BlockSpec(a.shape, lambda bi, qi: (0,) * a.ndim, pipeline_mode=pl.Buffered(1))
    return pl.pallas_call(
        functools.partial(_nsa_km_kernel, t=t, nkv=nkv, grp=grp, hd=hd),
        grid=(b, t // tq),
        in_specs=[pl.BlockSpec((1, nh, tq, hd), lambda bi, qi: (bi, 0, qi, 0)),
                  full_b(kcmp), full_b(vcmpt), full_b(ksh), full_b(vst), full_b(kwh), full_b(vwt),
                  pl.BlockSpec((1, LANES, tq), lambda bi, qi: (bi, 0, qi)),
                  pl.BlockSpec((nh, bias_ct.shape[1], tq), lambda bi, qi: (0, 0, qi)),
                  single(tiles), _const_spec(ovt.shape)],
        out_specs=pl.BlockSpec((1, tq, nh * hd), lambda bi, qi: (bi, qi, 0)),
        out_shape=jax.ShapeDtypeStruct((b, t, nh * hd), BF),
        scratch_shapes=[pltpu.VMEM((nkv, LANES, tq), F32)],
        compiler_params=_params(("parallel", "arbitrary")),
        name="nsa_attn",
    )(q, kcmp, vcmpt, ksh, vst, kwh, vwt, gatest, bias_ct, tiles, ovt)


def _step_pipeline(copies):
    b = pl.program_id(0)
    s = pl.program_id(1)
    nb = pl.num_programs(0)
    ns = pl.num_programs(1)
    g = b * ns + s
    slot = g % 2

    @pl.when(g == 0)
    def _():
        for c in copies(b, s, slot):
            c.start()

    @pl.when(g + 1 < nb * ns)
    def _():
        wrap = s + 1 == ns
        for c in copies(jnp.where(wrap, b + 1, b), jnp.where(wrap, 0, s + 1), 1 - slot):
            c.start()

    for c in copies(b, s, slot):
        c.wait()
    return slot


def _diag_rows(r, width):
    row = lax.broadcasted_iota(I32, r.shape, 0)
    col = lax.broadcasted_iota(I32, r.shape, 1)
    return jnp.sum(jnp.where(col // width == row, r, 0.0), axis=0, keepdims=True)


def _bf_round(x):
    return x.astype(BF).astype(F32)


def _mla_dec2_kernel(pt_ref, q_ref, knew_ref, wuv_ref, ckv_hbm, krt_hbm, o_ref,
                     cbuf, rbuf, sem, m_ref, l_ref, acc_ref, *, layer, npages, scale, vdim):
    s_id = pl.program_id(1)
    ns = pl.num_programs(1)
    page = cbuf.shape[2]

    def copies(bb, ss, sl):
        out = []
        for j in range(npages):
            pg = pt_ref[bb, ss * npages + j]
            out.append(pltpu.make_async_copy(ckv_hbm.at[layer, pg], cbuf.at[sl, j], sem.at[0, sl]))
            out.append(pltpu.make_async_copy(krt_hbm.at[layer, pg], rbuf.at[sl, :, pl.ds(j * page, page)],
                                             sem.at[1, sl]))
        return out

    slot = _step_pipeline(copies)

    @pl.when(s_id == 0)
    def _():
        m_ref[...] = jnp.full(m_ref.shape, NEG_BIG, F32)
        l_ref[...] = jnp.zeros(l_ref.shape, F32)
        acc_ref[...] = jnp.zeros(acc_ref.shape, F32)

    q = q_ref[0]
    kp = cbuf[slot].reshape(npages * page, cbuf.shape[3]).astype(BF)
    sc = (_dot_nt(q[:, 0:128].astype(BF), kp)
          + _dot(q[:, 128:128 + MLA_ROPE].astype(BF), rbuf[slot].astype(BF))) * scale
    p, alpha, m_new, l_new = _softmax_step(sc, m_ref[...], l_ref[...])
    acc_ref[...] = alpha * acc_ref[...] + _dot(p.astype(BF), kp)
    m_ref[...] = m_new
    l_ref[...] = l_new

    @pl.when(s_id == ns - 1)
    def _():
        kn = knew_ref[0]
        s_new = jnp.sum(q * kn, axis=1, keepdims=True) * scale
        m_p = m_ref[...]
        m_n = jnp.maximum(m_p, s_new)
        a = jnp.exp(m_p - m_n)
        p_new = jnp.exp(s_new - m_n)
        l_f = a * l_ref[...] + p_new
        acc = a * acc_ref[...] + _bf_round(p_new) * kn[:, 0:128]
        olat = (acc / l_f).astype(BF)
        o_ref[0] = _diag_rows(_dot(olat, wuv_ref[...]), vdim).astype(BF)


def _mla_dec2(page_table, q, knew, wuv_all, ckv_pool, krt_pool, layer, nh, scale, npages=16):
    db, n_pages = page_table.shape
    page = ckv_pool.shape[2]
    vdim = wuv_all.shape[1] // nh
    grid_spec = pltpu.PrefetchScalarGridSpec(
        num_scalar_prefetch=1, grid=(db, n_pages // npages),
        in_specs=[pl.BlockSpec((1, nh, 256), lambda b, s, pt: (b, 0, 0)),
                  pl.BlockSpec((1, 1, 256), lambda b, s, pt: (b, 0, 0)),
                  _const_spec(wuv_all.shape),
                  pl.BlockSpec(memory_space=pl.ANY), pl.BlockSpec(memory_space=pl.ANY)],
        out_specs=pl.BlockSpec((1, 1, wuv_all.shape[1]), lambda b, s, pt: (b, 0, 0)),
        scratch_shapes=[pltpu.VMEM((2, npages, page, ckv_pool.shape[3]), F32),
                        pltpu.VMEM((2, krt_pool.shape[2], npages * page), F32),
                        pltpu.SemaphoreType.DMA((2, 2)),
                        pltpu.VMEM((nh, 1), F32), pltpu.VMEM((nh, 1), F32), pltpu.VMEM((nh, 128), F32)])
    return pl.pallas_call(
        functools.partial(_mla_dec2_kernel, layer=layer, npages=npages, scale=scale, vdim=vdim),
        grid_spec=grid_spec,
        out_shape=jax.ShapeDtypeStruct((db, 1, wuv_all.shape[1]), BF),
        compiler_params=_params(("arbitrary", "arbitrary")),
        name="mla_dec",
    )(page_table, q, knew, wuv_all, ckv_pool, krt_pool)


def _suffix_sum_lanes(x):
    n = x.shape[1]
    lane = lax.broadcasted_iota(I32, x.shape, 1)
    y = x
    k = 1
    while k < n:
        y = y + jnp.where(lane + k < n, pltpu.roll(y, n - k, axis=1), 0.0)
        k *= 2
    return y


def _fox_dec2_kernel(pt_ref, q_ref, kn_ref, vn_ref, lfn_ref, kt_hbm, vt_hbm, lft_hbm, o_ref,
                     kbuf, vbuf, lbuf, sem, m_ref, l_ref, acc_ref, carry_ref, *, layer, npages, hd):
    s_id = pl.program_id(1)
    ns = pl.num_programs(1)
    page = kbuf.shape[2] // npages

    def copies(bb, ss, sl):
        out = []
        first = (ns - 1 - ss) * npages
        for j in range(npages):
            pg = pt_ref[bb, first + j]
            win = pl.ds(j * page, page)
            out.append(pltpu.make_async_copy(kt_hbm.at[layer, pg], kbuf.at[sl, :, win], sem.at[0, sl]))
            out.append(pltpu.make_async_copy(vt_hbm.at[layer, pg], vbuf.at[sl, :, win], sem.at[1, sl]))
            out.append(pltpu.make_async_copy(lft_hbm.at[layer, pg], lbuf.at[sl, :, win], sem.at[2, sl]))
        return out

    slot = _step_pipeline(copies)

    @pl.when(s_id == 0)
    def _():
        m_ref[...] = jnp.full(m_ref.shape, NEG_BIG, F32)
        l_ref[...] = jnp.zeros(l_ref.shape, F32)
        acc_ref[...] = jnp.zeros(acc_ref.shape, F32)
        carry_ref[...] = lfn_ref[0]

    q = q_ref[0]
    lf = lbuf[slot]
    incl = _suffix_sum_lanes(lf)
    carry = carry_ref[...]
    sc = _dot(q.astype(BF), kbuf[slot].astype(BF)) + ((incl - lf) + carry)
    carry_ref[...] = carry + incl[:, 0:1]
    p, alpha, m_new, l_new = _softmax_step(sc, m_ref[...], l_ref[...])
    acc_ref[...] = alpha * acc_ref[...] + _dot_nt(p.astype(BF), vbuf[slot].astype(BF))
    m_ref[...] = m_new
    l_ref[...] = l_new

    @pl.when(s_id == ns - 1)
    def _():
        s_new = jnp.sum(q * _bf_round(kn_ref[0]), axis=1, keepdims=True)
        m_p = m_ref[...]
        m_n = jnp.maximum(m_p, s_new)
        a = jnp.exp(m_p - m_n)
        p_new = jnp.exp(s_new - m_n)
        l_f = a * l_ref[...] + p_new
        acc = a * acc_ref[...] + _bf_round(p_new) * _bf_round(vn_ref[0])
        o_ref[0] = _diag_rows(acc / l_f, hd).astype(BF)


def _fox_dec2(page_table, qblk, kn, vn, lf_new, kt_pool, vt_pool, lft_pool, layer, nh, hd, npages=8):
    db, n_pages = page_table.shape
    page = kt_pool.shape[3]
    w = nh * hd
    m3 = lambda b, s, pt: (b, 0, 0)
    grid_spec = pltpu.PrefetchScalarGridSpec(
        num_scalar_prefetch=1, grid=(db, n_pages // npages),
        in_specs=[pl.BlockSpec((1, nh, w), m3), pl.BlockSpec((1, 1, w), m3), pl.BlockSpec((1, 1, w), m3),
                  pl.BlockSpec((1, nh, 1), m3),
                  pl.BlockSpec(memory_space=pl.ANY), pl.BlockSpec(memory_space=pl.ANY),
                  pl.BlockSpec(memory_space=pl.ANY)],
        out_specs=pl.BlockSpec((1, 1, w), m3),
        scratch_shapes=[pltpu.VMEM((2, w, npages * page), F32), pltpu.VMEM((2, w, npages * page), F32),
                        pltpu.VMEM((2, nh, npages * page), F32), pltpu.SemaphoreType.DMA((3, 2)),
                        pltpu.VMEM((nh, 1), F32), pltpu.VMEM((nh, 1), F32), pltpu.VMEM((nh, w), F32),
                        pltpu.VMEM((nh, 1), F32)])
    return pl.pallas_call(
        functools.partial(_fox_dec2_kernel, layer=layer, npages=npages, hd=hd),
        grid_spec=grid_spec,
        out_shape=jax.ShapeDtypeStruct((db, 1, w), BF),
        compiler_params=_params(("arbitrary", "arbitrary")),
        name="fox_dec",
    )(page_table, qblk, kn, vn, lf_new, kt_pool, vt_pool, lft_pool)


def _head_rows(fn, nh):
    return jnp.concatenate([fn(h) for h in range(nh)], axis=0)


def _nsa_cmp_dec_kernel(pt_ref, q_ref, perm_ref, wk_ref, wv_ref, pek_ref, pev_ref, w1k_ref, w1v_ref,
                        w2k_ref, w2v_ref, ov_ref, tbl_ref, kc_hbm, vc_hbm, oc_ref, ti_ref, ok_ref,
                        kbuf, vbuf, sem, xpk_ref, xpv_ref, fsk_ref, fsv_ref,
                        *, layer, npages, nkv, grp, pos, nsel):
    s_id = pl.program_id(1)
    ns = pl.num_programs(1)
    width = kbuf.shape[2]
    per_page = kbuf.shape[3] // CMP_STRIDE
    nchunk = npages * per_page

    def copies(bb, ss, sl):
        out = []
        for j in range(npages):
            pg = pt_ref[bb, ss * npages + j]
            out.append(pltpu.make_async_copy(kc_hbm.at[layer, pg], kbuf.at[sl, j], sem.at[0, sl]))
            out.append(pltpu.make_async_copy(vc_hbm.at[layer, pg], vbuf.at[sl, j], sem.at[1, sl]))
        return out

    slot = _step_pipeline(copies)
    base = pl.multiple_of(s_id * nchunk, nchunk)
    for buf, xp_ref, w_ref, fs_ref in ((kbuf, xpk_ref, wk_ref, fsk_ref), (vbuf, xpv_ref, wv_ref, fsv_ref)):
        for j in range(npages):
            xp = _dot_nt(perm_ref[...], buf[slot, j].astype(BF))
            for r in range(CMP_STRIDE):
                xp_ref[r, j * per_page:(j + 1) * per_page, :] = xp[r * per_page:(r + 1) * per_page, :]
        y = jnp.zeros((nchunk, 2 * width), F32)
        for r in range(CMP_STRIDE):
            y = y + _dot(xp_ref[r].astype(BF), w_ref[r])
        fs_ref[pl.ds(base, nchunk), :] = y

    @pl.when(s_id == ns - 1)
    def _():
        ncmp = fsk_ref.shape[0]
        nh = nkv * grp
        n_valid = (pos - (CMP_BLOCK - 1)) // CMP_STRIDE + 1
        lane = lax.broadcasted_iota(I32, (1, ncmp), 1)
        dist_c = pos - (lane * CMP_STRIDE + CMP_BLOCK - 1)
        kcmp = _compress_tail(fsk_ref[...], pek_ref, w1k_ref, w2k_ref, width).astype(BF)
        vcmp = _compress_tail(fsv_ref[...], pev_ref, w1v_ref, w2v_ref, width).astype(BF)
        bias = _head_rows(lambda h: _t5_bias(dist_c, lambda j: tbl_ref[j, h]), nh)
        valid = lane < n_valid
        s = jnp.where(valid, _dot_nt(q_ref[0].astype(BF), kcmp) + bias, NEG_BIG)
        mx = jnp.max(s, axis=1, keepdims=True)
        e = jnp.where(valid, jnp.exp(s - mx), 0.0)
        p = (e / jnp.maximum(jnp.sum(e, axis=1, keepdims=True), 1e-30)).astype(BF)
        oc_ref[0] = _dot(p, vcmp)
        imp_h = _dot(p, ov_ref[...])
        imp = _head_rows(lambda k: sum(imp_h[k * grp + g:k * grp + g + 1, :] for g in range(grp)), nkv)
        jl = lax.broadcasted_iota(I32, imp.shape, 1)
        cur = pos // SEL_BLOCK
        forced = (jl == 0) | (jl == cur) | (jl == cur - 1)
        score = jnp.where(forced, FORCE_SCORE, imp)
        score = jnp.where(jl * SEL_BLOCK <= pos, score, NEG_SCORE)
        score = jnp.where(jl < nsel, score, -jnp.inf)
        ol = lax.broadcasted_iota(I32, (nkv, LANES), 1)
        ti = jnp.zeros((nkv, LANES), I32)
        ok = jnp.zeros((nkv, LANES), I32)
        for n, _, idx, val in _topk_steps(score, min(SEL_TOP_N, nsel)):
            ti = jnp.where(ol == n, idx.astype(I32), ti)
            ok = jnp.where(ol == n, (val > 0.5 * NEG_SCORE).astype(I32), ok)
        ti_ref[0] = ti
        ok_ref[0] = ok


def _nsa_cmp_dec(page_table, qblk, perm, wk, wv, pek, pev, w1k, w1v, w2k, w2v, ov, t5_table,
                 kc_pool, vc_pool, layer, nkv, grp, pos, nsel, npages=16):
    db, n_pages = page_table.shape
    width, page = kc_pool.shape[2], kc_pool.shape[3]
    nh = nkv * grp
    ncmp = n_pages * page // CMP_STRIDE
    nchunk = npages * page // CMP_STRIDE
    consts = [perm, wk, wv, pek, pev, w1k, w1v, w2k, w2v, ov]
    m3 = lambda b, s, pt: (b, 0, 0)
    grid_spec = pltpu.PrefetchScalarGridSpec(
        num_scalar_prefetch=1, grid=(db, n_pages // npages),
        in_specs=([pl.BlockSpec((1, nh, width), m3)] + [_const_spec(c.shape) for c in consts]
                  + [pl.BlockSpec(memory_space=pltpu.SMEM),
                     pl.BlockSpec(memory_space=pl.ANY), pl.BlockSpec(memory_space=pl.ANY)]),
        out_specs=[pl.BlockSpec((1, nh, width), m3), pl.BlockSpec((1, nkv, LANES), m3),
                   pl.BlockSpec((1, nkv, LANES), m3)],
        scratch_shapes=[pltpu.VMEM((2, npages, width, page), F32), pltpu.VMEM((2, npages, width, page), F32),
                        pltpu.SemaphoreType.DMA((2, 2)),
                        pltpu.VMEM((CMP_STRIDE, nchunk, width), F32), pltpu.VMEM((CMP_STRIDE, nchunk, width), F32),
                        pltpu.VMEM((ncmp, 2 * width), F32), pltpu.VMEM((ncmp, 2 * width), F32)])
    return pl.pallas_call(
        functools.partial(_nsa_cmp_dec_kernel, layer=layer, npages=npages, nkv=nkv, grp=grp, pos=pos, nsel=nsel),
        grid_spec=grid_spec,
        out_shape=[jax.ShapeDtypeStruct((db, nh, width), F32),
                   jax.ShapeDtypeStruct((db, nkv, LANES), I32),
                   jax.ShapeDtypeStruct((db, nkv, LANES), I32)],
        compiler_params=_params(("arbitrary", "arbitrary")),
        name="nsa_dec_cmp",
    )(page_table, qblk, *consts, t5_table, kc_pool, vc_pool)


def _nsa_sel_win_dec_kernel(pt_ref, ti_ref, ok_ref, q_ref, ksn_ref, vsn_ref, kwn_ref, vwn_ref, wk_ref, wv_ref,
                            gt_ref, oc_ref, tbl_ref, ks_hbm, vs_hbm, o_ref, kb, vb, sem,
                            *, layer, nkv, grp, pos, npast_sel, page, nwin):
    b = pl.program_id(0)
    nh = nkv * grp
    ntop = kb.shape[2] // page
    bpp = page // SEL_BLOCK

    def sel_dma(k, n, wait):
        blk = ti_ref[b, k * ntop + n]
        win = pl.ds(n * page, page)

        @pl.when(blk < npast_sel)
        def _():
            pg = pt_ref[b, blk // bpp]
            for pool, buf, sm in ((ks_hbm, kb, sem.at[0]), (vs_hbm, vb, sem.at[1])):
                c = pltpu.make_async_copy(pool.at[layer, pg], buf.at[k, :, win], sm)
                if wait:
                    c.wait()
                else:
                    c.start()

        if not wait:
            @pl.when(blk >= npast_sel)
            def _():
                kb[k, :, win] = jnp.zeros((kb.shape[1], page), F32)
                vb[k, :, win] = jnp.zeros((vb.shape[1], page), F32)

    for k in range(nkv):
        for n in range(ntop):
            sel_dma(k, n, False)

    q = q_ref[0]
    qb = q.astype(BF)
    tbl0 = _head_rows(lambda h: jnp.zeros((1, 1), F32) + tbl_ref[0, h], nh)

    jw = lax.broadcasted_iota(I32, (1, nwin), 1)
    bias_w = _head_rows(lambda h: _t5_bias(nwin - jw, lambda j: tbl_ref[j, h]), nh)
    s = _dot(qb, wk_ref[0, 0].astype(BF)) + bias_w
    s_new = jnp.sum(q * _bf_round(kwn_ref[0]), axis=1, keepdims=True) + tbl0
    mx = jnp.maximum(jnp.max(s, axis=1, keepdims=True), s_new)
    e = jnp.exp(s - mx)
    e_new = jnp.exp(s_new - mx)
    l = jnp.sum(e, axis=1, keepdims=True) + e_new
    o_win = (_dot_nt(e.astype(BF), wv_ref[0, 0].astype(BF)) + _bf_round(e_new) * _bf_round(vwn_ref[0])) / l

    for k in range(nkv):
        for n in range(ntop):
            sel_dma(k, n, True)

    nkeys = ntop * page
    lane = lax.broadcasted_iota(I32, (1, nkeys), 1)
    slot_id = lane // page
    row = lane % page
    gates = gt_ref[0]
    for k in range(nkv):
        rows = slice(k * grp, (k + 1) * grp)
        blkv = jnp.zeros((1, nkeys), I32)
        okv = jnp.zeros((1, nkeys), I32)
        new_sel = jnp.int32(0)
        for n in range(ntop):
            blk = ti_ref[b, k * ntop + n]
            okn = ok_ref[b, k * ntop + n]
            blkv = jnp.where(slot_id == n, blk, blkv)
            okv = jnp.where(slot_id == n, okn, okv)
            new_sel = new_sel + jnp.where((blk >= npast_sel) & (okn > 0), 1, 0)
        kpos = (blkv // bpp) * page + row
        dist = pos - kpos
        mask = (okv > 0) & (blkv < npast_sel) & (row // SEL_BLOCK == blkv % bpp) & (dist >= 0)
        bias = _head_rows(lambda g: _t5_bias(dist, lambda j: tbl_ref[j, k * grp + g]), grp)
        sk = jnp.where(mask, _dot(qb[rows], kb[k].astype(BF)) + bias, NEG_BIG)
        sk_new = jnp.sum(q[rows] * _bf_round(ksn_ref[0]), axis=1, keepdims=True) + tbl0[rows]
        sk_new = jnp.where(new_sel > 0, sk_new, NEG_BIG)
        mx = jnp.maximum(jnp.max(sk, axis=1, keepdims=True), sk_new)
        e = jnp.where(mask, jnp.exp(sk - mx), 0.0)
        e_new = jnp.where(new_sel > 0, jnp.exp(sk_new - mx), 0.0)
        l = jnp.maximum(jnp.sum(e, axis=1, keepdims=True) + e_new, 1e-30)
        o_slc = (_dot_nt(e.astype(BF), vb[k].astype(BF)) + _bf_round(e_new) * _bf_round(vsn_ref[0])) / l
        g3 = gates[rows]
        o_ref[0, rows, :] = g3[:, 0:1] * oc_ref[0, rows, :] + g3[:, 1:2] * o_slc + g3[:, 2:3] * o_win[rows]


def _nsa_sel_win_dec(page_table, top_i, top_ok, qblk, ksn, vsn, kwn, vwn, win_kt, win_vt, gates, o_cmp,
                     t5_table, ks_pool, vs_pool, layer, nkv, grp, pos, npast_sel):
    db = qblk.shape[0]
    nh = nkv * grp
    ntop = top_i.shape[1] // nkv
    width, page = ks_pool.shape[2], ks_pool.shape[3]
    nwin = win_kt.shape[3]
    m3 = lambda b, *_: (b, 0, 0)
    grid_spec = pltpu.PrefetchScalarGridSpec(
        num_scalar_prefetch=3, grid=(db,),
        in_specs=[pl.BlockSpec((1, nh, width), m3)] + [pl.BlockSpec((1, 1, width), m3)] * 4
                 + [pl.BlockSpec((1, 1, width, nwin), lambda b, *_: (layer, b, 0, 0))] * 2
                 + [pl.BlockSpec((1, nh, 3), m3), pl.BlockSpec((1, nh, width), m3),
                    pl.BlockSpec(memory_space=pltpu.SMEM),
                    pl.BlockSpec(memory_space=pl.ANY), pl.BlockSpec(memory_space=pl.ANY)],
        out_specs=pl.BlockSpec((1, nh, width), m3),
        scratch_shapes=[pltpu.VMEM((nkv, width, ntop * page), F32), pltpu.VMEM((nkv, width, ntop * page), F32),
                        pltpu.SemaphoreType.DMA((2,))])
    return pl.pallas_call(
        functools.partial(_nsa_sel_win_dec_kernel, layer=layer, nkv=nkv, grp=grp, pos=pos,
                          npast_sel=npast_sel, page=page, nwin=nwin),
        grid_spec=grid_spec,
        out_shape=jax.ShapeDtypeStruct((db, nh, width), F32),
        compiler_params=_params(("arbitrary",)),
        name="nsa_dec_sel_win",
    )(page_table, top_i, top_ok, qblk, ksn, vsn, kwn, vwn, win_kt, win_vt, gates, o_cmp, t5_table,
      ks_pool, vs_pool)


def _pad_last(a, n):
    return jnp.pad(a, [(0, 0)] * (a.ndim - 1) + [(0, n - a.shape[-1])])


def _swap_halves(a):
    h = a.shape[-1] // 2
    return jnp.concatenate([a[..., h:], a[..., :h]], axis=-1)


def _split_cols(a, sizes):
    out, start = [], 0
    for s in sizes:
        out.append(a[..., start:start + s])
        start += s
    return out


def _block_diag(blocks):
    return jax.scipy.linalg.block_diag(*blocks)


def _rope_tables(pos):
    half = MLA_ROPE // 2
    inv = ROPE_THETA ** (-jnp.arange(half, dtype=F32) / half)
    ang = pos.astype(F32)[:, None] * inv[None, :]
    cos, sin = jnp.cos(ang), jnp.sin(ang)
    z = jnp.zeros((pos.shape[0], LANES - MLA_ROPE), F32)
    return jnp.concatenate([cos, cos, z], axis=1), jnp.concatenate([-sin, sin, z], axis=1)


def _overlap_matrix(ncmp, nsel, ncols):
    c0 = np.arange(ncmp)[:, None] * CMP_STRIDE
    s0 = np.arange(ncols)[None, :] * SEL_BLOCK
    ov = np.clip(np.minimum(c0 + CMP_BLOCK, s0 + SEL_BLOCK) - np.maximum(c0, s0), 0, None) / CMP_BLOCK
    ov = np.where(np.arange(ncols)[None, :] < nsel, ov, 0.0)
    return jnp.asarray(ov, dtype=BF)


def kernel(x_prompt, x_sample, cache_mla_ckv, cache_mla_krope, cache_fox_k, cache_fox_v, cache_fox_logf,
           cache_nsa_kcmp, cache_nsa_vcmp, cache_nsa_kslc, cache_nsa_vslc, state_nsa_kwin, state_nsa_vwin,
           page_table, norm_attn, norm_ffn, norm_final, ab_w_in, ab_fox_bf, ab_mla_gq, ab_mla_gkv,
           ab_mla_wuq, ab_mla_wuk, ab_mla_wuv, ab_w_out, c_w_in, c_pe_k, c_w1_k, c_w2_k, c_pe_v, c_w1_v,
           c_w2_v, c_w_out, t5_table, ffn_w_gate, ffn_w_up, ffn_w_down):
    bsz, t, d = x_prompt.shape
    db, t_s, _ = x_sample.shape
    depth = norm_attn.shape[0]
    n_pool, page = cache_mla_ckv.shape[1], cache_mla_ckv.shape[2]
    n_pages = page_table.shape[1]
    past_len = n_pages * page
    q_rank, mla_h, _ = ab_mla_wuq.shape[1:]
    kv_rank = ab_mla_wuk.shape[1]
    mla_v = ab_mla_wuv.shape[3]
    fox_h = ab_fox_bf.shape[1]
    fox_d = cache_fox_k.shape[4]
    nkv, hd = cache_nsa_kcmp.shape[3], cache_nsa_kcmp.shape[4]
    nsa_h = t5_table.shape[1]
    grp = nsa_h // nkv
    nbuf = state_nsa_kwin.shape[2]
    assert t_s == 1 and (q_rank, kv_rank, fox_h * fox_d) == (256, 128, 512)
    assert past_len % CMP_STRIDE == 0 and nbuf <= WINDOW and past_len >= nbuf
    assert t % 256 == 0 and t >= WINDOW and LANES * CMP_STRIDE == t
    mla_scale = (MLA_NOPE + MLA_ROPE) ** -0.5
    nsa_scale = hd ** -0.5
    m_p = bsz * t
    tm = 512

    pos_p = jnp.arange(t)
    pos_s = jnp.tile(past_len + jnp.arange(t_s), db)
    cos_p, sin_p = _rope_tables(pos_p)
    cos_s, sin_s = _rope_tables(pos_s)

    xp = x_prompt.reshape(m_p, d)
    xs = x_sample.reshape(db * t_s, d)
    tiles, bias_c = _t5_km(t5_table, t, t // CMP_STRIDE)
    ov_p = _overlap_matrix(t // CMP_STRIDE, t // SEL_BLOCK, LANES).T
    nsel_s = -(-(past_len + t_s) // SEL_BLOCK)
    ov_s = _overlap_matrix(past_len // CMP_STRIDE, nsel_s, -(-nsel_s // LANES) * LANES)
    tconst = t5_table[T5_BUCKETS - 1]

    def feat_major(a):
        a = jnp.moveaxis(a, 2, -1)
        return a.reshape(a.shape[0], a.shape[1], -1, a.shape[-1])

    krt_pool = feat_major(cache_mla_krope)
    fkt_pool, fvt_pool, lft_pool = feat_major(cache_fox_k), feat_major(cache_fox_v), feat_major(cache_fox_logf)
    kc_pool, vc_pool = feat_major(cache_nsa_kcmp), feat_major(cache_nsa_vcmp)
    ks_pool, vs_pool = feat_major(cache_nsa_kslc), feat_major(cache_nsa_vslc)
    win_kt, win_vt = feat_major(state_nsa_kwin), feat_major(state_nsa_vwin)
    per_page = page // CMP_STRIDE
    perm_np = np.zeros((page, page), np.float32)
    for r_ in range(CMP_STRIDE):
        for n_ in range(per_page):
            perm_np[r_ * per_page + n_, n_ * CMP_STRIDE + r_] = 1.0
    perm = jnp.asarray(perm_np, dtype=BF)
    fox_sel = jnp.eye(fox_h, dtype=F32)[None, :, :, None]
    nsa_sel = jnp.asarray(np.arange(nsa_h)[:, None] // grp == np.arange(nkv)[None, :], dtype=F32)[None, :, :, None]

    ab_p, ab_s, c_p, c_s = [], [], [], []
    for l in range(depth):
        i = l // 2
        g_attn = norm_attn[l][None, :]
        g_ffn = norm_ffn[l][None, :]
        gf = norm_final[None, :] if l == depth - 1 else None
        wg, wu, wd = ffn_w_gate[l].astype(BF), ffn_w_up[l].astype(BF), ffn_w_down[l].astype(BF)
        if l % 2 == 0:
            cq_w, ckv_w, kr_w, fq_w, fk_w, fv_w, fl_w = _split_cols(
                ab_w_in[i], (q_rank, kv_rank, MLA_ROPE, fox_h * fox_d, fox_h * fox_d, fox_h * fox_d, fox_h))
            w_in = jnp.concatenate([cq_w, ckv_w, _pad_last(kr_w, LANES), _pad_last(_swap_halves(kr_w), LANES),
                                    fq_w, fk_w, fv_w, _pad_last(fl_w, LANES)], axis=1).astype(BF)
            wuq = ab_mla_wuq[i]
            rope_w = wuq[:, :, MLA_NOPE:]
            wuq_all = jnp.concatenate([
                wuq[:, :, :MLA_NOPE].reshape(q_rank, mla_h * MLA_NOPE),
                _pad_last(rope_w, LANES).reshape(q_rank, mla_h * LANES),
                _pad_last(_swap_halves(rope_w), LANES).reshape(q_rank, mla_h * LANES)], axis=1).astype(BF)
            wuk_bd = _block_diag([ab_mla_wuk[i][:, h, :].T for h in range(mla_h)]).astype(BF)
            wuv_bd = _block_diag([ab_mla_wuv[i][:, h, :] for h in range(mla_h)]).astype(BF)
            wuv_all = ab_mla_wuv[i].reshape(kv_rank, mla_h * mla_v).astype(BF)
            w_out = ab_w_out[i].astype(BF)
            gq, gkv, bfox = ab_mla_gq[i][None, :], ab_mla_gkv[i][None, :], ab_fox_bf[i][None, :]

            cq, ckv, kr, kcat, fk, fv, lf, fqh, fkh, fvt, ckvt = _ab_proj_prompt(
                xp, g_attn, w_in, gkv, bfox, cos_p, sin_p, bsz, t, tm, fox_d ** -0.5 * LOG2E)
            q_hm = _mla_q_prompt(cq, gq, wuq_all, wuk_bd, cos_p, sin_p, bsz, t, tm, mla_h, mla_scale * LOG2E)
            o_mla = _mla_attn_km(q_hm, kcat.reshape(bsz, t, -1), ckvt, wuv_bd)
            lf3 = lf.reshape(bsz, t, fox_h)
            f2 = _cumsum_lanes(jnp.transpose(lf3, (0, 2, 1))) * LOG2E
            ones = jnp.ones_like(f2)
            pieces = _split3(f2)
            extra = lambda cols: _pad_last(jnp.stack(cols, axis=-1), LANES - fox_d).astype(BF)
            q_aug = jnp.concatenate([fqh, extra([ones, ones, ones, *pieces])], axis=-1)
            k_aug = jnp.concatenate([fkh, extra([-p for p in pieces] + [ones, ones, ones])], axis=-1)
            o_fox = _fox_attn_km(q_aug, k_aug, fvt, fox_d)
            xp = _post(xp, [o_mla.reshape(m_p, -1), o_fox.reshape(m_p, -1)], w_out, g_ffn, wg, wu, wd, gf, tm)
            ab_p.append((ckv.reshape(bsz, t, -1), kr.reshape(bsz, t, -1), fk.reshape(bsz, t, fox_h, fox_d),
                         fv.reshape(bsz, t, fox_h, fox_d), lf3))

            ms = db * t_s
            cq, ckv, kr, kcat, fq, fk, fv, fkb, fvb, lf = _ab_proj(xs, g_attn, w_in, gkv, bfox, cos_s, sin_s, ms)
            qcat = _mla_q(cq, gq, wuq_all, wuk_bd, cos_s, sin_s, ms, mla_h)
            o_mla = _mla_dec2(page_table, qcat.reshape(db, mla_h, 256).astype(F32),
                              kcat.reshape(db, 1, 256).astype(F32), wuv_all, cache_mla_ckv, krt_pool,
                              i, mla_h, mla_scale)
            qblk = (fox_sel * fq.reshape(db, 1, fox_h, fox_d).astype(F32)).reshape(db, fox_h, fox_h * fox_d)
            o_fox = _fox_dec2(page_table, qblk, fk.reshape(db, 1, -1), fv.reshape(db, 1, -1),
                              lf.reshape(db, fox_h, 1), fkt_pool, fvt_pool, lft_pool, i, fox_h, fox_d)
            xs = _post(xs, [o_mla.reshape(ms, -1), o_fox.reshape(ms, -1)], w_out, g_ffn, wg, wu, wd, gf, ms)
            ab_s.append((ckv.reshape(db, t_s, -1), kr.reshape(db, t_s, -1), fk.reshape(db, t_s, fox_h, fox_d),
                         fv.reshape(db, t_s, fox_h, fox_d), lf.reshape(db, t_s, fox_h)))
        else:
            nq, nk = nsa_h * hd, nkv * hd
            parts = _split_cols(c_w_in[i], (nq,) + (nk,) * 6 + (3 * nsa_h,))
            w_in = jnp.concatenate(parts[:7] + [_pad_last(parts[7], LANES)], axis=1).astype(BF)
            w_out = c_w_out[i].astype(BF)
            eye = jnp.eye(nkv, dtype=F32)

            def cmp_weights(pe, w1, w2):
                w1r = w1.reshape(2, CMP_STRIDE, hd, -1)
                w_dec = jnp.concatenate([w1r[0], w1r[1]], axis=-1).astype(BF)
                kron = jax.vmap(lambda m: jnp.kron(eye, m))
                w_bd = jnp.concatenate([kron(w1r[0]), kron(w1r[1])], axis=-1).astype(BF)
                pe_flat = jnp.pad(pe.reshape(1, -1), ((0, 7), (0, 0))).astype(BF)
                return (w_dec, w_bd, pe_flat, w1.astype(BF), jnp.tile(w1, (1, nkv)).astype(BF),
                        w2.astype(BF), jnp.kron(eye, w2).astype(BF))

            kw_ = cmp_weights(c_pe_k[i], c_w1_k[i], c_w2_k[i])
            vw_ = cmp_weights(c_pe_v[i], c_w1_v[i], c_w2_v[i])

            q, kc, vc, ks, vs, kw, vw, ksh, kwh, vst, vwt, gatest = _c_proj_prompt(
                xp, g_attn, w_in, bsz, t, tm, nsa_h, nkv, hd, nsa_scale * LOG2E)
            r3 = lambda a: a.reshape(bsz, t, -1)
            kcmp, vcmpt = _compress_prompt(r3(kc), r3(vc), kw_[1], vw_[1], kw_[2], vw_[2], kw_[4], vw_[4],
                                           kw_[6], vw_[6], nkv)
            o = _nsa_attn_km(q, kcmp, vcmpt, ksh, vst, kwh, vwt, gatest, bias_c, tiles, ov_p, nkv, grp, hd)
            xp = _post(xp, [o.reshape(m_p, -1)], w_out, g_ffn, wg, wu, wd, gf, tm)
            r4 = lambda a: a.reshape(bsz, t, nkv, hd)
            keep = min(WINDOW, t)
            c_p.append((r4(kc), r4(vc), r4(ks), r4(vs), r4(kw)[:, t - keep:], r4(vw)[:, t - keep:]))

            ms = db * t_s
            q, kc, vc, ks, vs, kw, vw, ksb, vsb, kwb, vwb, gates = _c_proj(xs, g_attn, w_in, ms, nq, nk, nsa_scale)
            qblk = (nsa_sel * q.reshape(db, nsa_h, 1, hd).astype(F32)).reshape(db, nsa_h, nkv * hd)
            o_cmp, top_i, top_ok = _nsa_cmp_dec(page_table, qblk, perm, kw_[1], vw_[1], kw_[2], vw_[2], kw_[4],
                                                vw_[4], kw_[6], vw_[6], ov_s, t5_table, kc_pool, vc_pool, i,
                                                nkv, grp, past_len, nsel_s)
            ntop = min(SEL_TOP_N, nsel_s)
            flat_top = lambda a: a[:, :, :ntop].reshape(db, nkv * ntop)
            r1 = lambda a: a.reshape(db, 1, -1)
            o_wide = _nsa_sel_win_dec(page_table, flat_top(top_i), flat_top(top_ok), qblk, r1(ks), r1(vs), r1(kw),
                                      r1(vw), win_kt, win_vt, gates[:, :3 * nsa_h].reshape(db, nsa_h, 3), o_cmp,
                                      t5_table, ks_pool, vs_pool, i, nkv, grp, past_len, past_len // SEL_BLOCK)
            o = jnp.stack([o_wide[:, k * grp:(k + 1) * grp, k * hd:(k + 1) * hd] for k in range(nkv)], axis=1)
            xs = _post(xs, [o.reshape(ms, -1).astype(BF)], w_out, g_ffn, wg, wu, wd, gf, ms)
            r4 = lambda a: a.reshape(db, t_s, nkv, hd)
            c_s.append((r4(kc), r4(vc), r4(ks), r4(vs),
                        jnp.concatenate([state_nsa_kwin[i], r4(kw)], axis=1)[:, t_s:],
                        jnp.concatenate([state_nsa_vwin[i], r4(vw)], axis=1)[:, t_s:]))

    def stack(rows, j):
        return jnp.stack([r[j] for r in rows])

    return (xp.reshape(bsz, t, d), xs.reshape(db, t_s, d),
            stack(ab_p, 0), stack(ab_s, 0), stack(ab_p, 1), stack(ab_s, 1),
            stack(ab_p, 2), stack(ab_s, 2), stack(ab_p, 3), stack(ab_s, 3), stack(ab_p, 4), stack(ab_s, 4),
            stack(c_p, 0), stack(c_s, 0), stack(c_p, 1), stack(c_s, 1), stack(c_p, 2), stack(c_s, 2),
            stack(c_p, 3), stack(c_s, 3), stack(c_p, 4), stack(c_s, 4), stack(c_p, 5), stack(c_s, 5))
```

```python
import functools
import math

import numpy as np
import jax
import jax.numpy as jnp
from jax import lax
from jax.experimental import pallas as pl
from jax.experimental.pallas import tpu as pltpu

F32 = jnp.float32
BF = jnp.bfloat16
I32 = jnp.int32

MLA_NOPE = 64
MLA_ROPE = 32
ROPE_THETA = 10000.0
CMP_STRIDE = 16
CMP_BLOCK = 32
SEL_BLOCK = 64
SEL_TOP_N = 16
WINDOW = 512
T5_BUCKETS = 32
T5_MAX_DIST = 128
EPS = 1e-6
FORCE_SCORE = 1e4
NEG_SCORE = -1e9
NEG_BIG = -1e30
LANES = 128

VMEM_LIMIT = 56 * 1024 * 1024


def _params(sem, vmem=VMEM_LIMIT):
    return pltpu.CompilerParams(dimension_semantics=sem, vmem_limit_bytes=vmem)


def _const_spec(shape):
    nd = len(shape)
    return pl.BlockSpec(shape, lambda *a: (0,) * nd)


def _rms(x, g):
    ms = jnp.mean(x * x, axis=-1, keepdims=True)
    return x * lax.rsqrt(ms + EPS) * g


def _dot(a, b):
    return jnp.dot(a, b, preferred_element_type=F32)


def _dot_nt(a, b):
    return lax.dot_general(a, b, (((1,), (1,)), ((), ())), preferred_element_type=F32)


def _log_sigmoid(x):
    return jnp.minimum(x, 0.0) - jnp.log(1.0 + jnp.exp(-jnp.abs(x)))


def _sigmoid(x):
    return 1.0 / (1.0 + jnp.exp(-x))


def _t5_thresholds():
    exact = T5_BUCKETS // 2
    d = np.arange(0, T5_MAX_DIST + 1)
    lr = np.log(np.maximum(d, 1).astype(np.float64) / exact) / math.log(T5_MAX_DIST / exact)
    large = np.minimum(exact + (lr * (T5_BUCKETS - exact)).astype(np.int64), T5_BUCKETS - 1)
    bucket = np.where(d < exact, d, large)
    return [int(np.argmax(bucket >= j)) for j in range(T5_BUCKETS)]


_T5_THR = _t5_thresholds()


def _t5_bias(dist, values):
    out = jnp.zeros(dist.shape, F32) + values(0)
    for j in range(1, T5_BUCKETS):
        out = jnp.where(dist >= _T5_THR[j], values(j), out)
    return out


def _ab_proj_kernel(x_ref, g_ref, w_ref, gkv_ref, bf_ref, cos_ref, sin_ref,
                    cq_ref, ckv_ref, kr_ref, kcat_ref, fq_ref, fk_ref, fv_ref,
                    fkb_ref, fvb_ref, lf_ref, *, fox_scale, nfh):
    h = _rms(x_ref[...], g_ref[...]).astype(BF)

    def mm(c0, c1):
        return _dot(h, w_ref[:, c0:c1])

    cq_ref[...] = mm(0, 256)
    ckv = _rms(mm(256, 384), gkv_ref[...])
    ckv_ref[...] = ckv
    kr = mm(384, 512) * cos_ref[...] + mm(512, 640) * sin_ref[...]
    kr_ref[...] = kr[:, :MLA_ROPE]
    kcat_ref[:, 0:128] = ckv.astype(BF)
    kcat_ref[:, 128:256] = kr.astype(BF)
    fq_ref[...] = (mm(640, 1152) * fox_scale).astype(BF)
    fk = mm(1152, 1664)
    fk_ref[...] = fk
    fkb_ref[...] = fk.astype(BF)
    fv = mm(1664, 2176)
    fv_ref[...] = fv
    fvb_ref[...] = fv.astype(BF)
    fl = mm(2176, 2304)[:, :nfh] + bf_ref[...]
    lf_ref[...] = _log_sigmoid(fl)


def _ab_proj_prompt_kernel(x_ref, g_ref, w_ref, gkv_ref, bf_ref, cos_ref, sin_ref,
                           cq_ref, ckv_ref, kr_ref, kcat_ref, fk_ref, fv_ref, lf_ref,
                           fqh_ref, fkh_ref, fvt_ref, ckvt_ref, *, q_scale, nfh, hd):
    h = _rms(x_ref[...], g_ref[...]).astype(BF)

    def mm(c0, c1):
        return _dot(h, w_ref[:, c0:c1])

    cq_ref[...] = mm(0, 256)
    ckv = _rms(mm(256, 384), gkv_ref[...])
    ckv_ref[...] = ckv
    kr = mm(384, 512) * cos_ref[...] + mm(512, 640) * sin_ref[...]
    kr_ref[...] = kr[:, :MLA_ROPE]
    kcat_ref[:, 0:128] = ckv.astype(BF)
    kcat_ref[:, 128:256] = kr.astype(BF)
    fq = mm(640, 1152) * q_scale
    fk = mm(1152, 1664)
    fk_ref[0] = fk.T
    fvt = mm(1664, 2176).T
    fv_ref[0] = fvt
    for i in range(nfh):
        fqh_ref[0, i] = fq[:, i * hd:(i + 1) * hd].astype(BF)
        fkh_ref[0, i] = fk[:, i * hd:(i + 1) * hd].astype(BF)
    fvt_ref[0] = fvt.astype(BF)
    ckvt_ref[0] = ckv.T.astype(BF)
    fl = mm(2176, 2304)[:, :nfh] + bf_ref[...]
    lf_ref[...] = _log_sigmoid(fl)


def _ab_proj_prompt(x, g, w, gkv, bf, cos_t, sin_t, bsz, t, tm, q_scale):
    m, d = x.shape
    nt = t // tm
    nfh = bf.shape[1]
    fd = 512
    hd = fd // nfh
    row = lambda wd: pl.BlockSpec((tm, wd), lambda i: (i, 0))
    tab = pl.BlockSpec((tm, LANES), lambda i: (i % nt, 0))
    hm = pl.BlockSpec((1, nfh, tm, hd), lambda i: (i // nt, 0, i % nt, 0))
    fm = lambda n: pl.BlockSpec((1, n, tm), lambda i: (i // nt, 0, i % nt))
    out_specs = [row(256), row(128), row(MLA_ROPE), row(256), fm(fd), fm(fd), row(nfh), hm, hm, fm(fd), fm(128)]
    out_shape = ([jax.ShapeDtypeStruct((m, 256), F32), jax.ShapeDtypeStruct((m, 128), F32),
                  jax.ShapeDtypeStruct((m, MLA_ROPE), F32), jax.ShapeDtypeStruct((m, 256), BF),
                  jax.ShapeDtypeStruct((bsz, fd, t), F32), jax.ShapeDtypeStruct((bsz, fd, t), F32),
                  jax.ShapeDtypeStruct((m, nfh), F32),
                  jax.ShapeDtypeStruct((bsz, nfh, t, hd), BF), jax.ShapeDtypeStruct((bsz, nfh, t, hd), BF),
                  jax.ShapeDtypeStruct((bsz, fd, t), BF), jax.ShapeDtypeStruct((bsz, 128, t), BF)])
    return pl.pallas_call(
        functools.partial(_ab_proj_prompt_kernel, q_scale=q_scale, nfh=nfh, hd=hd),
        grid=(m // tm,),
        in_specs=[row(d), _const_spec(g.shape), _const_spec(w.shape), _const_spec(gkv.shape),
                  _const_spec(bf.shape), tab, tab],
        out_specs=out_specs, out_shape=out_shape,
        compiler_params=_params(("parallel",)),
        name="ab_proj_prompt",
    )(x, g, w, gkv, bf, cos_t, sin_t)


def _fox_km_kernel(q_ref, k_ref, vt_ref, mask_ref, o_ref, *, nh, hd, tq, tk):
    qi = pl.program_id(1)
    qs = [q_ref[0, h] for h in range(nh)]

    def tile(h, ki, carry, bias):
        m_p, l_p, acc = carry
        start = pl.multiple_of(ki * tk, tk)
        st = _dot_nt(k_ref[0, h, pl.ds(start, tk), :], qs[h])
        if bias is not None:
            st = st + bias
        m_n = jnp.maximum(m_p, jnp.max(st, axis=0, keepdims=True))
        p = jnp.exp2(st - m_n)
        alpha = jnp.exp2(m_p - m_n)
        l_n = alpha * l_p + jnp.sum(p, axis=0, keepdims=True)
        vt = vt_ref[0, h * hd:(h + 1) * hd, pl.ds(start, tk)]
        return m_n, l_n, alpha * acc + _dot(vt, p.astype(BF))

    init = (jnp.full((1, tq), NEG_BIG, F32), jnp.zeros((1, tq), F32), jnp.zeros((hd, tq), F32))
    cs = lax.fori_loop(0, qi, lambda ki, c: tuple(tile(h, ki, c[h], None) for h in range(nh)), (init,) * nh)
    for h in range(nh):
        _, l_f, acc = tile(h, qi, cs[h], mask_ref[...])
        o_ref[0, :, h * hd:(h + 1) * hd] = (acc / l_f).T.astype(BF)


def _fox_attn_km(q_aug, k_aug, vt, hd, tq=256):
    b, nh, t, w = q_aug.shape
    mask = jnp.asarray(np.where(np.arange(tq)[:, None] <= np.arange(tq)[None, :], 0.0, NEG_BIG), dtype=F32)
    full_b = lambda a: pl.BlockSpec((1,) + a.shape[1:], lambda bi, qi: (bi,) + (0,) * (a.ndim - 1))
    return pl.pallas_call(
        functools.partial(_fox_km_kernel, nh=nh, hd=hd, tq=tq, tk=tq),
        grid=(b, t // tq),
        in_specs=[pl.BlockSpec((1, nh, tq, w), lambda bi, qi: (bi, 0, qi, 0)), full_b(k_aug), full_b(vt),
                  _const_spec(mask.shape)],
        out_specs=pl.BlockSpec((1, tq, nh * hd), lambda bi, qi: (bi, qi, 0)),
        out_shape=jax.ShapeDtypeStruct((b, t, nh * hd), BF),
        compiler_params=_params(("parallel", "arbitrary")),
        name="fox_attn",
    )(q_aug, k_aug, vt, mask)


def _mla_q_prompt_kernel(cq_ref, gq_ref, wuq_ref, wuk_ref, cos_ref, sin_ref, q_ref, *, nh, q_scale):
    cqn = _rms(cq_ref[...], gq_ref[...]).astype(BF)
    qn = _dot(cqn, wuq_ref[:, 0:nh * MLA_NOPE]).astype(BF)
    qlat = _dot(qn, wuk_ref[...]) * q_scale
    r0 = nh * MLA_NOPE
    r1 = r0 + nh * LANES
    cos = cos_ref[...] * q_scale
    sin = sin_ref[...] * q_scale
    for h in range(nh):
        qr = (_dot(cqn, wuq_ref[:, r0 + h * LANES:r0 + (h + 1) * LANES]) * cos
              + _dot(cqn, wuq_ref[:, r1 + h * LANES:r1 + (h + 1) * LANES]) * sin)
        q_ref[0, h, :, 0:128] = qlat[:, 128 * h:128 * (h + 1)].astype(BF)
        q_ref[0, h, :, 128:256] = qr.astype(BF)


def _mla_q_prompt(cq, gq, wuq, wukbd, cos_t, sin_t, bsz, t, tm, nh, q_scale):
    nt = t // tm
    tab = pl.BlockSpec((tm, LANES), lambda i: (i % nt, 0))
    return pl.pallas_call(
        functools.partial(_mla_q_prompt_kernel, nh=nh, q_scale=q_scale),
        grid=(cq.shape[0] // tm,),
        in_specs=[pl.BlockSpec((tm, cq.shape[1]), lambda i: (i, 0)), _const_spec(gq.shape),
                  _const_spec(wuq.shape), _const_spec(wukbd.shape), tab, tab],
        out_specs=pl.BlockSpec((1, nh, tm, 256), lambda i: (i // nt, 0, i % nt, 0)),
        out_shape=jax.ShapeDtypeStruct((bsz, nh, t, 256), BF),
        compiler_params=_params(("parallel",)),
        name="mla_q_prompt",
    )(cq, gq, wuq, wukbd, cos_t, sin_t)


def _mla_km_kernel(q_ref, k_ref, vt_ref, mask_ref, wuv_ref, o_ref, *, nh, tq, tk, nchain):
    qi = pl.program_id(1)
    last = qi // (tk // tq)
    odd = qi % (tk // tq)
    hpc = nh // nchain
    qs = [q_ref[0, c * hpc:(c + 1) * hpc].reshape(hpc * tq, q_ref.shape[3]) for c in range(nchain)]
    rank = vt_ref.shape[1]

    def tile(c, ki, carry, bias):
        m_p, l_p, acc = carry
        start = pl.multiple_of(ki * tk, tk)
        st = _dot_nt(k_ref[0, pl.ds(start, tk), :], qs[c])
        if bias is not None:
            st = st + bias
        m_n = jnp.maximum(m_p, jnp.max(st, axis=0, keepdims=True))
        p = jnp.exp2(st - m_n)
        alpha = jnp.exp2(m_p - m_n)
        l_n = alpha * l_p + jnp.sum(p, axis=0, keepdims=True)
        return m_n, l_n, alpha * acc + _dot(vt_ref[0, :, pl.ds(start, tk)], p.astype(BF))

    w = hpc * tq
    init = (jnp.full((1, w), NEG_BIG, F32), jnp.zeros((1, w), F32), jnp.zeros((rank, w), F32))
    cs = lax.fori_loop(0, last, lambda ki, c: tuple(tile(j, ki, c[j], None) for j in range(nchain)),
                       (init,) * nchain)
    mask = jnp.concatenate([mask_ref[odd]] * hpc, axis=1)
    olat = []
    for c in range(nchain):
        _, l_f, acc = tile(c, last, cs[c], mask)
        o_t = acc / l_f
        for j in range(hpc):
            olat.append(o_t[:, j * tq:(j + 1) * tq].T.astype(BF))
    o_ref[0] = _dot(jnp.concatenate(olat, axis=1), wuv_ref[...]).astype(BF)


def _mla_attn_km(q_hm, kcat, ckvt, wuvbd, tq=128, tk=256, nchain=4):
    b, nh, t, _ = q_hm.shape
    per = tk // tq
    d = np.arange(tk)[None, :, None] <= (np.arange(per)[:, None, None] * tq + np.arange(tq)[None, None, :])
    mask = jnp.asarray(np.where(d, 0.0, NEG_BIG), dtype=F32)
    full_b = lambda a: pl.BlockSpec((1,) + a.shape[1:], lambda bi, qi: (bi,) + (0,) * (a.ndim - 1))
    return pl.pallas_call(
        functools.partial(_mla_km_kernel, nh=nh, tq=tq, tk=tk, nchain=nchain),
        grid=(b, t // tq),
        in_specs=[pl.BlockSpec((1, nh, tq, q_hm.shape[3]), lambda bi, qi: (bi, 0, qi, 0)),
                  full_b(kcat), full_b(ckvt), _const_spec(mask.shape), _const_spec(wuvbd.shape)],
        out_specs=pl.BlockSpec((1, tq, wuvbd.shape[1]), lambda bi, qi: (bi, qi, 0)),
        out_shape=jax.ShapeDtypeStruct((b, t, wuvbd.shape[1]), BF),
        compiler_params=_params(("parallel", "arbitrary")),
        name="mla_attn",
    )(q_hm, kcat, ckvt, mask, wuvbd)


def _fox_bias_cols_kernel(lf_ref, qx_ref, kx_ref, *, nh, width):
    x = lf_ref[0]
    n = x.shape[1]
    lane = lax.broadcasted_iota(I32, x.shape, 1)
    k = 1
    while k < n:
        x = x + jnp.where(lane >= k, pltpu.roll(x, k, axis=1), 0.0)
        k *= 2
    f2 = x * LOG2E
    hi = _bf_round(f2)
    mid = _bf_round(f2 - hi)
    lo = _bf_round(f2 - hi - mid)
    ones = jnp.ones((1, n), F32)
    zeros = jnp.zeros((2, n), F32)
    pad = jnp.zeros((n, width - 8), BF)
    for h in range(nh):
        pieces = [a[h:h + 1, :] for a in (hi, mid, lo)]
        qx_ref[0, h, :, 0:8] = jnp.concatenate([ones, ones, ones] + pieces + [zeros], axis=0).T.astype(BF)
        kx_ref[0, h, :, 0:8] = jnp.concatenate([-p for p in pieces] + [ones, ones, ones, zeros], axis=0).T.astype(BF)
        qx_ref[0, h, :, 8:width] = pad
        kx_ref[0, h, :, 8:width] = pad


def _fox_bias_cols(lf_t, width):
    b, nh, t = lf_t.shape
    blk = pl.BlockSpec((1, nh, t, width), lambda i: (i, 0, 0, 0))
    return pl.pallas_call(
        functools.partial(_fox_bias_cols_kernel, nh=nh, width=width), grid=(b,),
        in_specs=[pl.BlockSpec((1, nh, t), lambda i: (i, 0, 0))],
        out_specs=[blk, blk],
        out_shape=[jax.ShapeDtypeStruct((b, nh, t, width), BF)] * 2,
        compiler_params=_params(("parallel",)),
        name="fox_bias_cols",
    )(lf_t)


def _ab_proj(x, g, w, gkv, bf, cos_t, sin_t, tm):
    m, d = x.shape
    nt = cos_t.shape[0] // tm
    nfh = bf.shape[1]
    fd = 512
    row = lambda wd: pl.BlockSpec((tm, wd), lambda i: (i, 0))
    tab = pl.BlockSpec((tm, LANES), lambda i: (i % nt, 0))
    outs = [((m, 256), F32), ((m, 128), F32), ((m, MLA_ROPE), F32), ((m, 256), BF),
            ((m, fd), BF), ((m, fd), F32), ((m, fd), F32), ((m, fd), BF), ((m, fd), BF),
            ((m, nfh), F32)]
    return pl.pallas_call(
        functools.partial(_ab_proj_kernel, fox_scale=(fd // nfh) ** -0.5, nfh=nfh),
        grid=(m // tm,),
        in_specs=[row(d), _const_spec(g.shape), _const_spec(w.shape), _const_spec(gkv.shape),
                  _const_spec(bf.shape), tab, tab],
        out_specs=[row(s[1]) for s, _ in outs],
        out_shape=[jax.ShapeDtypeStruct(s, dt) for s, dt in outs],
        compiler_params=_params(("parallel",)),
        name="ab_proj",
    )(x, g, w, gkv, bf, cos_t, sin_t)


def _mla_q_kernel(cq_ref, gq_ref, wuq_ref, wuk_ref, cos_ref, sin_ref, q_ref, *, nh):
    cqn = _rms(cq_ref[...], gq_ref[...]).astype(BF)
    qn = _dot(cqn, wuq_ref[:, 0:nh * MLA_NOPE]).astype(BF)
    qlat = _dot(qn, wuk_ref[...])
    r0 = nh * MLA_NOPE
    r1 = r0 + nh * LANES
    cos = cos_ref[...]
    sin = sin_ref[...]
    for h in range(nh):
        qr = (_dot(cqn, wuq_ref[:, r0 + h * LANES:r0 + (h + 1) * LANES]) * cos
              + _dot(cqn, wuq_ref[:, r1 + h * LANES:r1 + (h + 1) * LANES]) * sin)
        q_ref[:, 256 * h:256 * h + 128] = qlat[:, 128 * h:128 * (h + 1)].astype(BF)
        q_ref[:, 256 * h + 128:256 * (h + 1)] = qr.astype(BF)


def _mla_q(cq, gq, wuq, wukbd, cos_t, sin_t, tm, nh):
    m = cq.shape[0]
    nt = cos_t.shape[0] // tm
    tab = pl.BlockSpec((tm, LANES), lambda i: (i % nt, 0))
    return pl.pallas_call(
        functools.partial(_mla_q_kernel, nh=nh),
        grid=(m // tm,),
        in_specs=[pl.BlockSpec((tm, cq.shape[1]), lambda i: (i, 0)), _const_spec(gq.shape),
                  _const_spec(wuq.shape), _const_spec(wukbd.shape), tab, tab],
        out_specs=pl.BlockSpec((tm, 256 * nh), lambda i: (i, 0)),
        out_shape=jax.ShapeDtypeStruct((m, 256 * nh), BF),
        compiler_params=_params(("parallel",)),
        name="mla_q",
    )(cq, gq, wuq, wukbd, cos_t, sin_t)


def _softmax_step(s, m_prev, l_prev):
    m_new = jnp.maximum(m_prev, jnp.max(s, axis=-1, keepdims=True))
    alpha = jnp.exp(m_prev - m_new)
    p = jnp.exp(s - m_new)
    l_new = alpha * l_prev + jnp.sum(p, axis=-1, keepdims=True)
    return p, alpha, m_new, l_new


def _mla_attn_kernel(q_ref, k_ref, wuv_ref, o_ref, m_ref, l_ref, acc_ref, *, tq, tk, nh, scale):
    qi = pl.program_id(1)
    ki = pl.program_id(2)
    nk = pl.num_programs(2)

    @pl.when(ki == 0)
    def _():
        m_ref[...] = jnp.full(m_ref.shape, NEG_BIG, F32)
        l_ref[...] = jnp.zeros(l_ref.shape, F32)
        acc_ref[...] = jnp.zeros(acc_ref.shape, F32)

    @pl.when(ki * tk <= qi * tq + tq - 1)
    def _():
        k = k_ref[0]
        v = k[:, 0:128]
        rows = qi * tq + lax.broadcasted_iota(I32, (tq, tk), 0)
        cols = ki * tk + lax.broadcasted_iota(I32, (tq, tk), 1)
        mask = cols <= rows
        for h in range(nh):
            q = q_ref[0, :, 256 * h:256 * (h + 1)]
            s = jnp.where(mask, _dot_nt(q, k) * scale, NEG_BIG)
            p, alpha, m_new, l_new = _softmax_step(s, m_ref[h], l_ref[h])
            acc_ref[h] = alpha * acc_ref[h] + _dot(p.astype(BF), v)
            m_ref[h] = m_new
            l_ref[h] = l_new

    @pl.when(ki == nk - 1)
    def _():
        olat = jnp.concatenate([(acc_ref[h] / l_ref[h]).astype(BF) for h in range(nh)], axis=1)
        o_ref[0] = _dot(olat, wuv_ref[...]).astype(BF)


def _mla_attn(qcat, kcat, wuvbd, nh, scale, tq=256, tk=256):
    b, t, _ = qcat.shape
    nq, nk = t // tq, t // tk
    kmap = lambda bi, qi, ki: (bi, jnp.minimum(ki, (qi * tq + tq - 1) // tk), 0)
    return pl.pallas_call(
        functools.partial(_mla_attn_kernel, tq=tq, tk=tk, nh=nh, scale=scale),
        grid=(b, nq, nk),
        in_specs=[pl.BlockSpec((1, tq, 256 * nh), lambda bi, qi, ki: (bi, qi, 0)),
                  pl.BlockSpec((1, tk, 256), kmap), _const_spec(wuvbd.shape)],
        out_specs=pl.BlockSpec((1, tq, wuvbd.shape[1]), lambda bi, qi, ki: (bi, qi, 0)),
        out_shape=jax.ShapeDtypeStruct((b, t, wuvbd.shape[1]), BF),
        scratch_shapes=[pltpu.VMEM((nh, tq, 1), F32), pltpu.VMEM((nh, tq, 1), F32),
                        pltpu.VMEM((nh, tq, 128), F32)],
        compiler_params=_params(("parallel", "parallel", "arbitrary")),
        name="mla_attn",
    )(qcat, kcat, wuvbd)


def _cumsum_kernel(x_ref, o_ref):
    x = x_ref[0]
    n = x.shape[1]
    lane = lax.broadcasted_iota(I32, x.shape, 1)
    k = 1
    while k < n:
        x = x + jnp.where(lane >= k, pltpu.roll(x, k, axis=1), 0.0)
        k *= 2
    o_ref[0] = x


def _cumsum_lanes(x):
    b, h, t = x.shape
    return pl.pallas_call(
        _cumsum_kernel, grid=(b,),
        in_specs=[pl.BlockSpec((1, h, t), lambda i: (i, 0, 0))],
        out_specs=pl.BlockSpec((1, h, t), lambda i: (i, 0, 0)),
        out_shape=jax.ShapeDtypeStruct((b, h, t), F32),
        compiler_params=_params(("parallel",)),
        name="fox_cumsum",
    )(x)


def _fox_attn_kernel(q_ref, k_ref, v_ref, f_ref, ft_ref, o_ref, m_ref, l_ref, acc_ref, *, tq, tk, nh):
    qi = pl.program_id(1)
    ki = pl.program_id(2)
    nk = pl.num_programs(2)

    @pl.when(ki == 0)
    def _():
        m_ref[...] = jnp.full(m_ref.shape, NEG_BIG, F32)
        l_ref[...] = jnp.zeros(l_ref.shape, F32)
        acc_ref[...] = jnp.zeros(acc_ref.shape, F32)

    @pl.when(ki * tk <= qi * tq + tq - 1)
    def _():
        rows = qi * tq + lax.broadcasted_iota(I32, (tq, tk), 0)
        cols = ki * tk + lax.broadcasted_iota(I32, (tq, tk), 1)
        mask = cols <= rows
        lo = lax.broadcasted_iota(I32, (tq, LANES), 1) < 64
        fq = f_ref[0]
        fk = ft_ref[0]
        for hp in range(nh // 2):
            q2 = q_ref[0, :, 128 * hp:128 * (hp + 1)]
            k2 = k_ref[0, :, 128 * hp:128 * (hp + 1)]
            v2 = v_ref[0, :, 128 * hp:128 * (hp + 1)]
            zero = jnp.zeros_like(q2)
            pvs, alphas = [], []
            for e in range(2):
                h = 2 * hp + e
                qm = jnp.where(lo, q2, zero) if e == 0 else jnp.where(lo, zero, q2)
                s = _dot_nt(qm, k2) + (fq[:, h:h + 1] - fk[h:h + 1, :])
                s = jnp.where(mask, s, NEG_BIG)
                p, alpha, m_new, l_new = _softmax_step(s, m_ref[h], l_ref[h])
                m_ref[h] = m_new
                l_ref[h] = l_new
                pvs.append(_dot(p.astype(BF), v2))
                alphas.append(alpha)
            a2 = jnp.where(lo, alphas[0], alphas[1])
            acc_ref[:, 128 * hp:128 * (hp + 1)] = (a2 * acc_ref[:, 128 * hp:128 * (hp + 1)]
                                                   + jnp.where(lo, pvs[0], pvs[1]))

    @pl.when(ki == nk - 1)
    def _():
        lo = lax.broadcasted_iota(I32, (tq, LANES), 1) < 64
        for hp in range(nh // 2):
            l2 = jnp.where(lo, l_ref[2 * hp], l_ref[2 * hp + 1])
            o_ref[0, :, 128 * hp:128 * (hp + 1)] = (acc_ref[:, 128 * hp:128 * (hp + 1)] / l2).astype(BF)


def _fox_attn(fq, fk, fv, f, ft, nh, tq=256, tk=256):
    b, t, w = fq.shape
    nq, nk = t // tq, t // tk
    kclamp = lambda qi, ki: jnp.minimum(ki, (qi * tq + tq - 1) // tk)
    return pl.pallas_call(
        functools.partial(_fox_attn_kernel, tq=tq, tk=tk, nh=nh),
        grid=(b, nq, nk),
        in_specs=[pl.BlockSpec((1, tq, w), lambda bi, qi, ki: (bi, qi, 0)),
                  pl.BlockSpec((1, tk, w), lambda bi, qi, ki: (bi, kclamp(qi, ki), 0)),
                  pl.BlockSpec((1, tk, w), lambda bi, qi, ki: (bi, kclamp(qi, ki), 0)),
                  pl.BlockSpec((1, tq, nh), lambda bi, qi, ki: (bi, qi, 0)),
                  pl.BlockSpec((1, nh, tk), lambda bi, qi, ki: (bi, 0, kclamp(qi, ki)))],
        out_specs=pl.BlockSpec((1, tq, w), lambda bi, qi, ki: (bi, qi, 0)),
        out_shape=jax.ShapeDtypeStruct((b, t, w), BF),
        scratch_shapes=[pltpu.VMEM((nh, tq, 1), F32), pltpu.VMEM((nh, tq, 1), F32),
                        pltpu.VMEM((tq, w), F32)],
        compiler_params=_params(("parallel", "parallel", "arbitrary")),
        name="fox_attn",
    )(fq, fk, fv, f, ft)


def _post_kernel(*refs, n_o, tf, final):
    x_ref = refs[0]
    o_refs = refs[1:1 + n_o]
    wo_refs = refs[1 + n_o:2 + n_o]
    g_ref, wg_ref, wu_ref, wd_ref = refs[2 + n_o:6 + n_o]
    nxt = 6 + n_o
    if final:
        gf_ref = refs[nxt]
        nxt += 1
    out_ref = refs[nxt]
    o = o_refs[0][...] if n_o == 1 else jnp.concatenate([r[...] for r in o_refs], axis=1)
    x1 = x_ref[...] + _dot(o, wo_refs[0][...])
    h = _rms(x1, g_ref[...]).astype(BF)
    acc = x1
    nf = wg_ref.shape[1] // tf
    for f in range(nf):
        a = _dot(h, wg_ref[:, f * tf:(f + 1) * tf])
        u = _dot(h, wu_ref[:, f * tf:(f + 1) * tf])
        t = (a * _sigmoid(a) * u).astype(BF)
        acc = acc + _dot(t, wd_ref[f * tf:(f + 1) * tf, :])
    if final:
        out_ref[...] = _rms(acc, gf_ref[...])
    else:
        out_ref[...] = acc


def _post(x, os_, wo, g, wg, wu, wd, gf, tm):
    m, d = x.shape
    n_o = len(os_)
    final = gf is not None
    row = lambda wd_: pl.BlockSpec((tm, wd_), lambda i: (i, 0))
    single = lambda a: pl.BlockSpec(a.shape, lambda i: (0,) * a.ndim, pipeline_mode=pl.Buffered(1))
    args = [x] + list(os_) + [wo, g, wg, wu, wd] + ([gf] if final else [])
    in_specs = ([row(d)] + [row(o.shape[1]) for o in os_] + [single(wo)]
                + [_const_spec(g.shape), single(wg), single(wu), single(wd)]
                + ([_const_spec(gf.shape)] if final else []))
    return pl.pallas_call(
        functools.partial(_post_kernel, n_o=n_o, tf=256, final=final),
        grid=(m // tm,),
        in_specs=in_specs,
        out_specs=row(d),
        out_shape=jax.ShapeDtypeStruct((m, d), F32),
        compiler_params=_params(("parallel",)),
        name="post_ffn",
    )(*args)


def _c_proj_kernel(x_ref, g_ref, w_ref, q_ref, kc_ref, vc_ref, ks_ref, vs_ref, kw_ref, vw_ref,
                   ksb_ref, vsb_ref, kwb_ref, vwb_ref, gt_ref, *, scale, nq, nkv):
    h = _rms(x_ref[...], g_ref[...]).astype(BF)

    def mm(c0, c1):
        return _dot(h, w_ref[:, c0:c1])

    q_ref[...] = (mm(0, nq) * scale).astype(BF)
    c = nq
    kc_ref[...] = mm(c, c + nkv)
    vc_ref[...] = mm(c + nkv, c + 2 * nkv)
    ks = mm(c + 2 * nkv, c + 3 * nkv)
    ks_ref[...] = ks
    ksb_ref[...] = ks.astype(BF)
    vs = mm(c + 3 * nkv, c + 4 * nkv)
    vs_ref[...] = vs
    vsb_ref[...] = vs.astype(BF)
    kw = mm(c + 4 * nkv, c + 5 * nkv)
    kw_ref[...] = kw
    kwb_ref[...] = kw.astype(BF)
    vw = mm(c + 5 * nkv, c + 6 * nkv)
    vw_ref[...] = vw
    vwb_ref[...] = vw.astype(BF)
    gt_ref[...] = _sigmoid(mm(c + 6 * nkv, c + 6 * nkv + LANES))


def _c_proj(x, g, w, tm, nq, nkv, scale):
    m, d = x.shape
    row = lambda wd: pl.BlockSpec((tm, wd), lambda i: (i, 0))
    outs = ([((m, nq), BF)] + [((m, nkv), F32)] * 6 + [((m, nkv), BF)] * 4 + [((m, LANES), F32)])
    return pl.pallas_call(
        functools.partial(_c_proj_kernel, scale=scale, nq=nq, nkv=nkv),
        grid=(m // tm,),
        in_specs=[row(d), _const_spec(g.shape), _const_spec(w.shape)],
        out_specs=[row(s[1]) for s, _ in outs],
        out_shape=[jax.ShapeDtypeStruct(s, dt) for s, dt in outs],
        compiler_params=_params(("parallel",)),
        name="c_proj",
    )(x, g, w)


def _c_proj_prompt_kernel(x_ref, g_ref, w_ref, q_ref, kc_ref, vc_ref, kct_ref, vct_ref, kst_ref, vst_ref,
                          kwt_ref, vwt_ref, ksh_ref, kwh_ref, vsb_ref, vwb_ref, gtt_ref,
                          *, scale, nh, nkv, hd):
    h = _rms(x_ref[...], g_ref[...]).astype(BF)
    nq, nk = nh * hd, nkv * hd

    def mm(c0, c1):
        return _dot(h, w_ref[:, c0:c1])

    q = mm(0, nq) * scale
    for i in range(nh):
        q_ref[0, i] = q[:, i * hd:(i + 1) * hd].astype(BF)
    c = nq
    for j, (r_ref, t_ref, h_ref, b_ref) in enumerate((
            (kc_ref, kct_ref, None, None), (vc_ref, vct_ref, None, None),
            (None, kst_ref, ksh_ref, None), (None, vst_ref, None, vsb_ref),
            (None, kwt_ref, kwh_ref, None), (None, vwt_ref, None, vwb_ref))):
        a = mm(c + j * nk, c + (j + 1) * nk)
        at = a.T
        t_ref[0] = at
        if r_ref is not None:
            r_ref[...] = a
        if h_ref is not None:
            for i in range(nkv):
                h_ref[0, i] = a[:, i * hd:(i + 1) * hd].astype(BF)
        if b_ref is not None:
            b_ref[0] = at.astype(BF)
    gtt_ref[0] = _sigmoid(mm(c + 6 * nk, c + 6 * nk + LANES)).T


def _c_proj_prompt(x, g, w, bsz, t, tm, nh, nkv, hd, scale):
    m, d = x.shape
    nq, nk = nh * hd, nkv * hd
    nt = t // tm
    row = lambda wd: pl.BlockSpec((tm, wd), lambda i: (i, 0))
    hm = lambda n: pl.BlockSpec((1, n, tm, hd), lambda i: (i // nt, 0, i % nt, 0))
    fm = lambda n: pl.BlockSpec((1, n, tm), lambda i: (i // nt, 0, i % nt))
    out_specs = [hm(nh)] + [row(nk)] * 2 + [fm(nk)] * 6 + [hm(nkv), hm(nkv), fm(nk), fm(nk), fm(LANES)]
    out_shape = ([jax.ShapeDtypeStruct((bsz, nh, t, hd), BF)] + [jax.ShapeDtypeStruct((m, nk), F32)] * 2
                 + [jax.ShapeDtypeStruct((bsz, nk, t), F32)] * 6
                 + [jax.ShapeDtypeStruct((bsz, nkv, t, hd), BF)] * 2
                 + [jax.ShapeDtypeStruct((bsz, nk, t), BF)] * 2 + [jax.ShapeDtypeStruct((bsz, LANES, t), F32)])
    return pl.pallas_call(
        functools.partial(_c_proj_prompt_kernel, scale=scale, nh=nh, nkv=nkv, hd=hd),
        grid=(m // tm,),
        in_specs=[row(d), _const_spec(g.shape), _const_spec(w.shape)],
        out_specs=out_specs, out_shape=out_shape,
        compiler_params=_params(("parallel",)),
        name="c_proj_prompt",
    )(x, g, w)


def _gelu(x):
    return 0.5 * x * (1.0 + jnp.tanh(math.sqrt(2.0 / math.pi) * (x + 0.044715 * (x * x * x))))


def _compress_tail(y, pe_ref, w1_ref, w2_ref, width):
    nchunk = y.shape[0]
    pe_term = _dot(pe_ref[...], w1_ref[...])[0:1, :]
    second = pltpu.roll(y[:, width:2 * width], nchunk - 1, axis=0)
    hid = _gelu(y[:, 0:width] + second + pe_term).astype(BF)
    return _dot(hid, w2_ref[...])


def _compress_prompt_kernel(kc0_ref, kc1_ref, vc0_ref, vc1_ref, wk_ref, wv_ref, pek_ref, pev_ref,
                            w1k_ref, w1v_ref, w2k_ref, w2v_ref, ko_ref, vo_ref, *, nchunk, width, nkv):
    outs = []
    for x_refs, w_ref, pe_ref, w1_ref, w2_ref in (
            ((kc0_ref, kc1_ref), wk_ref, pek_ref, w1k_ref, w2k_ref),
            ((vc0_ref, vc1_ref), wv_ref, pev_ref, w1v_ref, w2v_ref)):
        y = jnp.zeros((nchunk, 2 * width), F32)
        for r in range(CMP_STRIDE):
            for half, x_ref in enumerate(x_refs):
                xr = x_ref[0, pl.ds(r, nchunk, stride=CMP_STRIDE), :].astype(BF)
                y = y + _dot(xr, w_ref[r, half * LANES:(half + 1) * LANES, :])
        outs.append(_compress_tail(y, pe_ref, w1_ref, w2_ref, width))
    hd = width // nkv
    for i in range(nkv):
        ko_ref[0, i] = outs[0][:, i * hd:(i + 1) * hd].astype(BF)
    vo_ref[0] = outs[1].T.astype(BF)


def _compress_prompt(kc, vc, wk, wv, pek, pev, w1k, w1v, w2k, w2v, nkv):
    b, t, width = kc.shape
    assert width == 2 * LANES
    nchunk = t // CMP_STRIDE
    half = lambda j: pl.BlockSpec((1, t, LANES), lambda i: (i, 0, j))
    consts = [wk, wv, pek, pev, w1k, w1v, w2k, w2v]
    return pl.pallas_call(
        functools.partial(_compress_prompt_kernel, nchunk=nchunk, width=width, nkv=nkv),
        grid=(b,),
        in_specs=[half(0), half(1), half(0), half(1)] + [_const_spec(c.shape) for c in consts],
        out_specs=[pl.BlockSpec((1, nkv, nchunk, width // nkv), lambda i: (i, 0, 0, 0)),
                   pl.BlockSpec((1, width, nchunk), lambda i: (i, 0, 0))],
        out_shape=[jax.ShapeDtypeStruct((b, nkv, nchunk, width // nkv), BF),
                   jax.ShapeDtypeStruct((b, width, nchunk), BF)],
        compiler_params=_params(("parallel",)),
        name="nsa_compress_prompt",
    )(kc, kc, vc, vc, *consts)


def _t5_tiles_kernel(tbl_ref, tiles_ref, bc_ref, *, t, ncmp):
    h = pl.program_id(0)
    val = lambda j: tbl_ref[j, h]
    r = lax.broadcasted_iota(I32, (LANES, LANES), 0)
    c = lax.broadcasted_iota(I32, (LANES, LANES), 1)
    tiles_ref[0, 0] = _t5_bias(r - c, val)
    tiles_ref[0, 1] = _t5_bias(LANES + r - c, val)
    tt = lax.broadcasted_iota(I32, (t, ncmp), 0)
    n = lax.broadcasted_iota(I32, (t, ncmp), 1)
    bc_ref[0] = _t5_bias(tt - (n * CMP_STRIDE + CMP_BLOCK - 1), val)


def _t5_tiles(t5_table, t, ncmp):
    nh = t5_table.shape[1]
    return pl.pallas_call(
        functools.partial(_t5_tiles_kernel, t=t, ncmp=ncmp),
        grid=(nh,),
        in_specs=[pl.BlockSpec(memory_space=pltpu.SMEM)],
        out_specs=[pl.BlockSpec((1, 2, LANES, LANES), lambda i: (i, 0, 0, 0)),
                   pl.BlockSpec((1, t, ncmp), lambda i: (i, 0, 0))],
        out_shape=[jax.ShapeDtypeStruct((nh, 2, LANES, LANES), F32),
                   jax.ShapeDtypeStruct((nh, t, ncmp), F32)],
        compiler_params=_params(("arbitrary",)),
        name="t5_tiles",
    )(t5_table)


def _topk_steps(score, k):
    lane = lax.broadcasted_iota(I32, score.shape, 1).astype(F32)
    big = float(score.shape[1])
    out = []
    for n in range(k):
        m = jnp.max(score, axis=1, keepdims=True)
        idx = jnp.min(jnp.where(score == m, lane, big), axis=1, keepdims=True)
        hit = lane == idx
        out.append((n, hit, idx, m))
        score = jnp.where(hit, -jnp.inf, score)
    return out


def _topk_mask(score, k):
    sel = jnp.zeros(score.shape, F32)
    for _, hit, _, _ in _topk_steps(score, k):
        sel = jnp.where(hit, 1.0, sel)
    return sel > 0.5


def _nsa_attn_kernel(q_ref, kcmp_ref, vcmp_ref, ks_ref, vs_ref, kw_ref, vw_ref, gt_ref, bc_ref,
                     tiles_ref, ov_ref, ex_ref, tc_ref, o_ref, km_ref, *, tq, t, nkv, grp, hd):
    qi = pl.program_id(1)
    ncmp = kcmp_ref.shape[1]
    nsel = t // SEL_BLOCK
    m4 = grp * tq
    trow = qi * tq + lax.broadcasted_iota(I32, (tq, LANES), 0)
    lane = lax.broadcasted_iota(I32, (tq, LANES), 1)
    gates = gt_ref[0]

    def stack(fn):
        return jnp.concatenate([fn(g) for g in range(grp)], axis=0)

    r4 = lax.broadcasted_iota(I32, (m4, LANES), 0) % tq
    c4 = lax.broadcasted_iota(I32, (m4, LANES), 1)

    for k in range(nkv):
        q4 = stack(lambda g: q_ref[0, :, (k * grp + g) * hd:(k * grp + g + 1) * hd])
        kc = kcmp_ref[0, :, k * hd:(k + 1) * hd]
        vc = vcmp_ref[0, :, k * hd:(k + 1) * hd]
        bias_c = stack(lambda g: bc_ref[k * grp + g])
        t4 = qi * tq + r4
        mask_c = (c4 * CMP_STRIDE + CMP_BLOCK - 1) <= t4
        s = jnp.where(mask_c, _dot_nt(q4, kc) + bias_c, NEG_BIG)
        mx = jnp.max(s, axis=1, keepdims=True)
        e = jnp.where(mask_c, jnp.exp(s - mx), 0.0)
        p = e / jnp.maximum(jnp.sum(e, axis=1, keepdims=True), 1e-30)
        pb = p.astype(BF)
        o_cmp = _dot(pb, vc)
        imp4 = _dot(pb, ov_ref[...])
        imp = imp4[0:tq]
        for g in range(1, grp):
            imp = imp + imp4[g * tq:(g + 1) * tq]
        cur = trow // SEL_BLOCK
        forced = (lane == 0) | (lane == cur) | (lane == cur - 1)
        score = jnp.where(forced, FORCE_SCORE, imp)
        score = jnp.where(lane * SEL_BLOCK <= trow, score, NEG_SCORE)
        score = jnp.where(lane < nsel, score, -jnp.inf)
        sel = _topk_mask(score, min(SEL_TOP_N, nsel)) & (score > 0.5 * NEG_SCORE)
        km_ref[...] = _dot(jnp.where(sel, 1.0, 0.0).astype(BF), ex_ref[...])

        def attend(ki, carry, kref, vref, bias, mask_fn):
            m_p, l_p, acc = carry
            start = pl.multiple_of(ki * LANES, LANES)
            kt = kref[0, pl.ds(start, LANES), k * hd:(k + 1) * hd]
            vt = vref[0, pl.ds(start, LANES), k * hd:(k + 1) * hd]
            sc = _dot_nt(q4, kt) + bias
            sc = jnp.where(mask_fn(ki, start), sc, NEG_BIG)
            p_, alpha, m_n, l_n = _softmax_step(sc, m_p, l_p)
            return m_n, l_n, alpha * acc + _dot(p_.astype(BF), vt)

        def const_bias():
            return stack(lambda g: jnp.zeros((tq, LANES), F32) + tc_ref[k * grp + g])

        def tile_bias(which):
            return stack(lambda g: tiles_ref[k * grp + g, which])

        def sel_mask(ki, start):
            km = km_ref[:, pl.ds(start, LANES)]
            km4 = jnp.concatenate([km] * grp, axis=0)
            return (km4 > 0.5) & ((ki * LANES + c4) <= t4)

        def win_mask(ki, start):
            dist = t4 - (ki * LANES + c4)
            return (dist >= 0) & (dist <= WINDOW)

        init = (jnp.full((m4, 1), NEG_BIG, F32), jnp.zeros((m4, 1), F32), jnp.zeros((m4, hd), F32))

        def run(kref, vref, mask_fn, first):
            far = lax.fori_loop(first, jnp.maximum(qi - 1, first),
                                lambda ki, c: attend(ki, c, kref, vref, const_bias(), mask_fn), init)
            near = lax.cond(qi >= 1,
                            lambda c: attend(qi - 1, c, kref, vref, tile_bias(1), mask_fn),
                            lambda c: c, far)
            m_f, l_f, acc = attend(qi, near, kref, vref, tile_bias(0), mask_fn)
            return acc / l_f

        o_slc = run(ks_ref, vs_ref, sel_mask, 0)
        o_win = run(kw_ref, vw_ref, win_mask, jnp.maximum(qi - WINDOW // LANES, 0))
        for g in range(grp):
            hh = k * grp + g
            og = (gates[:, 3 * hh:3 * hh + 1] * o_cmp[g * tq:(g + 1) * tq]
                  + gates[:, 3 * hh + 1:3 * hh + 2] * o_slc[g * tq:(g + 1) * tq]
                  + gates[:, 3 * hh + 2:3 * hh + 3] * o_win[g * tq:(g + 1) * tq])
            o_ref[0, :, hh * hd:(hh + 1) * hd] = og.astype(BF)


def _nsa_attn(q, kcmp, vcmp, ksb, vsb, kwb, vwb, gates, bias_c, tiles, ov, ex, tconst, nkv, grp, hd,
              tq=128):
    b, t, wq = q.shape
    nq = t // tq
    nh = nkv * grp
    full_b = lambda a: pl.BlockSpec((1,) + a.shape[1:], lambda bi, qi: (bi,) + (0,) * (a.ndim - 1))
    return pl.pallas_call(
        functools.partial(_nsa_attn_kernel, tq=tq, t=t, nkv=nkv, grp=grp, hd=hd),
        grid=(b, nq),
        in_specs=[pl.BlockSpec((1, tq, wq), lambda bi, qi: (bi, qi, 0)),
                  full_b(kcmp), full_b(vcmp), full_b(ksb), full_b(vsb), full_b(kwb), full_b(vwb),
                  pl.BlockSpec((1, tq, LANES), lambda bi, qi: (bi, qi, 0)),
                  pl.BlockSpec((nh, tq, LANES), lambda bi, qi: (0, qi, 0)),
                  _const_spec(tiles.shape), _const_spec(ov.shape), _const_spec(ex.shape),
                  pl.BlockSpec(memory_space=pltpu.SMEM)],
        out_specs=pl.BlockSpec((1, tq, wq), lambda bi, qi: (bi, qi, 0)),
        out_shape=jax.ShapeDtypeStruct((b, t, wq), BF),
        scratch_shapes=[pltpu.VMEM((tq, t), F32)],
        compiler_params=_params(("parallel", "arbitrary")),
        name="nsa_attn",
    )(q, kcmp, vcmp, ksb, vsb, kwb, vwb, gates, bias_c, tiles, ov, ex, tconst)


def _page_copies(pt_ref, b, first_page, npages, pools, bufs, sems, layer):
    out = []
    for j in range(npages):
        page = pt_ref[b, first_page + j]
        for pool, buf, sem in zip(pools, bufs, sems):
            out.append(pltpu.make_async_copy(pool.at[layer, page], buf(j), sem))
    return out


def _paged_pipeline(pt_ref, pools, bufs_of_slot, sem_of_slot, layer, npages):
    b = pl.program_id(0)
    s = pl.program_id(1)
    nb = pl.num_programs(0)
    ns = pl.num_programs(1)
    g = b * ns + s
    slot = g % 2

    def copies(bb, ss, sl):
        return _page_copies(pt_ref, bb, ss * npages, npages, pools, bufs_of_slot(sl), sem_of_slot(sl), layer)

    @pl.when(g == 0)
    def _():
        for c in copies(b, s, slot):
            c.start()

    @pl.when(g + 1 < nb * ns)
    def _():
        wrap = s + 1 == ns
        for c in copies(jnp.where(wrap, b + 1, b), jnp.where(wrap, 0, s + 1), 1 - slot):
            c.start()

    for c in copies(b, s, slot):
        c.wait()
    return slot


def _mla_dec_kernel(pt_ref, q_ref, knew_ref, wuv_ref, ckv_hbm, kr_hbm, o_ref,
                    cbuf, rbuf, sem, m_ref, l_ref, acc_ref, *, layer, npages, nh, scale, vdim):
    s_id = pl.program_id(1)
    ns = pl.num_programs(1)
    slot = _paged_pipeline(
        pt_ref, (ckv_hbm, kr_hbm),
        lambda sl: (lambda j: cbuf.at[sl, j], lambda j: rbuf.at[sl, j]),
        lambda sl: (sem.at[0, sl], sem.at[1, sl]), layer, npages)

    @pl.when(s_id == 0)
    def _():
        m_ref[...] = jnp.full(m_ref.shape, NEG_BIG, F32)
        l_ref[...] = jnp.zeros(l_ref.shape, F32)
        acc_ref[...] = jnp.zeros(acc_ref.shape, F32)

    q = q_ref[0]
    page = cbuf.shape[2]
    kp = cbuf[slot].reshape(npages * page, cbuf.shape[3]).astype(BF)
    rp = rbuf[slot].reshape(npages * page, rbuf.shape[3]).astype(BF)
    sc = (_dot_nt(q[:, 0:128].astype(BF), kp) + _dot_nt(q[:, 128:128 + MLA_ROPE].astype(BF), rp)) * scale
    p, alpha, m_new, l_new = _softmax_step(sc, m_ref[...], l_ref[...])
    acc_ref[...] = alpha * acc_ref[...] + _dot(p.astype(BF), kp)
    m_ref[...] = m_new
    l_ref[...] = l_new

    @pl.when(s_id == ns - 1)
    def _():
        kn = knew_ref[0]
        s_new = jnp.sum(q * kn, axis=1, keepdims=True) * scale
        m_p = m_ref[...]
        m_n = jnp.maximum(m_p, s_new)
        a = jnp.exp(m_p - m_n)
        p_new = jnp.exp(s_new - m_n)
        l_f = a * l_ref[...] + p_new
        acc = a * acc_ref[...] + p_new.astype(BF).astype(F32) * kn[:, 0:128]
        olat = (acc / l_f).astype(BF)
        r = _dot(olat, wuv_ref[...])
        row = lax.broadcasted_iota(I32, r.shape, 0)
        col = lax.broadcasted_iota(I32, r.shape, 1)
        o_ref[0] = jnp.sum(jnp.where(col // vdim == row, r, 0.0), axis=0, keepdims=True).astype(BF)


def _mla_dec(page_table, q, knew, wuv_all, ckv_pool, kr_pool, layer, nh, scale, npages=16):
    db, n_pages = page_table.shape
    page = ckv_pool.shape[2]
    vdim = wuv_all.shape[1] // nh
    grid_spec = pltpu.PrefetchScalarGridSpec(
        num_scalar_prefetch=1, grid=(db, n_pages // npages),
        in_specs=[pl.BlockSpec((1, nh, 256), lambda b, s, pt: (b, 0, 0)),
                  pl.BlockSpec((1, 1, 256), lambda b, s, pt: (b, 0, 0)),
                  _const_spec(wuv_all.shape),
                  pl.BlockSpec(memory_space=pl.ANY), pl.BlockSpec(memory_space=pl.ANY)],
        out_specs=pl.BlockSpec((1, 1, wuv_all.shape[1]), lambda b, s, pt: (b, 0, 0)),
        scratch_shapes=[pltpu.VMEM((2, npages, page, ckv_pool.shape[3]), F32),
                        pltpu.VMEM((2, npages, page, kr_pool.shape[3]), F32),
                        pltpu.SemaphoreType.DMA((2, 2)),
                        pltpu.VMEM((nh, 1), F32), pltpu.VMEM((nh, 1), F32), pltpu.VMEM((nh, 128), F32)])
    return pl.pallas_call(
        functools.partial(_mla_dec_kernel, layer=layer, npages=npages, nh=nh, scale=scale, vdim=vdim),
        grid_spec=grid_spec,
        out_shape=jax.ShapeDtypeStruct((db, 1, wuv_all.shape[1]), BF),
        compiler_params=_params(("arbitrary", "arbitrary")),
        name="mla_dec",
    )(page_table, q, knew, wuv_all, ckv_pool, kr_pool)


def _fox_bias_kernel(pt_ref, lfn_ref, lf_hbm, o_ref, lbuf, sem, *, layer, n_pages):
    b = pl.program_id(0)
    page = lbuf.shape[1]

    def copy(p):
        return pltpu.make_async_copy(lf_hbm.at[layer, pt_ref[b, p]], lbuf.at[p], sem.at[0])

    def start(p, c):
        copy(p).start()
        return c

    def wait(p, c):
        copy(p).wait()
        return c

    lax.fori_loop(0, n_pages, start, 0)
    lax.fori_loop(0, n_pages, wait, 0)

    def xpose(p, c):
        o_ref[0, :, pl.ds(pl.multiple_of(p * page, page), page)] = lbuf[p].T
        return c

    lax.fori_loop(0, n_pages, xpose, 0)
    x = o_ref[0]
    n = x.shape[1]
    lane = lax.broadcasted_iota(I32, x.shape, 1)
    y = x
    k = 1
    while k < n:
        y = y + jnp.where(lane + k < n, pltpu.roll(y, n - k, axis=1), 0.0)
        k *= 2
    o_ref[0] = (y - x) + lfn_ref[0]


def _fox_bias(page_table, lf_new, lf_pool, layer):
    db, n_pages = page_table.shape
    page, nh = lf_pool.shape[2], lf_pool.shape[3]
    grid_spec = pltpu.PrefetchScalarGridSpec(
        num_scalar_prefetch=1, grid=(db,),
        in_specs=[pl.BlockSpec((1, nh, 1), lambda b, pt: (b, 0, 0)), pl.BlockSpec(memory_space=pl.ANY)],
        out_specs=pl.BlockSpec((1, nh, n_pages * page), lambda b, pt: (b, 0, 0)),
        scratch_shapes=[pltpu.VMEM((n_pages, page, nh), F32), pltpu.SemaphoreType.DMA((1,))])
    return pl.pallas_call(
        functools.partial(_fox_bias_kernel, layer=layer, n_pages=n_pages),
        grid_spec=grid_spec,
        out_shape=jax.ShapeDtypeStruct((db, nh, n_pages * page), F32),
        compiler_params=_params(("arbitrary",)),
        name="fox_bias",
    )(page_table, lf_new, lf_pool)


def _fox_dec_kernel(pt_ref, q_ref, kn_ref, vn_ref, bias_ref, k_hbm, v_hbm, o_ref,
                    kbuf, vbuf, sem, m_ref, l_ref, acc_ref, *, layer, npages, nh):
    s_id = pl.program_id(1)
    ns = pl.num_programs(1)
    rows = kbuf.shape[1] // npages
    slot = _paged_pipeline(
        pt_ref, (k_hbm, v_hbm),
        lambda sl: (lambda j: kbuf.at[sl, pl.ds(j * rows, rows), :], lambda j: vbuf.at[sl, pl.ds(j * rows, rows), :]),
        lambda sl: (sem.at[0, sl], sem.at[1, sl]), layer, npages)

    @pl.when(s_id == 0)
    def _():
        m_ref[...] = jnp.full(m_ref.shape, NEG_BIG, F32)
        l_ref[...] = jnp.zeros(l_ref.shape, F32)
        acc_ref[...] = jnp.zeros(acc_ref.shape, F32)

    q = q_ref[0]
    nkeys = kbuf.shape[1] // nh
    hrow = lax.broadcasted_iota(I32, q.shape, 0)
    sc = bias_ref[0]
    for h in range(nh):
        kh = kbuf[slot, pl.ds(h, nkeys, stride=nh), :].astype(BF)
        sc = sc + _dot_nt(jnp.where(hrow == h, q, 0.0).astype(BF), kh)
    p, alpha, m_new, l_new = _softmax_step(sc, m_ref[...], l_ref[...])
    prow = lax.broadcasted_iota(I32, p.shape, 0)
    acc = alpha * acc_ref[...]
    for h in range(nh):
        vh = vbuf[slot, pl.ds(h, nkeys, stride=nh), :].astype(BF)
        acc = acc + _dot(jnp.where(prow == h, p, 0.0).astype(BF), vh)
    acc_ref[...] = acc
    m_ref[...] = m_new
    l_ref[...] = l_new

    @pl.when(s_id == ns - 1)
    def _():
        kn = kn_ref[0].astype(BF).astype(F32)
        vn = vn_ref[0].astype(BF).astype(F32)
        s_new = jnp.sum(q * kn, axis=1, keepdims=True)
        m_p = m_ref[...]
        m_n = jnp.maximum(m_p, s_new)
        a = jnp.exp(m_p - m_n)
        p_new = jnp.exp(s_new - m_n)
        l_f = a * l_ref[...] + p_new
        o_ref[0] = ((a * acc_ref[...] + p_new.astype(BF).astype(F32) * vn) / l_f).astype(BF)


def _fox_dec(page_table, q, kn, vn, bias, k_pool, v_pool, layer, nh, npages=8):
    db, n_pages = page_table.shape
    rows, hd = k_pool.shape[2], k_pool.shape[3]
    page = rows // nh
    grid_spec = pltpu.PrefetchScalarGridSpec(
        num_scalar_prefetch=1, grid=(db, n_pages // npages),
        in_specs=[pl.BlockSpec((1, nh, hd), lambda b, s, pt: (b, 0, 0)),
                  pl.BlockSpec((1, nh, hd), lambda b, s, pt: (b, 0, 0)),
                  pl.BlockSpec((1, nh, hd), lambda b, s, pt: (b, 0, 0)),
                  pl.BlockSpec((1, nh, npages * page), lambda b, s, pt: (b, 0, s)),
                  pl.BlockSpec(memory_space=pl.ANY), pl.BlockSpec(memory_space=pl.ANY)],
        out_specs=pl.BlockSpec((1, nh, hd), lambda b, s, pt: (b, 0, 0)),
        scratch_shapes=[pltpu.VMEM((2, npages * rows, hd), F32), pltpu.VMEM((2, npages * rows, hd), F32),
                        pltpu.SemaphoreType.DMA((2, 2)),
                        pltpu.VMEM((nh, 1), F32), pltpu.VMEM((nh, 1), F32), pltpu.VMEM((nh, hd), F32)])
    return pl.pallas_call(
        functools.partial(_fox_dec_kernel, layer=layer, npages=npages, nh=nh),
        grid_spec=grid_spec,
        out_shape=jax.ShapeDtypeStruct((db, nh, hd), BF),
        compiler_params=_params(("arbitrary", "arbitrary")),
        name="fox_dec",
    )(page_table, q, kn, vn, bias, k_pool, v_pool)


def _rows(fn, n):
    return jnp.concatenate([fn(g) for g in range(n)], axis=0)


def _nsa_dec_a_kernel(pt_ref, q_ref, w1k_ref, w1v_ref, pek_ref, pev_ref, w1fk_ref, w1fv_ref, w2k_ref,
                      w2v_ref, ov_ref, tbl_ref, kc_hbm, vc_hbm, oc_ref, ti_ref, ok_ref,
                      kbuf, vbuf, sem, fsk_ref, fsv_ref, *, layer, npages, nkv, grp, hd, pos, nsel):
    s_id = pl.program_id(1)
    ns = pl.num_programs(1)
    rows = kbuf.shape[1] // npages
    slot = _paged_pipeline(
        pt_ref, (kc_hbm, vc_hbm),
        lambda sl: (lambda j: kbuf.at[sl, pl.ds(j * rows, rows), :], lambda j: vbuf.at[sl, pl.ds(j * rows, rows), :]),
        lambda sl: (sem.at[0, sl], sem.at[1, sl]), layer, npages)
    nchunk = kbuf.shape[1] // (nkv * CMP_STRIDE)
    base = pl.multiple_of(s_id * nchunk, nchunk)
    for buf, w_ref, fs_ref in ((kbuf, w1k_ref, fsk_ref), (vbuf, w1v_ref, fsv_ref)):
        for k in range(nkv):
            y = jnp.zeros((nchunk, 2 * hd), F32)
            for r in range(CMP_STRIDE):
                xr = buf[slot, pl.ds(nkv * r + k, nchunk, stride=nkv * CMP_STRIDE), :].astype(BF)
                y = y + _dot(xr, w_ref[r])
            fs_ref[k, pl.ds(base, nchunk), :] = y

    @pl.when(s_id == ns - 1)
    def _():
        ncmp = fsk_ref.shape[1]
        n_valid = (pos - (CMP_BLOCK - 1)) // CMP_STRIDE + 1
        lane = lax.broadcasted_iota(I32, (1, ncmp), 1)
        dist_c = pos - (lane * CMP_STRIDE + CMP_BLOCK - 1)
        pe_k = _dot(pek_ref[...], w1fk_ref[...])[0:1, :]
        pe_v = _dot(pev_ref[...], w1fv_ref[...])[0:1, :]
        imps = []
        for k in range(nkv):
            def cmp_rows(fs_ref, pe, w2_ref):
                fs = fs_ref[k]
                second = pltpu.roll(fs, ncmp - 1, axis=0)[:, hd:2 * hd]
                hid = _gelu(fs[:, 0:hd] + second + pe).astype(BF)
                return _dot(hid, w2_ref[...]).astype(BF)

            kcmp = cmp_rows(fsk_ref, pe_k, w2k_ref)
            vcmp = cmp_rows(fsv_ref, pe_v, w2v_ref)
            qk = q_ref[0, k * grp:(k + 1) * grp, :].astype(BF)
            bias = _rows(lambda g: _t5_bias(dist_c, lambda j: tbl_ref[j, k * grp + g]), grp)
            valid = lane < n_valid
            s = jnp.where(valid, _dot_nt(qk, kcmp) + bias, NEG_BIG)
            mx = jnp.max(s, axis=1, keepdims=True)
            e = jnp.where(valid, jnp.exp(s - mx), 0.0)
            p = (e / jnp.maximum(jnp.sum(e, axis=1, keepdims=True), 1e-30)).astype(BF)
            oc_ref[0, k * grp:(k + 1) * grp, :] = _dot(p, vcmp)
            imps.append(jnp.sum(_dot(p, ov_ref[...]), axis=0, keepdims=True))
        imp = jnp.concatenate(imps, axis=0)
        jl = lax.broadcasted_iota(I32, imp.shape, 1)
        cur = pos // SEL_BLOCK
        forced = (jl == 0) | (jl == cur) | (jl == cur - 1)
        score = jnp.where(forced, FORCE_SCORE, imp)
        score = jnp.where(jl * SEL_BLOCK <= pos, score, NEG_SCORE)
        score = jnp.where(jl < nsel, score, -jnp.inf)
        ol = lax.broadcasted_iota(I32, (nkv, LANES), 1)
        ti = jnp.zeros((nkv, LANES), I32)
        ok = jnp.zeros((nkv, LANES), I32)
        for n, _, idx, val in _topk_steps(score, min(SEL_TOP_N, nsel)):
            ti = jnp.where(ol == n, idx.astype(I32), ti)
            ok = jnp.where(ol == n, (val > 0.5 * NEG_SCORE).astype(I32), ok)
        ti_ref[0] = ti
        ok_ref[0] = ok


def _nsa_dec_a(page_table, q, w1k, w1v, pek, pev, w1fk, w1fv, w2k, w2v, ov, t5_table, kc_pool, vc_pool,
               layer, nkv, grp, hd, pos, nsel, npages=16):
    db, n_pages = page_table.shape
    rows = kc_pool.shape[2]
    page = rows // nkv
    ncmp = n_pages * page // CMP_STRIDE
    consts = [w1k, w1v, pek, pev, w1fk, w1fv, w2k, w2v, ov]
    grid_spec = pltpu.PrefetchScalarGridSpec(
        num_scalar_prefetch=1, grid=(db, n_pages // npages),
        in_specs=([pl.BlockSpec((1, nkv * grp, hd), lambda b, s, pt: (b, 0, 0))]
                  + [_const_spec(c.shape) for c in consts]
                  + [pl.BlockSpec(memory_space=pltpu.SMEM),
                     pl.BlockSpec(memory_space=pl.ANY), pl.BlockSpec(memory_space=pl.ANY)]),
        out_specs=[pl.BlockSpec((1, nkv * grp, hd), lambda b, s, pt: (b, 0, 0)),
                   pl.BlockSpec((1, nkv, LANES), lambda b, s, pt: (b, 0, 0)),
                   pl.BlockSpec((1, nkv, LANES), lambda b, s, pt: (b, 0, 0))],
        scratch_shapes=[pltpu.VMEM((2, npages * rows, hd), F32), pltpu.VMEM((2, npages * rows, hd), F32),
                        pltpu.SemaphoreType.DMA((2, 2)),
                        pltpu.VMEM((nkv, ncmp, 2 * hd), F32), pltpu.VMEM((nkv, ncmp, 2 * hd), F32)])
    return pl.pallas_call(
        functools.partial(_nsa_dec_a_kernel, layer=layer, npages=npages, nkv=nkv, grp=grp, hd=hd,
                          pos=pos, nsel=nsel),
        grid_spec=grid_spec,
        out_shape=[jax.ShapeDtypeStruct((db, nkv * grp, hd), F32),
                   jax.ShapeDtypeStruct((db, nkv, LANES), I32),
                   jax.ShapeDtypeStruct((db, nkv, LANES), I32)],
        compiler_params=_params(("arbitrary", "arbitrary")),
        name="nsa_dec_cmp",
    )(page_table, q, *consts, t5_table, kc_pool, vc_pool)


def _nsa_dec_b_kernel(pt_ref, ti_ref, ok_ref, q_ref, ksn_ref, vsn_ref, kwn_ref, vwn_ref, wk_ref, wv_ref,
                      gt_ref, oc_ref, tbl_ref, ks_hbm, vs_hbm, o_ref, kb, vb, sem,
                      *, layer, nkv, grp, hd, pos, npast_sel, blk_rows, blocks_per_page, nwin):
    b = pl.program_id(0)
    ntop = kb.shape[1] // blk_rows

    def sel_dma(k, n, wait):
        blk = ti_ref[b, k * ntop + n]

        @pl.when(blk < npast_sel)
        def _():
            pg = pt_ref[b, blk // blocks_per_page]
            off = pl.multiple_of((blk % blocks_per_page) * blk_rows, blk_rows)
            for pool, buf, sm in ((ks_hbm, kb, sem.at[0]), (vs_hbm, vb, sem.at[1])):
                c = pltpu.make_async_copy(pool.at[layer, pg, pl.ds(off, blk_rows), :],
                                          buf.at[k, pl.ds(n * blk_rows, blk_rows), :], sm)
                if wait:
                    c.wait()
                else:
                    c.start()

        if not wait:
            @pl.when(blk >= npast_sel)
            def _():
                for buf, new_ref in ((kb, ksn_ref), (vb, vsn_ref)):
                    buf[k, pl.ds(n * blk_rows, blk_rows), :] = jnp.zeros((blk_rows, hd), F32)
                    buf[k, pl.ds(n * blk_rows + k, 1), :] = new_ref[0, k:k + 1, :]

    for k in range(nkv):
        for n in range(ntop):
            sel_dma(k, n, False)

    def head_col(fn):
        r = lax.broadcasted_iota(I32, (grp, 1), 0)
        out = jnp.zeros((grp, 1), F32)
        for g in range(grp):
            out = jnp.where(r == g, fn(g), out)
        return out

    def bf_round(x):
        return x.astype(BF).astype(F32)

    o_wins = []
    jw = lax.broadcasted_iota(I32, (1, nwin), 1)
    dist_w = nwin - jw
    for k in range(nkv):
        qk = q_ref[0, k * grp:(k + 1) * grp, :]
        kw = wk_ref[0, 0, pl.ds(k, nwin, stride=nkv), :].astype(BF)
        vw = wv_ref[0, 0, pl.ds(k, nwin, stride=nkv), :].astype(BF)
        bias = _rows(lambda g: _t5_bias(dist_w, lambda j: tbl_ref[j, k * grp + g]), grp)
        s = _dot_nt(qk.astype(BF), kw) + bias
        s_new = (jnp.sum(qk * bf_round(kwn_ref[0, k:k + 1, :]), axis=1, keepdims=True)
                 + head_col(lambda g: tbl_ref[0, k * grp + g]))
        mx = jnp.maximum(jnp.max(s, axis=1, keepdims=True), s_new)
        e = jnp.exp(s - mx)
        e_new = jnp.exp(s_new - mx)
        l = jnp.sum(e, axis=1, keepdims=True) + e_new
        o_wins.append((_dot(e.astype(BF), vw) + bf_round(e_new) * bf_round(vwn_ref[0, k:k + 1, :])) / l)

    for k in range(nkv):
        for n in range(ntop):
            sel_dma(k, n, True)

    nkeys = ntop * SEL_BLOCK
    lane = lax.broadcasted_iota(I32, (1, nkeys), 1)
    slot_id = lane // SEL_BLOCK
    for k in range(nkv):
        qk = q_ref[0, k * grp:(k + 1) * grp, :]
        ks = kb[k, pl.ds(k, nkeys, stride=nkv), :].astype(BF)
        vs = vb[k, pl.ds(k, nkeys, stride=nkv), :].astype(BF)
        blkv = jnp.zeros((1, nkeys), I32)
        okv = jnp.zeros((1, nkeys), I32)
        for n in range(ntop):
            blkv = jnp.where(slot_id == n, ti_ref[b, k * ntop + n], blkv)
            okv = jnp.where(slot_id == n, ok_ref[b, k * ntop + n], okv)
        dist = pos - (blkv * SEL_BLOCK + lane % SEL_BLOCK)
        mask = (okv > 0) & (dist >= 0)
        bias = _rows(lambda g: _t5_bias(dist, lambda j: tbl_ref[j, k * grp + g]), grp)
        s = jnp.where(mask, _dot_nt(qk.astype(BF), ks) + bias, NEG_BIG)
        mx = jnp.max(s, axis=1, keepdims=True)
        e = jnp.where(mask, jnp.exp(s - mx), 0.0)
        l = jnp.maximum(jnp.sum(e, axis=1, keepdims=True), 1e-30)
        o_slc = _dot(e.astype(BF), vs) / l
        gt = gt_ref[0, k * grp:(k + 1) * grp, :]
        o = (gt[:, 0:1] * oc_ref[0, k * grp:(k + 1) * grp, :] + gt[:, 1:2] * o_slc + gt[:, 2:3] * o_wins[k])
        o_ref[0, k * grp:(k + 1) * grp, :] = o.astype(BF)


def _nsa_dec_b(page_table, top_i, top_ok, q, ksn, vsn, kwn, vwn, win_k, win_v, gates, o_cmp, t5_table,
               ks_pool, vs_pool, layer, nkv, grp, hd, pos, npast_sel):
    db = q.shape[0]
    nh = nkv * grp
    ntop = top_i.shape[1] // nkv
    page_rows = ks_pool.shape[2]
    blk_rows = SEL_BLOCK * nkv
    nwin = win_k.shape[2] // nkv
    m3 = lambda b, *_: (b, 0, 0)
    grid_spec = pltpu.PrefetchScalarGridSpec(
        num_scalar_prefetch=3, grid=(db,),
        in_specs=[pl.BlockSpec((1, nh, hd), m3)] + [pl.BlockSpec((1, nkv, hd), m3)] * 4
                 + [pl.BlockSpec((1, 1) + win_k.shape[2:], lambda b, *_: (layer, b, 0, 0))] * 2
                 + [pl.BlockSpec((1, nh, 3), m3), pl.BlockSpec((1, nh, hd), m3),
                    pl.BlockSpec(memory_space=pltpu.SMEM),
                    pl.BlockSpec(memory_space=pl.ANY), pl.BlockSpec(memory_space=pl.ANY)],
        out_specs=pl.BlockSpec((1, nh, hd), m3),
        scratch_shapes=[pltpu.VMEM((nkv, ntop * blk_rows, hd), F32),
                        pltpu.VMEM((nkv, ntop * blk_rows, hd), F32),
                        pltpu.SemaphoreType.DMA((2,))])
    return pl.pallas_call(
        functools.partial(_nsa_dec_b_kernel, layer=layer, nkv=nkv, grp=grp, hd=hd, pos=pos,
                          npast_sel=npast_sel, blk_rows=blk_rows,
                          blocks_per_page=page_rows // blk_rows, nwin=nwin),
        grid_spec=grid_spec,
        out_shape=jax.ShapeDtypeStruct((db, nh, hd), BF),
        compiler_params=_params(("arbitrary",)),
        name="nsa_dec_sel_win",
    )(page_table, top_i, top_ok, q, ksn, vsn, kwn, vwn, win_k, win_v, gates, o_cmp, t5_table,
      ks_pool, vs_pool)


KM_TQ = 128
KM_TK = 256


LOG2E = 1.4426950408889634


def _t5_km_kernel(tbl_ref, tiles_ref, bc_ref, *, t, ncmp):
    h = pl.program_id(0)
    far = tbl_ref[T5_BUCKETS - 1, h]
    rel = lambda j: (tbl_ref[j, h] - far) * LOG2E
    c = lax.broadcasted_iota(I32, (KM_TK, KM_TQ), 0)
    r = lax.broadcasted_iota(I32, (KM_TK, KM_TQ), 1)
    for i in range(3):
        dist = i * KM_TQ + r - c
        tiles_ref[0, i] = jnp.where(dist >= 0, _t5_bias(dist, rel), NEG_BIG)
    tiles_ref[0, 3] = jnp.zeros((KM_TK, KM_TQ), F32)
    n = lax.broadcasted_iota(I32, (ncmp, t), 0)
    tt = lax.broadcasted_iota(I32, (ncmp, t), 1)
    bc_ref[0] = _t5_bias(tt - (n * CMP_STRIDE + CMP_BLOCK - 1), lambda j: tbl_ref[j, h] * LOG2E)


def _t5_km(t5_table, t, ncmp):
    nh = t5_table.shape[1]
    return pl.pallas_call(
        functools.partial(_t5_km_kernel, t=t, ncmp=ncmp),
        grid=(nh,),
        in_specs=[pl.BlockSpec(memory_space=pltpu.SMEM)],
        out_specs=[pl.BlockSpec((1, 4, KM_TK, KM_TQ), lambda i: (i, 0, 0, 0)),
                   pl.BlockSpec((1, ncmp, t), lambda i: (i, 0, 0))],
        out_shape=[jax.ShapeDtypeStruct((nh, 4, KM_TK, KM_TQ), F32),
                   jax.ShapeDtypeStruct((nh, ncmp, t), F32)],
        compiler_params=_params(("arbitrary",)),
        name="t5_tiles",
    )(t5_table)


def _lanes(fn, n):
    return jnp.concatenate([fn(g) for g in range(n)], axis=1)


def _nsa_km_kernel(q_ref, kcmp_ref, vcmpt_ref, ks_ref, vst_ref, kw_ref, vwt_ref, gtt_ref, bc_ref,
                   tiles_ref, ovt_ref, o_ref, selt_ref, *, t, nkv, grp, hd):
    qi = pl.program_id(1)
    tq, tk = KM_TQ, KM_TK
    nq4 = grp * tq
    ncmp = kcmp_ref.shape[2]
    nsel = t // SEL_BLOCK
    per_tile = tk // SEL_BLOCK
    tcol = qi * tq + lax.broadcasted_iota(I32, (1, nq4), 1) % tq
    krow = lax.broadcasted_iota(I32, (tk, 1), 0)
    last = qi // 2
    odd = qi % 2

    q4s = [q_ref[0, k * grp:(k + 1) * grp].reshape(nq4, hd) for k in range(nkv)]
    o_cmps = []
    for k in range(nkv):
        q4 = q4s[k]

        nrow = lax.broadcasted_iota(I32, (ncmp, 1), 0)
        mask_c = (nrow * CMP_STRIDE + CMP_BLOCK - 1) <= tcol
        s = _dot_nt(kcmp_ref[0, k], q4) + _lanes(lambda g: bc_ref[k * grp + g], grp)
        s = jnp.where(mask_c, s, NEG_BIG)
        e = jnp.where(mask_c, jnp.exp2(s - jnp.max(s, axis=0, keepdims=True)), 0.0)
        pb = (e / jnp.maximum(jnp.sum(e, axis=0, keepdims=True), 1e-30)).astype(BF)
        o_cmp = _dot(vcmpt_ref[0, k * hd:(k + 1) * hd, :], pb)
        imp4 = _dot(ovt_ref[...], pb)
        imp = imp4[:, 0:tq]
        for g in range(1, grp):
            imp = imp + imp4[:, g * tq:(g + 1) * tq]

        jrow = lax.broadcasted_iota(I32, imp.shape, 0)
        trow = qi * tq + lax.broadcasted_iota(I32, imp.shape, 1)
        cur = trow // SEL_BLOCK
        forced = (jrow == 0) | (jrow == cur) | (jrow == cur - 1)
        score = jnp.where(forced, FORCE_SCORE, imp)
        score = jnp.where(jrow * SEL_BLOCK <= trow, score, NEG_SCORE)
        score = jnp.where(jrow < nsel, score, -jnp.inf)
        ok = score > 0.5 * NEG_SCORE
        jf = jrow.astype(F32)
        sel = jnp.zeros(imp.shape, F32)
        for _ in range(min(SEL_TOP_N, nsel)):
            mx = jnp.max(score, axis=0, keepdims=True)
            idx = jnp.min(jnp.where(score == mx, jf, float(imp.shape[0])), axis=0, keepdims=True)
            hit = jf == idx
            sel = jnp.where(hit, 1.0, sel)
            score = jnp.where(hit, -jnp.inf, score)
        selt_ref[k] = jnp.where(ok & (sel > 0.5), 0.0, NEG_BIG)
        o_cmps.append(o_cmp)

    def tile(k, ki, carry, kref, vtref, bias):
        m_p, l_p, acc = carry
        start = pl.multiple_of(ki * tk, tk)
        st = _dot_nt(kref[0, k, pl.ds(start, tk), :], q4s[k]) + bias
        m_n = jnp.maximum(m_p, jnp.max(st, axis=0, keepdims=True))
        p = jnp.exp2(st - m_n)
        alpha = jnp.exp2(m_p - m_n)
        l_n = alpha * l_p + jnp.sum(p, axis=0, keepdims=True)
        vt = vtref[0, k * hd:(k + 1) * hd, pl.ds(start, tk)]
        return m_n, l_n, alpha * acc + _dot(vt, p.astype(BF))

    def near_bias(k, which):
        return _lanes(lambda g: tiles_ref[k * grp + g, which], grp)

    def sel_bias(k, ki):
        rows = [jnp.broadcast_to(selt_ref[k, pl.ds(ki * per_tile + j, 1), :], (SEL_BLOCK, tq))
                for j in range(per_tile)]
        return jnp.concatenate([jnp.concatenate(rows, axis=0)] * grp, axis=1)

    init = (jnp.full((1, nq4), NEG_BIG, F32), jnp.zeros((1, nq4), F32), jnp.zeros((hd, nq4), F32))

    far = lax.fori_loop(
        0, jnp.maximum(last - 1, 0),
        lambda ki, cs: tuple(tile(k, ki, cs[k], ks_ref, vst_ref, sel_bias(k, ki)) for k in range(nkv)),
        (init,) * nkv)

    k1 = jnp.maximum(last - 1, 0)
    k2 = jnp.maximum(last - 2, 0)
    pen1 = jnp.where(last >= 1, 0.0, NEG_BIG)
    win_far = jnp.where((tcol - (k2 * tk + krow) <= WINDOW) & (last >= 2), 0.0, NEG_BIG)
    for k in range(nkv):
        c = tile(k, k1, far[k], ks_ref, vst_ref, near_bias(k, 2 + odd) + sel_bias(k, k1) + pen1)
        _, l_s, acc_s = tile(k, last, c, ks_ref, vst_ref, near_bias(k, odd) + sel_bias(k, last))
        o_slc = acc_s / l_s

        c = tile(k, k2, init, kw_ref, vwt_ref, win_far)
        c = tile(k, k1, c, kw_ref, vwt_ref, near_bias(k, 2 + odd) + pen1)
        _, l_w, acc_w = tile(k, last, c, kw_ref, vwt_ref, near_bias(k, odd))
        o_win = acc_w / l_w

        gate = lambda br: _lanes(lambda g: gtt_ref[0, pl.ds(3 * (k * grp + g) + br, 1), :], grp)
        o_t = gate(0) * o_cmps[k] + gate(1) * o_slc + gate(2) * o_win
        for g in range(grp):
            hh = k * grp + g
            o_ref[0, :, hh * hd:(hh + 1) * hd] = o_t[:, g * tq:(g + 1) * tq].T.astype(BF)


def _nsa_attn_km(q, kcmp, vcmpt, ksh, vst, kwh, vwt, gatest, bias_ct, tiles, ovt, nkv, grp, hd):
    b, nh, t, _ = q.shape
    tq = KM_TQ
    assert WINDOW == 2 * KM_TK and t % KM_TK == 0
    full_b = lambda a: pl.BlockSpec((1,) + a.shape[1:], lambda bi, qi: (bi,) + (0,) * (a.ndim - 1))
    single = lambda a: pl.BlockSpec(a.shape, lambda bi, qi: (0,) * a.ndim, pipeline_mode=pl.Buffered(1))
    return pl.pallas_call(
        functools.partial(_nsa_km_kernel, t=t, nkv=nkv, grp=grp, hd=hd),
        grid=(b, t // tq),
        in_specs=[pl.BlockSpec((1, nh, tq, hd), lambda bi, qi: (bi, 0, qi, 0)),
                  full_b(kcmp), full_b(vcmpt), full_b(ksh), full_b(vst), full_b(kwh), full_b(vwt),
                  pl.BlockSpec((1, LANES, tq), lambda bi, qi: (bi, 0, qi)),
                  pl.BlockSpec((nh, bias_ct.shape[1], tq), lambda bi, qi: (0, 0, qi)),
                  single(tiles), _const_spec(ovt.shape)],
        out_specs=pl.BlockSpec((1, tq, nh * hd), lambda bi, qi: (bi, qi, 0)),
        out_shape=jax.ShapeDtypeStruct((b, t, nh * hd), BF),
        scratch_shapes=[pltpu.VMEM((nkv, LANES, tq), F32)],
        compiler_params=_params(("parallel", "arbitrary")),
        name="nsa_attn",
    )(q, kcmp, vcmpt, ksh, vst, kwh, vwt, gatest, bias_ct, tiles, ovt)


def _step_pipeline(copies):
    b = pl.program_id(0)
    s = pl.program_id(1)
    nb = pl.num_programs(0)
    ns = pl.num_programs(1)
    g = b * ns + s
    slot = g % 2

    @pl.when(g == 0)
    def _():
        for c in copies(b, s, slot):
            c.start()

    @pl.when(g + 1 < nb * ns)
    def _():
        wrap = s + 1 == ns
        for c in copies(jnp.where(wrap, b + 1, b), jnp.where(wrap, 0, s + 1), 1 - slot):
            c.start()

    for c in copies(b, s, slot):
        c.wait()
    return slot


def _diag_rows(r, width):
    row = lax.broadcasted_iota(I32, r.shape, 0)
    col = lax.broadcasted_iota(I32, r.shape, 1)
    return jnp.sum(jnp.where(col // width == row, r, 0.0), axis=0, keepdims=True)


def _bf_round(x):
    return x.astype(BF).astype(F32)


def _mla_dec2_kernel(pt_ref, q_ref, knew_ref, wuv_ref, ckv_hbm, krt_hbm, o_ref,
                     cbuf, rbuf, sem, m_ref, l_ref, acc_ref, *, layer, npages, scale, vdim):
    s_id = pl.program_id(1)
    ns = pl.num_programs(1)
    page = cbuf.shape[2]

    def copies(bb, ss, sl):
        out = []
        for j in range(npages):
            pg = pt_ref[bb, ss * npages + j]
            out.append(pltpu.make_async_copy(ckv_hbm.at[layer, pg], cbuf.at[sl, j], sem.at[0, sl]))
            out.append(pltpu.make_async_copy(krt_hbm.at[layer, pg], rbuf.at[sl, :, pl.ds(j * page, page)],
                                             sem.at[1, sl]))
        return out

    slot = _step_pipeline(copies)

    @pl.when(s_id == 0)
    def _():
        m_ref[...] = jnp.full(m_ref.shape, NEG_BIG, F32)
        l_ref[...] = jnp.zeros(l_ref.shape, F32)
        acc_ref[...] = jnp.zeros(acc_ref.shape, F32)

    q = q_ref[0]
    kp = cbuf[slot].reshape(npages * page, cbuf.shape[3]).astype(BF)
    sc = (_dot_nt(q[:, 0:128].astype(BF), kp)
          + _dot(q[:, 128:128 + MLA_ROPE].astype(BF), rbuf[slot].astype(BF))) * scale
    p, alpha, m_new, l_new = _softmax_step(sc, m_ref[...], l_ref[...])
    acc_ref[...] = alpha * acc_ref[...] + _dot(p.astype(BF), kp)
    m_ref[...] = m_new
    l_ref[...] = l_new

    @pl.when(s_id == ns - 1)
    def _():
        kn = knew_ref[0]
        s_new = jnp.sum(q * kn, axis=1, keepdims=True) * scale
        m_p = m_ref[...]
        m_n = jnp.maximum(m_p, s_new)
        a = jnp.exp(m_p - m_n)
        p_new = jnp.exp(s_new - m_n)
        l_f = a * l_ref[...] + p_new
        acc = a * acc_ref[...] + _bf_round(p_new) * kn[:, 0:128]
        olat = (acc / l_f).astype(BF)
        o_ref[0] = _diag_rows(_dot(olat, wuv_ref[...]), vdim).astype(BF)


def _mla_dec2(page_table, q, knew, wuv_all, ckv_pool, krt_pool, layer, nh, scale, npages=32):
    db, n_pages = page_table.shape
    page = ckv_pool.shape[2]
    vdim = wuv_all.shape[1] // nh
    grid_spec = pltpu.PrefetchScalarGridSpec(
        num_scalar_prefetch=1, grid=(db, n_pages // npages),
        in_specs=[pl.BlockSpec((1, nh, 256), lambda b, s, pt: (b, 0, 0)),
                  pl.BlockSpec((1, 1, 256), lambda b, s, pt: (b, 0, 0)),
                  _const_spec(wuv_all.shape),
                  pl.BlockSpec(memory_space=pl.ANY), pl.BlockSpec(memory_space=pl.ANY)],
        out_specs=pl.BlockSpec((1, 1, wuv_all.shape[1]), lambda b, s, pt: (b, 0, 0)),
        scratch_shapes=[pltpu.VMEM((2, npages, page, ckv_pool.shape[3]), F32),
                        pltpu.VMEM((2, krt_pool.shape[2], npages * page), F32),
                        pltpu.SemaphoreType.DMA((2, 2)),
                        pltpu.VMEM((nh, 1), F32), pltpu.VMEM((nh, 1), F32), pltpu.VMEM((nh, 128), F32)])
    return pl.pallas_call(
        functools.partial(_mla_dec2_kernel, layer=layer, npages=npages, scale=scale, vdim=vdim),
        grid_spec=grid_spec,
        out_shape=jax.ShapeDtypeStruct((db, 1, wuv_all.shape[1]), BF),
        compiler_params=_params(("arbitrary", "arbitrary")),
        name="mla_dec",
    )(page_table, q, knew, wuv_all, ckv_pool, krt_pool)


def _suffix_sum_lanes(x):
    n = x.shape[1]
    lane = lax.broadcasted_iota(I32, x.shape, 1)
    y = x
    k = 1
    while k < n:
        y = y + jnp.where(lane + k < n, pltpu.roll(y, n - k, axis=1), 0.0)
        k *= 2
    return y


def _fox_dec2_kernel(pt_ref, q_ref, kn_ref, vn_ref, lfn_ref, kt_hbm, vt_hbm, lft_hbm, o_ref,
                     kbuf, vbuf, lbuf, sem, m_ref, l_ref, acc_ref, carry_ref, *, layer, npages, hd):
    s_id = pl.program_id(1)
    ns = pl.num_programs(1)
    page = kbuf.shape[2] // npages

    def copies(bb, ss, sl):
        out = []
        first = (ns - 1 - ss) * npages
        for j in range(npages):
            pg = pt_ref[bb, first + j]
            win = pl.ds(j * page, page)
            out.append(pltpu.make_async_copy(kt_hbm.at[layer, pg], kbuf.at[sl, :, win], sem.at[0, sl]))
            out.append(pltpu.make_async_copy(vt_hbm.at[layer, pg], vbuf.at[sl, :, win], sem.at[1, sl]))
            out.append(pltpu.make_async_copy(lft_hbm.at[layer, pg], lbuf.at[sl, :, win], sem.at[2, sl]))
        return out

    slot = _step_pipeline(copies)

    @pl.when(s_id == 0)
    def _():
        m_ref[...] = jnp.full(m_ref.shape, NEG_BIG, F32)
        l_ref[...] = jnp.zeros(l_ref.shape, F32)
        acc_ref[...] = jnp.zeros(acc_ref.shape, F32)
        carry_ref[...] = lfn_ref[0]

    q = q_ref[0]
    lf = lbuf[slot]
    incl = _suffix_sum_lanes(lf)
    carry = carry_ref[...]
    sc = _dot(q.astype(BF), kbuf[slot].astype(BF)) + ((incl - lf) + carry)
    carry_ref[...] = carry + incl[:, 0:1]
    p, alpha, m_new, l_new = _softmax_step(sc, m_ref[...], l_ref[...])
    acc_ref[...] = alpha * acc_ref[...] + _dot_nt(p.astype(BF), vbuf[slot].astype(BF))
    m_ref[...] = m_new
    l_ref[...] = l_new

    @pl.when(s_id == ns - 1)
    def _():
        s_new = jnp.sum(q * _bf_round(kn_ref[0]), axis=1, keepdims=True)
        m_p = m_ref[...]
        m_n = jnp.maximum(m_p, s_new)
        a = jnp.exp(m_p - m_n)
        p_new = jnp.exp(s_new - m_n)
        l_f = a * l_ref[...] + p_new
        acc = a * acc_ref[...] + _bf_round(p_new) * _bf_round(vn_ref[0])
        o_ref[0] = _diag_rows(acc / l_f, hd).astype(BF)


def _fox_dec2(page_table, qblk, kn, vn, lf_new, kt_pool, vt_pool, lft_pool, layer, nh, hd, npages=16):
    db, n_pages = page_table.shape
    page = kt_pool.shape[3]
    w = nh * hd
    m3 = lambda b, s, pt: (b, 0, 0)
    grid_spec = pltpu.PrefetchScalarGridSpec(
        num_scalar_prefetch=1, grid=(db, n_pages // npages),
        in_specs=[pl.BlockSpec((1, nh, w), m3), pl.BlockSpec((1, 1, w), m3), pl.BlockSpec((1, 1, w), m3),
                  pl.BlockSpec((1, nh, 1), m3),
                  pl.BlockSpec(memory_space=pl.ANY), pl.BlockSpec(memory_space=pl.ANY),
                  pl.BlockSpec(memory_space=pl.ANY)],
        out_specs=pl.BlockSpec((1, 1, w), m3),
        scratch_shapes=[pltpu.VMEM((2, w, npages * page), F32), pltpu.VMEM((2, w, npages * page), F32),
                        pltpu.VMEM((2, nh, npages * page), F32), pltpu.SemaphoreType.DMA((3, 2)),
                        pltpu.VMEM((nh, 1), F32), pltpu.VMEM((nh, 1), F32), pltpu.VMEM((nh, w), F32),
                        pltpu.VMEM((nh, 1), F32)])
    return pl.pallas_call(
        functools.partial(_fox_dec2_kernel, layer=layer, npages=npages, hd=hd),
        grid_spec=grid_spec,
        out_shape=jax.ShapeDtypeStruct((db, 1, w), BF),
        compiler_params=_params(("arbitrary", "arbitrary")),
        name="fox_dec",
    )(page_table, qblk, kn, vn, lf_new, kt_pool, vt_pool, lft_pool)


def _head_rows(fn, nh):
    return jnp.concatenate([fn(h) for h in range(nh)], axis=0)


def _nsa_cmp_dec_kernel(pt_ref, q_ref, perm_ref, wk_ref, wv_ref, pek_ref, pev_ref, w1k_ref, w1v_ref,
                        w2k_ref, w2v_ref, ov_ref, tbl_ref, kc_hbm, vc_hbm, oc_ref, ti_ref, ok_ref,
                        kbuf, vbuf, sem, xpk_ref, xpv_ref, fsk_ref, fsv_ref,
                        *, layer, npages, nkv, grp, pos, nsel):
    s_id = pl.program_id(1)
    ns = pl.num_programs(1)
    width = kbuf.shape[2]
    per_page = kbuf.shape[3] // CMP_STRIDE
    nchunk = npages * per_page

    def copies(bb, ss, sl):
        out = []
        for j in range(npages):
            pg = pt_ref[bb, ss * npages + j]
            out.append(pltpu.make_async_copy(kc_hbm.at[layer, pg], kbuf.at[sl, j], sem.at[0, sl]))
            out.append(pltpu.make_async_copy(vc_hbm.at[layer, pg], vbuf.at[sl, j], sem.at[1, sl]))
        return out

    slot = _step_pipeline(copies)
    base = pl.multiple_of(s_id * nchunk, nchunk)
    for buf, xp_ref, w_ref, fs_ref in ((kbuf, xpk_ref, wk_ref, fsk_ref), (vbuf, xpv_ref, wv_ref, fsv_ref)):
        for j in range(npages):
            xp = _dot_nt(perm_ref[...], buf[slot, j].astype(BF))
            for r in range(CMP_STRIDE):
                xp_ref[r, j * per_page:(j + 1) * per_page, :] = xp[r * per_page:(r + 1) * per_page, :]
        y = jnp.zeros((nchunk, 2 * width), F32)
        for r in range(CMP_STRIDE):
            y = y + _dot(xp_ref[r].astype(BF), w_ref[r])
        fs_ref[pl.ds(base, nchunk), :] = y

    @pl.when(s_id == ns - 1)
    def _():
        ncmp = fsk_ref.shape[0]
        nh = nkv * grp
        n_valid = (pos - (CMP_BLOCK - 1)) // CMP_STRIDE + 1
        lane = lax.broadcasted_iota(I32, (1, ncmp), 1)
        dist_c = pos - (lane * CMP_STRIDE + CMP_BLOCK - 1)
        kcmp = _compress_tail(fsk_ref[...], pek_ref, w1k_ref, w2k_ref, width).astype(BF)
        vcmp = _compress_tail(fsv_ref[...], pev_ref, w1v_ref, w2v_ref, width).astype(BF)
        bias = _head_rows(lambda h: _t5_bias(dist_c, lambda j: tbl_ref[j, h]), nh)
        valid = lane < n_valid
        s = jnp.where(valid, _dot_nt(q_ref[0].astype(BF), kcmp) + bias, NEG_BIG)
        mx = jnp.max(s, axis=1, keepdims=True)
        e = jnp.where(valid, jnp.exp(s - mx), 0.0)
        p = (e / jnp.maximum(jnp.sum(e, axis=1, keepdims=True), 1e-30)).astype(BF)
        oc_ref[0] = _dot(p, vcmp)
        imp_h = _dot(p, ov_ref[...])
        imp = _head_rows(lambda k: sum(imp_h[k * grp + g:k * grp + g + 1, :] for g in range(grp)), nkv)
        jl = lax.broadcasted_iota(I32, imp.shape, 1)
        cur = pos // SEL_BLOCK
        forced = (jl == 0) | (jl == cur) | (jl == cur - 1)
        score = jnp.where(forced, FORCE_SCORE, imp)
        score = jnp.where(jl * SEL_BLOCK <= pos, score, NEG_SCORE)
        score = jnp.where(jl < nsel, score, -jnp.inf)
        ol = lax.broadcasted_iota(I32, (nkv, LANES), 1)
        ti = jnp.zeros((nkv, LANES), I32)
        ok = jnp.zeros((nkv, LANES), I32)
        for n, _, idx, val in _topk_steps(score, min(SEL_TOP_N, nsel)):
            ti = jnp.where(ol == n, idx.astype(I32), ti)
            ok = jnp.where(ol == n, (val > 0.5 * NEG_SCORE).astype(I32), ok)
        ti_ref[0] = ti
        ok_ref[0] = ok


def _nsa_cmp_dec(page_table, qblk, perm, wk, wv, pek, pev, w1k, w1v, w2k, w2v, ov, t5_table,
                 kc_pool, vc_pool, layer, nkv, grp, pos, nsel, npages=16):
    db, n_pages = page_table.shape
    width, page = kc_pool.shape[2], kc_pool.shape[3]
    nh = nkv * grp
    ncmp = n_pages * page // CMP_STRIDE
    nchunk = npages * page // CMP_STRIDE
    consts = [perm, wk, wv, pek, pev, w1k, w1v, w2k, w2v, ov]
    m3 = lambda b, s, pt: (b, 0, 0)
    grid_spec = pltpu.PrefetchScalarGridSpec(
        num_scalar_prefetch=1, grid=(db, n_pages // npages),
        in_specs=([pl.BlockSpec((1, nh, width), m3)] + [_const_spec(c.shape) for c in consts]
                  + [pl.BlockSpec(memory_space=pltpu.SMEM),
                     pl.BlockSpec(memory_space=pl.ANY), pl.BlockSpec(memory_space=pl.ANY)]),
        out_specs=[pl.BlockSpec((1, nh, width), m3), pl.BlockSpec((1, nkv, LANES), m3),
                   pl.BlockSpec((1, nkv, LANES), m3)],
        scratch_shapes=[pltpu.VMEM((2, npages, width, page), F32), pltpu.VMEM((2, npages, width, page), F32),
                        pltpu.SemaphoreType.DMA((2, 2)),
                        pltpu.VMEM((CMP_STRIDE, nchunk, width), F32), pltpu.VMEM((CMP_STRIDE, nchunk, width), F32),
                        pltpu.VMEM((ncmp, 2 * width), F32), pltpu.VMEM((ncmp, 2 * width), F32)])
    return pl.pallas_call(
        functools.partial(_nsa_cmp_dec_kernel, layer=layer, npages=npages, nkv=nkv, grp=grp, pos=pos, nsel=nsel),
        grid_spec=grid_spec,
        out_shape=[jax.ShapeDtypeStruct((db, nh, width), F32),
                   jax.ShapeDtypeStruct((db, nkv, LANES), I32),
                   jax.ShapeDtypeStruct((db, nkv, LANES), I32)],
        compiler_params=_params(("arbitrary", "arbitrary")),
        name="nsa_dec_cmp",
    )(page_table, qblk, *consts, t5_table, kc_pool, vc_pool)


def _nsa_sel_win_dec_kernel(pt_ref, ti_ref, ok_ref, q_ref, ksn_ref, vsn_ref, kwn_ref, vwn_ref, wk_ref, wv_ref,
                            gt_ref, oc_ref, tbl_ref, ks_hbm, vs_hbm, o_ref, kb, vb, sem,
                            *, layer, nkv, grp, pos, npast_sel, page, nwin):
    b = pl.program_id(0)
    nh = nkv * grp
    ntop = kb.shape[2] // page
    bpp = page // SEL_BLOCK

    hdim = kb.shape[1] // nkv

    @pl.when(b == 0)
    def _():
        kb[...] = jnp.zeros(kb.shape, F32)
        vb[...] = jnp.zeros(vb.shape, F32)

    def sel_dma(k, n, wait):
        blk = ti_ref[b, k * ntop + n]
        win = pl.ds(n * page, page)
        feat = pl.ds(k * hdim, hdim)

        @pl.when(blk < npast_sel)
        def _():
            pg = pt_ref[b, blk // bpp]
            for pool, buf, sm in ((ks_hbm, kb, sem.at[0]), (vs_hbm, vb, sem.at[1])):
                c = pltpu.make_async_copy(pool.at[layer, pg, feat, :], buf.at[k, feat, win], sm)
                if wait:
                    c.wait()
                else:
                    c.start()

        if not wait:
            @pl.when(blk >= npast_sel)
            def _():
                kb[k, feat, win] = jnp.zeros((hdim, page), F32)
                vb[k, feat, win] = jnp.zeros((hdim, page), F32)

    for k in range(nkv):
        for n in range(ntop):
            sel_dma(k, n, False)

    q = q_ref[0]
    qb = q.astype(BF)
    tbl0 = _head_rows(lambda h: jnp.zeros((1, 1), F32) + tbl_ref[0, h], nh)

    jw = lax.broadcasted_iota(I32, (1, nwin), 1)
    bias_w = _head_rows(lambda h: _t5_bias(nwin - jw, lambda j: tbl_ref[j, h]), nh)
    s = _dot(qb, wk_ref[0, 0].astype(BF)) + bias_w
    s_new = jnp.sum(q * _bf_round(kwn_ref[0]), axis=1, keepdims=True) + tbl0
    mx = jnp.maximum(jnp.max(s, axis=1, keepdims=True), s_new)
    e = jnp.exp(s - mx)
    e_new = jnp.exp(s_new - mx)
    l = jnp.sum(e, axis=1, keepdims=True) + e_new
    o_win = (_dot_nt(e.astype(BF), wv_ref[0, 0].astype(BF)) + _bf_round(e_new) * _bf_round(vwn_ref[0])) / l

    for k in range(nkv):
        for n in range(ntop):
            sel_dma(k, n, True)

    nkeys = ntop * page
    lane = lax.broadcasted_iota(I32, (1, nkeys), 1)
    slot_id = lane // page
    row = lane % page
    gates = gt_ref[0]
    for k in range(nkv):
        rows = slice(k * grp, (k + 1) * grp)
        blkv = jnp.zeros((1, nkeys), I32)
        okv = jnp.zeros((1, nkeys), I32)
        new_sel = jnp.int32(0)
        for n in range(ntop):
            blk = ti_ref[b, k * ntop + n]
            okn = ok_ref[b, k * ntop + n]
            blkv = jnp.where(slot_id == n, blk, blkv)
            okv = jnp.where(slot_id == n, okn, okv)
            new_sel = new_sel + jnp.where((blk >= npast_sel) & (okn > 0), 1, 0)
        kpos = (blkv // bpp) * page + row
        dist = pos - kpos
        mask = (okv > 0) & (blkv < npast_sel) & (row // SEL_BLOCK == blkv % bpp) & (dist >= 0)
        bias = _head_rows(lambda g: _t5_bias(dist, lambda j: tbl_ref[j, k * grp + g]), grp)
        sk = jnp.where(mask, _dot(qb[rows], kb[k].astype(BF)) + bias, NEG_BIG)
        sk_new = jnp.sum(q[rows] * _bf_round(ksn_ref[0]), axis=1, keepdims=True) + tbl0[rows]
        sk_new = jnp.where(new_sel > 0, sk_new, NEG_BIG)
        mx = jnp.maximum(jnp.max(sk, axis=1, keepdims=True), sk_new)
        e = jnp.where(mask, jnp.exp(sk - mx), 0.0)
        e_new = jnp.where(new_sel > 0, jnp.exp(sk_new - mx), 0.0)
        l = jnp.maximum(jnp.sum(e, axis=1, keepdims=True) + e_new, 1e-30)
        o_slc = (_dot_nt(e.astype(BF), vb[k].astype(BF)) + _bf_round(e_new) * _bf_round(vsn_ref[0])) / l
        g3 = gates[rows]
        o_ref[0, rows, :] = g3[:, 0:1] * oc_ref[0, rows, :] + g3[:, 1:2] * o_slc + g3[:, 2:3] * o_win[rows]


def _nsa_sel_win_dec(page_table, top_i, top_ok, qblk, ksn, vsn, kwn, vwn, win_kt, win_vt, gates, o_cmp,
                     t5_table, ks_pool, vs_pool, layer, nkv, grp, pos, npast_sel):
    db = qblk.shape[0]
    nh = nkv * grp
    ntop = top_i.shape[1] // nkv
    width, page = ks_pool.shape[2], ks_pool.shape[3]
    nwin = win_kt.shape[3]
    m3 = lambda b, *_: (b, 0, 0)
    grid_spec = pltpu.PrefetchScalarGridSpec(
        num_scalar_prefetch=3, grid=(db,),
        in_specs=[pl.BlockSpec((1, nh, width), m3)] + [pl.BlockSpec((1, 1, width), m3)] * 4
                 + [pl.BlockSpec((1, 1, width, nwin), lambda b, *_: (layer, b, 0, 0))] * 2
                 + [pl.BlockSpec((1, nh, 3), m3), pl.BlockSpec((1, nh, width), m3),
                    pl.BlockSpec(memory_space=pltpu.SMEM),
                    pl.BlockSpec(memory_space=pl.ANY), pl.BlockSpec(memory_space=pl.ANY)],
        out_specs=pl.BlockSpec((1, nh, width), m3),
        scratch_shapes=[pltpu.VMEM((nkv, width, ntop * page), F32), pltpu.VMEM((nkv, width, ntop * page), F32),
                        pltpu.SemaphoreType.DMA((2,))])
    return pl.pallas_call(
        functools.partial(_nsa_sel_win_dec_kernel, layer=layer, nkv=nkv, grp=grp, pos=pos,
                          npast_sel=npast_sel, page=page, nwin=nwin),
        grid_spec=grid_spec,
        out_shape=jax.ShapeDtypeStruct((db, nh, width), F32),
        compiler_params=_params(("arbitrary",)),
        name="nsa_dec_sel_win",
    )(page_table, top_i, top_ok, qblk, ksn, vsn, kwn, vwn, win_kt, win_vt, gates, o_cmp, t5_table,
      ks_pool, vs_pool)


def _pad_last(a, n):
    return jnp.pad(a, [(0, 0)] * (a.ndim - 1) + [(0, n - a.shape[-1])])


def _swap_halves(a):
    h = a.shape[-1] // 2
    return jnp.concatenate([a[..., h:], a[..., :h]], axis=-1)


def _split_cols(a, sizes):
    out, start = [], 0
    for s in sizes:
        out.append(a[..., start:start + s])
        start += s
    return out


def _block_diag(blocks):
    n, r, c = blocks.shape
    eye = jnp.eye(n, dtype=blocks.dtype)
    return (blocks[:, :, None, :] * eye[:, None, :, None]).reshape(n * r, n * c)


def _rope_tables(pos):
    half = MLA_ROPE // 2
    inv = ROPE_THETA ** (-jnp.arange(half, dtype=F32) / half)
    ang = pos.astype(F32)[:, None] * inv[None, :]
    cos, sin = jnp.cos(ang), jnp.sin(ang)
    z = jnp.zeros((pos.shape[0], LANES - MLA_ROPE), F32)
    return jnp.concatenate([cos, cos, z], axis=1), jnp.concatenate([-sin, sin, z], axis=1)


def _overlap_matrix(ncmp, nsel, ncols):
    c0 = np.arange(ncmp)[:, None] * CMP_STRIDE
    s0 = np.arange(ncols)[None, :] * SEL_BLOCK
    ov = np.clip(np.minimum(c0 + CMP_BLOCK, s0 + SEL_BLOCK) - np.maximum(c0, s0), 0, None) / CMP_BLOCK
    ov = np.where(np.arange(ncols)[None, :] < nsel, ov, 0.0)
    return jnp.asarray(ov, dtype=BF)


def kernel(x_prompt, x_sample, cache_mla_ckv, cache_mla_krope, cache_fox_k, cache_fox_v, cache_fox_logf,
           cache_nsa_kcmp, cache_nsa_vcmp, cache_nsa_kslc, cache_nsa_vslc, state_nsa_kwin, state_nsa_vwin,
           page_table, norm_attn, norm_ffn, norm_final, ab_w_in, ab_fox_bf, ab_mla_gq, ab_mla_gkv,
           ab_mla_wuq, ab_mla_wuk, ab_mla_wuv, ab_w_out, c_w_in, c_pe_k, c_w1_k, c_w2_k, c_pe_v, c_w1_v,
           c_w2_v, c_w_out, t5_table, ffn_w_gate, ffn_w_up, ffn_w_down):
    bsz, t, d = x_prompt.shape
    db, t_s, _ = x_sample.shape
    depth = norm_attn.shape[0]
    n_pool, page = cache_mla_ckv.shape[1], cache_mla_ckv.shape[2]
    n_pages = page_table.shape[1]
    past_len = n_pages * page
    q_rank, mla_h, _ = ab_mla_wuq.shape[1:]
    kv_rank = ab_mla_wuk.shape[1]
    mla_v = ab_mla_wuv.shape[3]
    fox_h = ab_fox_bf.shape[1]
    fox_d = cache_fox_k.shape[4]
    nkv, hd = cache_nsa_kcmp.shape[3], cache_nsa_kcmp.shape[4]
    nsa_h = t5_table.shape[1]
    grp = nsa_h // nkv
    nbuf = state_nsa_kwin.shape[2]
    assert t_s == 1 and (q_rank, kv_rank, fox_h * fox_d) == (256, 128, 512)
    assert past_len % CMP_STRIDE == 0 and nbuf <= WINDOW and past_len >= nbuf
    assert t % 256 == 0 and t >= WINDOW and LANES * CMP_STRIDE == t
    mla_scale = (MLA_NOPE + MLA_ROPE) ** -0.5
    nsa_scale = hd ** -0.5
    m_p = bsz * t
    tm = 512

    pos_p = jnp.arange(t)
    pos_s = jnp.tile(past_len + jnp.arange(t_s), db)
    cos_p, sin_p = _rope_tables(pos_p)
    cos_s, sin_s = _rope_tables(pos_s)

    xp = x_prompt.reshape(m_p, d)
    xs = x_sample.reshape(db * t_s, d)
    tiles, bias_c = _t5_km(t5_table, t, t // CMP_STRIDE)
    ov_p = _overlap_matrix(t // CMP_STRIDE, t // SEL_BLOCK, LANES).T
    nsel_s = -(-(past_len + t_s) // SEL_BLOCK)
    ov_s = _overlap_matrix(past_len // CMP_STRIDE, nsel_s, -(-nsel_s // LANES) * LANES)
    tconst = t5_table[T5_BUCKETS - 1]

    def feat_major(a):
        a = jnp.moveaxis(a, 2, -1)
        return a.reshape(a.shape[0], a.shape[1], -1, a.shape[-1])

    krt_pool = feat_major(cache_mla_krope)
    fkt_pool, fvt_pool, lft_pool = feat_major(cache_fox_k), feat_major(cache_fox_v), feat_major(cache_fox_logf)
    kc_pool, vc_pool = feat_major(cache_nsa_kcmp), feat_major(cache_nsa_vcmp)
    ks_pool, vs_pool = feat_major(cache_nsa_kslc), feat_major(cache_nsa_vslc)
    win_kt, win_vt = feat_major(state_nsa_kwin), feat_major(state_nsa_vwin)
    per_page = page // CMP_STRIDE
    perm_np = np.zeros((page, page), np.float32)
    for r_ in range(CMP_STRIDE):
        for n_ in range(per_page):
            perm_np[r_ * per_page + n_, n_ * CMP_STRIDE + r_] = 1.0
    perm = jnp.asarray(perm_np, dtype=BF)
    fox_sel = jnp.eye(fox_h, dtype=F32)[None, :, :, None]
    nsa_sel = jnp.asarray(np.arange(nsa_h)[:, None] // grp == np.arange(nkv)[None, :], dtype=F32)[None, :, :, None]

    ab_p, ab_s, c_p, c_s = [], [], [], []
    for l in range(depth):
        i = l // 2
        g_attn = norm_attn[l][None, :]
        g_ffn = norm_ffn[l][None, :]
        gf = norm_final[None, :] if l == depth - 1 else None
        wg, wu, wd = ffn_w_gate[l].astype(BF), ffn_w_up[l].astype(BF), ffn_w_down[l].astype(BF)
        if l % 2 == 0:
            cq_w, ckv_w, kr_w, fq_w, fk_w, fv_w, fl_w = _split_cols(
                ab_w_in[i], (q_rank, kv_rank, MLA_ROPE, fox_h * fox_d, fox_h * fox_d, fox_h * fox_d, fox_h))
            w_in = jnp.concatenate([cq_w, ckv_w, _pad_last(kr_w, LANES), _pad_last(_swap_halves(kr_w), LANES),
                                    fq_w, fk_w, fv_w, _pad_last(fl_w, LANES)], axis=1).astype(BF)
            wuq = ab_mla_wuq[i]
            rope_w = wuq[:, :, MLA_NOPE:]
            wuq_all = jnp.concatenate([
                wuq[:, :, :MLA_NOPE].reshape(q_rank, mla_h * MLA_NOPE),
                _pad_last(rope_w, LANES).reshape(q_rank, mla_h * LANES),
                _pad_last(_swap_halves(rope_w), LANES).reshape(q_rank, mla_h * LANES)], axis=1).astype(BF)
            wuk_bd = _block_diag(jnp.transpose(ab_mla_wuk[i], (1, 2, 0))).astype(BF)
            wuv_bd = _block_diag(jnp.transpose(ab_mla_wuv[i], (1, 0, 2))).astype(BF)
            wuv_all = ab_mla_wuv[i].reshape(kv_rank, mla_h * mla_v).astype(BF)
            w_out = ab_w_out[i].astype(BF)
            gq, gkv, bfox = ab_mla_gq[i][None, :], ab_mla_gkv[i][None, :], ab_fox_bf[i][None, :]

            cq, ckv, kr, kcat, fk, fv, lf, fqh, fkh, fvt, ckvt = _ab_proj_prompt(
                xp, g_attn, w_in, gkv, bfox, cos_p, sin_p, bsz, t, tm, fox_d ** -0.5 * LOG2E)
            q_hm = _mla_q_prompt(cq, gq, wuq_all, wuk_bd, cos_p, sin_p, bsz, t, tm, mla_h, mla_scale * LOG2E)
            o_mla = _mla_attn_km(q_hm, kcat.reshape(bsz, t, -1), ckvt, wuv_bd)
            lf3 = lf.reshape(bsz, t, fox_h)
            qx, kx = _fox_bias_cols(jnp.transpose(lf3, (0, 2, 1)), LANES - fox_d)
            o_fox = _fox_attn_km(jnp.concatenate([fqh, qx], axis=-1), jnp.concatenate([fkh, kx], axis=-1),
                                 fvt, fox_d)
            xp = _post(xp, [o_mla.reshape(m_p, -1), o_fox.reshape(m_p, -1)], w_out, g_ffn, wg, wu, wd, gf, tm)
            hm4 = lambda a: jnp.transpose(a.reshape(bsz, fox_h, fox_d, t), (0, 3, 1, 2))
            ab_p.append((ckv.reshape(bsz, t, -1), kr.reshape(bsz, t, -1), hm4(fk), hm4(fv), lf3))

            ms = db * t_s
            cq, ckv, kr, kcat, fq, fk, fv, fkb, fvb, lf = _ab_proj(xs, g_attn, w_in, gkv, bfox, cos_s, sin_s, ms)
            qcat = _mla_q(cq, gq, wuq_all, wuk_bd, cos_s, sin_s, ms, mla_h)
            o_mla = _mla_dec2(page_table, qcat.reshape(db, mla_h, 256).astype(F32),
                              kcat.reshape(db, 1, 256).astype(F32), wuv_all, cache_mla_ckv, krt_pool,
                              i, mla_h, mla_scale)
            qblk = (fox_sel * fq.reshape(db, 1, fox_h, fox_d).astype(F32)).reshape(db, fox_h, fox_h * fox_d)
            o_fox = _fox_dec2(page_table, qblk, fk.reshape(db, 1, -1), fv.reshape(db, 1, -1),
                              lf.reshape(db, fox_h, 1), fkt_pool, fvt_pool, lft_pool, i, fox_h, fox_d)
            xs = _post(xs, [o_mla.reshape(ms, -1), o_fox.reshape(ms, -1)], w_out, g_ffn, wg, wu, wd, gf, ms)
            ab_s.append((ckv.reshape(db, t_s, -1), kr.reshape(db, t_s, -1), fk.reshape(db, t_s, fox_h, fox_d),
                         fv.reshape(db, t_s, fox_h, fox_d), lf.reshape(db, t_s, fox_h)))
        else:
            nq, nk = nsa_h * hd, nkv * hd
            parts = _split_cols(c_w_in[i], (nq,) + (nk,) * 6 + (3 * nsa_h,))
            w_in = jnp.concatenate(parts[:7] + [_pad_last(parts[7], LANES)], axis=1).astype(BF)
            w_out = c_w_out[i].astype(BF)
            eye = jnp.eye(nkv, dtype=F32)

            def cmp_weights(pe, w1, w2):
                w1r = w1.reshape(2, CMP_STRIDE, hd, -1)
                w_dec = jnp.concatenate([w1r[0], w1r[1]], axis=-1).astype(BF)
                rep = lambda m: _block_diag(jnp.broadcast_to(m, (nkv,) + m.shape))
                kron = jax.vmap(rep)
                w_bd = jnp.concatenate([kron(w1r[0]), kron(w1r[1])], axis=-1).astype(BF)
                pe_flat = jnp.pad(pe.reshape(1, -1), ((0, 7), (0, 0))).astype(BF)
                return (w_dec, w_bd, pe_flat, w1.astype(BF), jnp.tile(w1, (1, nkv)).astype(BF),
                        w2.astype(BF), rep(w2).astype(BF))

            kw_ = cmp_weights(c_pe_k[i], c_w1_k[i], c_w2_k[i])
            vw_ = cmp_weights(c_pe_v[i], c_w1_v[i], c_w2_v[i])

            q, kc, vc, kct, vct, kst, vst32, kwt, vwt32, ksh, kwh, vst, vwt, gatest = _c_proj_prompt(
                xp, g_attn, w_in, bsz, t, tm, nsa_h, nkv, hd, nsa_scale * LOG2E)
            r3 = lambda a: a.reshape(bsz, t, -1)
            kcmp, vcmpt = _compress_prompt(r3(kc), r3(vc), kw_[1], vw_[1], kw_[2], vw_[2], kw_[4], vw_[4],
                                           kw_[6], vw_[6], nkv)
            o = _nsa_attn_km(q, kcmp, vcmpt, ksh, vst, kwh, vwt, gatest, bias_c, tiles, ov_p, nkv, grp, hd)
            xp = _post(xp, [o.reshape(m_p, -1)], w_out, g_ffn, wg, wu, wd, gf, tm)
            r4 = lambda a: jnp.transpose(a.reshape(bsz, nkv, hd, a.shape[-1]), (0, 3, 1, 2))
            keep = min(WINDOW, t)
            c_p.append((r4(kct), r4(vct), r4(kst), r4(vst32), r4(kwt[:, :, t - keep:]), r4(vwt32[:, :, t - keep:])))

            ms = db * t_s
            q, kc, vc, ks, vs, kw, vw, ksb, vsb, kwb, vwb, gates = _c_proj(xs, g_attn, w_in, ms, nq, nk, nsa_scale)
            qblk = (nsa_sel * q.reshape(db, nsa_h, 1, hd).astype(F32)).reshape(db, nsa_h, nkv * hd)
            o_cmp, top_i, top_ok = _nsa_cmp_dec(page_table, qblk, perm, kw_[1], vw_[1], kw_[2], vw_[2], kw_[4],
                                                vw_[4], kw_[6], vw_[6], ov_s, t5_table, kc_pool, vc_pool, i,
                                                nkv, grp, past_len, nsel_s)
            ntop = min(SEL_TOP_N, nsel_s)
            flat_top = lambda a: a[:, :, :ntop].reshape(db, nkv * ntop)
            r1 = lambda a: a.reshape(db, 1, -1)
            o_wide = _nsa_sel_win_dec(page_table, flat_top(top_i), flat_top(top_ok), qblk, r1(ks), r1(vs), r1(kw),
                                      r1(vw), win_kt, win_vt, gates[:, :3 * nsa_h].reshape(db, nsa_h, 3), o_cmp,
                                      t5_table, ks_pool, vs_pool, i, nkv, grp, past_len, past_len // SEL_BLOCK)
            o = jnp.stack([o_wide[:, k * grp:(k + 1) * grp, k * hd:(k + 1) * hd] for k in range(nkv)], axis=1)
            xs = _post(xs, [o.reshape(ms, -1).astype(BF)], w_out, g_ffn, wg, wu, wd, gf, ms)
            r4 = lambda a: a.reshape(db, t_s, nkv, hd)
            c_s.append((r4(kc), r4(vc), r4(ks), r4(vs),
                        jnp.concatenate([state_nsa_kwin[i], r4(kw)], axis=1)[:, t_s:],
                        jnp.concatenate([state_nsa_vwin[i], r4(vw)], axis=1)[:, t_s:]))

    def stack(rows, j):
        return jnp.stack([r[j] for r in rows])

    return (xp.reshape(bsz, t, d), xs.reshape(db, t_s, d),
            stack(ab_p, 0), stack(ab_s, 0), stack(ab_p, 1), stack(ab_s, 1),
            stack(ab_p, 2), stack(ab_s, 2), stack(ab_p, 3), stack(ab_s, 3), stack(ab_p, 4), stack(ab_s, 4),
            stack(c_p, 0), stack(c_s, 0), stack(c_p, 1), stack(c_s, 1), stack(c_p, 2), stack(c_s, 2),
            stack(c_p, 3), stack(c_s, 3), stack(c_p, 4), stack(c_s, 4), stack(c_p, 5), stack(c_s, 5))
```

```python
import functools
import math

import numpy as np
import jax
import jax.numpy as jnp
from jax import lax
from jax.experimental import pallas as pl
from jax.experimental.pallas import tpu as pltpu

F32 = jnp.float32
BF = jnp.bfloat16
I32 = jnp.int32

MLA_NOPE = 64
MLA_ROPE = 32
ROPE_THETA = 10000.0
CMP_STRIDE = 16
CMP_BLOCK = 32
SEL_BLOCK = 64
SEL_TOP_N = 16
WINDOW = 512
T5_BUCKETS = 32
T5_MAX_DIST = 128
EPS = 1e-6
FORCE_SCORE = 1e4
NEG_SCORE = -1e9
NEG_BIG = -1e30
LANES = 128

VMEM_LIMIT = 56 * 1024 * 1024


def _params(sem, vmem=VMEM_LIMIT):
    return pltpu.CompilerParams(dimension_semantics=sem, vmem_limit_bytes=vmem)


def _const_spec(shape):
    nd = len(shape)
    return pl.BlockSpec(shape, lambda *a: (0,) * nd)


def _rms(x, g):
    ms = jnp.mean(x * x, axis=-1, keepdims=True)
    return x * lax.rsqrt(ms + EPS) * g


def _dot(a, b):
    return jnp.dot(a, b, preferred_element_type=F32)


def _dot_nt(a, b):
    return lax.dot_general(a, b, (((1,), (1,)), ((), ())), preferred_element_type=F32)


def _log_sigmoid(x):
    return jnp.minimum(x, 0.0) - jnp.log(1.0 + jnp.exp(-jnp.abs(x)))


def _sigmoid(x):
    return 1.0 / (1.0 + jnp.exp(-x))


def _t5_thresholds():
    exact = T5_BUCKETS // 2
    d = np.arange(0, T5_MAX_DIST + 1)
    lr = np.log(np.maximum(d, 1).astype(np.float64) / exact) / math.log(T5_MAX_DIST / exact)
    large = np.minimum(exact + (lr * (T5_BUCKETS - exact)).astype(np.int64), T5_BUCKETS - 1)
    bucket = np.where(d < exact, d, large)
    return [int(np.argmax(bucket >= j)) for j in range(T5_BUCKETS)]


_T5_THR = _t5_thresholds()


def _t5_bias(dist, values):
    out = jnp.zeros(dist.shape, F32) + values(0)
    for j in range(1, T5_BUCKETS):
        out = jnp.where(dist >= _T5_THR[j], values(j), out)
    return out


def _ab_proj_kernel(x_ref, g_ref, w_ref, gkv_ref, bf_ref, cos_ref, sin_ref,
                    cq_ref, ckv_ref, kr_ref, kcat_ref, fq_ref, fk_ref, fv_ref,
                    fkb_ref, fvb_ref, lf_ref, *, fox_scale, nfh):
    h = _rms(x_ref[...], g_ref[...]).astype(BF)

    def mm(c0, c1):
        return _dot(h, w_ref[:, c0:c1])

    cq_ref[...] = mm(0, 256)
    ckv = _rms(mm(256, 384), gkv_ref[...])
    ckv_ref[...] = ckv
    kr = mm(384, 512) * cos_ref[...] + mm(512, 640) * sin_ref[...]
    kr_ref[...] = kr[:, :MLA_ROPE]
    kcat_ref[:, 0:128] = ckv.astype(BF)
    kcat_ref[:, 128:256] = kr.astype(BF)
    fq_ref[...] = (mm(640, 1152) * fox_scale).astype(BF)
    fk = mm(1152, 1664)
    fk_ref[...] = fk
    fkb_ref[...] = fk.astype(BF)
    fv = mm(1664, 2176)
    fv_ref[...] = fv
    fvb_ref[...] = fv.astype(BF)
    fl = mm(2176, 2304)[:, :nfh] + bf_ref[...]
    lf_ref[...] = _log_sigmoid(fl)


def _ab_proj_prompt_kernel(x_ref, g_ref, w_ref, gkv_ref, bf_ref, cos_ref, sin_ref,
                           cq_ref, ckv_ref, kr_ref, kcat_ref, fk_ref, fv_ref, lf_ref,
                           fqh_ref, fkh_ref, fvt_ref, ckvt_ref, *, q_scale, nfh, hd):
    h = _rms(x_ref[...], g_ref[...]).astype(BF)

    def mm(c0, c1):
        return _dot(h, w_ref[:, c0:c1])

    cq_ref[...] = mm(0, 256)
    ckv = _rms(mm(256, 384), gkv_ref[...])
    ckv_ref[...] = ckv
    kr = mm(384, 512) * cos_ref[...] + mm(512, 640) * sin_ref[...]
    kr_ref[...] = kr[:, :MLA_ROPE]
    kcat_ref[:, 0:128] = ckv.astype(BF)
    kcat_ref[:, 128:256] = kr.astype(BF)
    fq = mm(640, 1152) * q_scale
    fk = mm(1152, 1664)
    fk_ref[0] = fk.T
    fvt = mm(1664, 2176).T
    fv_ref[0] = fvt
    for i in range(nfh):
        fqh_ref[0, i] = fq[:, i * hd:(i + 1) * hd].astype(BF)
        fkh_ref[0, i] = fk[:, i * hd:(i + 1) * hd].astype(BF)
    fvt_ref[0] = fvt.astype(BF)
    ckvt_ref[0] = ckv.T.astype(BF)
    fl = mm(2176, 2304)[:, :nfh] + bf_ref[...]
    lf_ref[...] = _log_sigmoid(fl)


def _ab_proj_prompt(x, g, w, gkv, bf, cos_t, sin_t, bsz, t, tm, q_scale):
    m, d = x.shape
    nt = t // tm
    nfh = bf.shape[1]
    fd = 512
    hd = fd // nfh
    row = lambda wd: pl.BlockSpec((tm, wd), lambda i: (i, 0))
    tab = pl.BlockSpec((tm, LANES), lambda i: (i % nt, 0))
    hm = pl.BlockSpec((1, nfh, tm, hd), lambda i: (i // nt, 0, i % nt, 0))
    fm = lambda n: pl.BlockSpec((1, n, tm), lambda i: (i // nt, 0, i % nt))
    out_specs = [row(256), row(128), row(MLA_ROPE), row(256), fm(fd), fm(fd), row(nfh), hm, hm, fm(fd), fm(128)]
    out_shape = ([jax.ShapeDtypeStruct((m, 256), F32), jax.ShapeDtypeStruct((m, 128), F32),
                  jax.ShapeDtypeStruct((m, MLA_ROPE), F32), jax.ShapeDtypeStruct((m, 256), BF),
                  jax.ShapeDtypeStruct((bsz, fd, t), F32), jax.ShapeDtypeStruct((bsz, fd, t), F32),
                  jax.ShapeDtypeStruct((m, nfh), F32),
                  jax.ShapeDtypeStruct((bsz, nfh, t, hd), BF), jax.ShapeDtypeStruct((bsz, nfh, t, hd), BF),
                  jax.ShapeDtypeStruct((bsz, fd, t), BF), jax.ShapeDtypeStruct((bsz, 128, t), BF)])
    return pl.pallas_call(
        functools.partial(_ab_proj_prompt_kernel, q_scale=q_scale, nfh=nfh, hd=hd),
        grid=(m // tm,),
        in_specs=[row(d), _const_spec(g.shape), _const_spec(w.shape), _const_spec(gkv.shape),
                  _const_spec(bf.shape), tab, tab],
        out_specs=out_specs, out_shape=out_shape,
        compiler_params=_params(("parallel",)),
        name="ab_proj_prompt",
    )(x, g, w, gkv, bf, cos_t, sin_t)


def _fox_km_kernel(q_ref, k_ref, vt_ref, mask_ref, o_ref, *, nh, hd, tq, tk):
    qi = pl.program_id(1)
    qs = [q_ref[0, h] for h in range(nh)]

    def tile(h, ki, carry, bias):
        m_p, l_p, acc = carry
        start = pl.multiple_of(ki * tk, tk)
        st = _dot_nt(k_ref[0, h, pl.ds(start, tk), :], qs[h])
        if bias is not None:
            st = st + bias
        m_n = jnp.maximum(m_p, jnp.max(st, axis=0, keepdims=True))
        p = jnp.exp2(st - m_n)
        alpha = jnp.exp2(m_p - m_n)
        l_n = alpha * l_p + jnp.sum(p, axis=0, keepdims=True)
        vt = vt_ref[0, h * hd:(h + 1) * hd, pl.ds(start, tk)]
        return m_n, l_n, alpha * acc + _dot(vt, p.astype(BF))

    init = (jnp.full((1, tq), NEG_BIG, F32), jnp.zeros((1, tq), F32), jnp.zeros((hd, tq), F32))
    cs = lax.fori_loop(0, qi, lambda ki, c: tuple(tile(h, ki, c[h], None) for h in range(nh)), (init,) * nh)
    for h in range(nh):
        _, l_f, acc = tile(h, qi, cs[h], mask_ref[...])
        o_ref[0, :, h * hd:(h + 1) * hd] = (acc / l_f).T.astype(BF)


def _fox_attn_km(q_aug, k_aug, vt, hd, tq=1024):
    b, nh, t, w = q_aug.shape
    mask = jnp.asarray(np.where(np.arange(tq)[:, None] <= np.arange(tq)[None, :], 0.0, NEG_BIG), dtype=F32)
    full_b = lambda a: pl.BlockSpec((1,) + a.shape[1:], lambda bi, qi: (bi,) + (0,) * (a.ndim - 1))
    return pl.pallas_call(
        functools.partial(_fox_km_kernel, nh=nh, hd=hd, tq=tq, tk=tq),
        grid=(b, t // tq),
        in_specs=[pl.BlockSpec((1, nh, tq, w), lambda bi, qi: (bi, 0, qi, 0)), full_b(k_aug), full_b(vt),
                  _const_spec(mask.shape)],
        out_specs=pl.BlockSpec((1, tq, nh * hd), lambda bi, qi: (bi, qi, 0)),
        out_shape=jax.ShapeDtypeStruct((b, t, nh * hd), BF),
        compiler_params=_params(("parallel", "arbitrary")),
        name="fox_attn",
    )(q_aug, k_aug, vt, mask)


def _mla_q_prompt_kernel(cq_ref, gq_ref, wuq_ref, wuk_ref, cos_ref, sin_ref, q_ref, *, nh, q_scale):
    cqn = _rms(cq_ref[...], gq_ref[...]).astype(BF)
    qn = _dot(cqn, wuq_ref[:, 0:nh * MLA_NOPE]).astype(BF)
    qlat = _dot(qn, wuk_ref[...]) * q_scale
    r0 = nh * MLA_NOPE
    r1 = r0 + nh * LANES
    cos = cos_ref[...] * q_scale
    sin = sin_ref[...] * q_scale
    for h in range(nh):
        qr = (_dot(cqn, wuq_ref[:, r0 + h * LANES:r0 + (h + 1) * LANES]) * cos
              + _dot(cqn, wuq_ref[:, r1 + h * LANES:r1 + (h + 1) * LANES]) * sin)
        q_ref[0, h, :, 0:128] = qlat[:, 128 * h:128 * (h + 1)].astype(BF)
        q_ref[0, h, :, 128:256] = qr.astype(BF)


def _mla_q_prompt(cq, gq, wuq, wukbd, cos_t, sin_t, bsz, t, tm, nh, q_scale):
    nt = t // tm
    tab = pl.BlockSpec((tm, LANES), lambda i: (i % nt, 0))
    return pl.pallas_call(
        functools.partial(_mla_q_prompt_kernel, nh=nh, q_scale=q_scale),
        grid=(cq.shape[0] // tm,),
        in_specs=[pl.BlockSpec((tm, cq.shape[1]), lambda i: (i, 0)), _const_spec(gq.shape),
                  _const_spec(wuq.shape), _const_spec(wukbd.shape), tab, tab],
        out_specs=pl.BlockSpec((1, nh, tm, 256), lambda i: (i // nt, 0, i % nt, 0)),
        out_shape=jax.ShapeDtypeStruct((bsz, nh, t, 256), BF),
        compiler_params=_params(("parallel",)),
        name="mla_q_prompt",
    )(cq, gq, wuq, wukbd, cos_t, sin_t)


def _mla_km_kernel(q_ref, k_ref, vt_ref, mask_ref, wuv_ref, o_ref, *, nh, tq, tk, nchain):
    qi = pl.program_id(1)
    last = qi // (tk // tq)
    odd = qi % (tk // tq)
    hpc = nh // nchain
    qs = [q_ref[0, c * hpc:(c + 1) * hpc].reshape(hpc * tq, q_ref.shape[3]) for c in range(nchain)]
    rank = vt_ref.shape[1]

    def tile(c, ki, carry, bias):
        m_p, l_p, acc = carry
        start = pl.multiple_of(ki * tk, tk)
        st = _dot_nt(k_ref[0, pl.ds(start, tk), :], qs[c])
        if bias is not None:
            st = st + bias
        m_n = jnp.maximum(m_p, jnp.max(st, axis=0, keepdims=True))
        p = jnp.exp2(st - m_n)
        alpha = jnp.exp2(m_p - m_n)
        l_n = alpha * l_p + jnp.sum(p, axis=0, keepdims=True)
        return m_n, l_n, alpha * acc + _dot(vt_ref[0, :, pl.ds(start, tk)], p.astype(BF))

    w = hpc * tq
    init = (jnp.full((1, w), NEG_BIG, F32), jnp.zeros((1, w), F32), jnp.zeros((rank, w), F32))
    cs = lax.fori_loop(0, last, lambda ki, c: tuple(tile(j, ki, c[j], None) for j in range(nchain)),
                       (init,) * nchain)
    mask = jnp.concatenate([mask_ref[odd]] * hpc, axis=1)
    olat = []
    for c in range(nchain):
        _, l_f, acc = tile(c, last, cs[c], mask)
        o_t = acc / l_f
        for j in range(hpc):
            olat.append(o_t[:, j * tq:(j + 1) * tq].T.astype(BF))
    o_ref[0] = _dot(jnp.concatenate(olat, axis=1), wuv_ref[...]).astype(BF)


def _mla_attn_km(q_hm, kcat, ckvt, wuvbd, tq=512, tk=512, nchain=4):
    b, nh, t, _ = q_hm.shape
    per = tk // tq
    d = np.arange(tk)[None, :, None] <= (np.arange(per)[:, None, None] * tq + np.arange(tq)[None, None, :])
    mask = jnp.asarray(np.where(d, 0.0, NEG_BIG), dtype=F32)
    full_b = lambda a: pl.BlockSpec((1,) + a.shape[1:], lambda bi, qi: (bi,) + (0,) * (a.ndim - 1))
    return pl.pallas_call(
        functools.partial(_mla_km_kernel, nh=nh, tq=tq, tk=tk, nchain=nchain),
        grid=(b, t // tq),
        in_specs=[pl.BlockSpec((1, nh, tq, q_hm.shape[3]), lambda bi, qi: (bi, 0, qi, 0)),
                  full_b(kcat), full_b(ckvt), _const_spec(mask.shape), _const_spec(wuvbd.shape)],
        out_specs=pl.BlockSpec((1, tq, wuvbd.shape[1]), lambda bi, qi: (bi, qi, 0)),
        out_shape=jax.ShapeDtypeStruct((b, t, wuvbd.shape[1]), BF),
        compiler_params=_params(("parallel", "arbitrary")),
        name="mla_attn",
    )(q_hm, kcat, ckvt, mask, wuvbd)


def _fox_bias_cols_kernel(lf_ref, qx_ref, kx_ref, *, nh, width):
    x = lf_ref[0]
    n = x.shape[1]
    lane = lax.broadcasted_iota(I32, x.shape, 1)
    k = 1
    while k < n:
        x = x + jnp.where(lane >= k, pltpu.roll(x, k, axis=1), 0.0)
        k *= 2
    f2 = x * LOG2E
    hi = _bf_round(f2)
    mid = _bf_round(f2 - hi)
    lo = _bf_round(f2 - hi - mid)
    ones = jnp.ones((1, n), F32)
    zeros = jnp.zeros((2, n), F32)
    pad = jnp.zeros((n, width - 8), BF)
    for h in range(nh):
        pieces = [a[h:h + 1, :] for a in (hi, mid, lo)]
        qx_ref[0, h, :, 0:8] = jnp.concatenate([ones, ones, ones] + pieces + [zeros], axis=0).T.astype(BF)
        kx_ref[0, h, :, 0:8] = jnp.concatenate([-p for p in pieces] + [ones, ones, ones, zeros], axis=0).T.astype(BF)
        qx_ref[0, h, :, 8:width] = pad
        kx_ref[0, h, :, 8:width] = pad


def _fox_bias_cols(lf_t, width):
    b, nh, t = lf_t.shape
    blk = pl.BlockSpec((1, nh, t, width), lambda i: (i, 0, 0, 0))
    return pl.pallas_call(
        functools.partial(_fox_bias_cols_kernel, nh=nh, width=width), grid=(b,),
        in_specs=[pl.BlockSpec((1, nh, t), lambda i: (i, 0, 0))],
        out_specs=[blk, blk],
        out_shape=[jax.ShapeDtypeStruct((b, nh, t, width), BF)] * 2,
        compiler_params=_params(("parallel",)),
        name="fox_bias_cols",
    )(lf_t)


def _ab_proj(x, g, w, gkv, bf, cos_t, sin_t, tm):
    m, d = x.shape
    nt = cos_t.shape[0] // tm
    nfh = bf.shape[1]
    fd = 512
    row = lambda wd: pl.BlockSpec((tm, wd), lambda i: (i, 0))
    tab = pl.BlockSpec((tm, LANES), lambda i: (i % nt, 0))
    outs = [((m, 256), F32), ((m, 128), F32), ((m, MLA_ROPE), F32), ((m, 256), BF),
            ((m, fd), BF), ((m, fd), F32), ((m, fd), F32), ((m, fd), BF), ((m, fd), BF),
            ((m, nfh), F32)]
    return pl.pallas_call(
        functools.partial(_ab_proj_kernel, fox_scale=(fd // nfh) ** -0.5, nfh=nfh),
        grid=(m // tm,),
        in_specs=[row(d), _const_spec(g.shape), _const_spec(w.shape), _const_spec(gkv.shape),
                  _const_spec(bf.shape), tab, tab],
        out_specs=[row(s[1]) for s, _ in outs],
        out_shape=[jax.ShapeDtypeStruct(s, dt) for s, dt in outs],
        compiler_params=_params(("parallel",)),
        name="ab_proj",
    )(x, g, w, gkv, bf, cos_t, sin_t)


def _mla_q_kernel(cq_ref, gq_ref, wuq_ref, wuk_ref, cos_ref, sin_ref, q_ref, *, nh):
    cqn = _rms(cq_ref[...], gq_ref[...]).astype(BF)
    qn = _dot(cqn, wuq_ref[:, 0:nh * MLA_NOPE]).astype(BF)
    qlat = _dot(qn, wuk_ref[...])
    r0 = nh * MLA_NOPE
    r1 = r0 + nh * LANES
    cos = cos_ref[...]
    sin = sin_ref[...]
    for h in range(nh):
        qr = (_dot(cqn, wuq_ref[:, r0 + h * LANES:r0 + (h + 1) * LANES]) * cos
              + _dot(cqn, wuq_ref[:, r1 + h * LANES:r1 + (h + 1) * LANES]) * sin)
        q_ref[:, 256 * h:256 * h + 128] = qlat[:, 128 * h:128 * (h + 1)].astype(BF)
        q_ref[:, 256 * h + 128:256 * (h + 1)] = qr.astype(BF)


def _mla_q(cq, gq, wuq, wukbd, cos_t, sin_t, tm, nh):
    m = cq.shape[0]
    nt = cos_t.shape[0] // tm
    tab = pl.BlockSpec((tm, LANES), lambda i: (i % nt, 0))
    return pl.pallas_call(
        functools.partial(_mla_q_kernel, nh=nh),
        grid=(m // tm,),
        in_specs=[pl.BlockSpec((tm, cq.shape[1]), lambda i: (i, 0)), _const_spec(gq.shape),
                  _const_spec(wuq.shape), _const_spec(wukbd.shape), tab, tab],
        out_specs=pl.BlockSpec((tm, 256 * nh), lambda i: (i, 0)),
        out_shape=jax.ShapeDtypeStruct((m, 256 * nh), BF),
        compiler_params=_params(("parallel",)),
        name="mla_q",
    )(cq, gq, wuq, wukbd, cos_t, sin_t)


def _softmax_step(s, m_prev, l_prev):
    m_new = jnp.maximum(m_prev, jnp.max(s, axis=-1, keepdims=True))
    alpha = jnp.exp(m_prev - m_new)
    p = jnp.exp(s - m_new)
    l_new = alpha * l_prev + jnp.sum(p, axis=-1, keepdims=True)
    return p, alpha, m_new, l_new


def _mla_attn_kernel(q_ref, k_ref, wuv_ref, o_ref, m_ref, l_ref, acc_ref, *, tq, tk, nh, scale):
    qi = pl.program_id(1)
    ki = pl.program_id(2)
    nk = pl.num_programs(2)

    @pl.when(ki == 0)
    def _():
        m_ref[...] = jnp.full(m_ref.shape, NEG_BIG, F32)
        l_ref[...] = jnp.zeros(l_ref.shape, F32)
        acc_ref[...] = jnp.zeros(acc_ref.shape, F32)

    @pl.when(ki * tk <= qi * tq + tq - 1)
    def _():
        k = k_ref[0]
        v = k[:, 0:128]
        rows = qi * tq + lax.broadcasted_iota(I32, (tq, tk), 0)
        cols = ki * tk + lax.broadcasted_iota(I32, (tq, tk), 1)
        mask = cols <= rows
        for h in range(nh):
            q = q_ref[0, :, 256 * h:256 * (h + 1)]
            s = jnp.where(mask, _dot_nt(q, k) * scale, NEG_BIG)
            p, alpha, m_new, l_new = _softmax_step(s, m_ref[h], l_ref[h])
            acc_ref[h] = alpha * acc_ref[h] + _dot(p.astype(BF), v)
            m_ref[h] = m_new
            l_ref[h] = l_new

    @pl.when(ki == nk - 1)
    def _():
        olat = jnp.concatenate([(acc_ref[h] / l_ref[h]).astype(BF) for h in range(nh)], axis=1)
        o_ref[0] = _dot(olat, wuv_ref[...]).astype(BF)


def _mla_attn(qcat, kcat, wuvbd, nh, scale, tq=256, tk=256):
    b, t, _ = qcat.shape
    nq, nk = t // tq, t // tk
    kmap = lambda bi, qi, ki: (bi, jnp.minimum(ki, (qi * tq + tq - 1) // tk), 0)
    return pl.pallas_call(
        functools.partial(_mla_attn_kernel, tq=tq, tk=tk, nh=nh, scale=scale),
        grid=(b, nq, nk),
        in_specs=[pl.BlockSpec((1, tq, 256 * nh), lambda bi, qi, ki: (bi, qi, 0)),
                  pl.BlockSpec((1, tk, 256), kmap), _const_spec(wuvbd.shape)],
        out_specs=pl.BlockSpec((1, tq, wuvbd.shape[1]), lambda bi, qi, ki: (bi, qi, 0)),
        out_shape=jax.ShapeDtypeStruct((b, t, wuvbd.shape[1]), BF),
        scratch_shapes=[pltpu.VMEM((nh, tq, 1), F32), pltpu.VMEM((nh, tq, 1), F32),
                        pltpu.VMEM((nh, tq, 128), F32)],
        compiler_params=_params(("parallel", "parallel", "arbitrary")),
        name="mla_attn",
    )(qcat, kcat, wuvbd)


def _cumsum_kernel(x_ref, o_ref):
    x = x_ref[0]
    n = x.shape[1]
    lane = lax.broadcasted_iota(I32, x.shape, 1)
    k = 1
    while k < n:
        x = x + jnp.where(lane >= k, pltpu.roll(x, k, axis=1), 0.0)
        k *= 2
    o_ref[0] = x


def _cumsum_lanes(x):
    b, h, t = x.shape
    return pl.pallas_call(
        _cumsum_kernel, grid=(b,),
        in_specs=[pl.BlockSpec((1, h, t), lambda i: (i, 0, 0))],
        out_specs=pl.BlockSpec((1, h, t), lambda i: (i, 0, 0)),
        out_shape=jax.ShapeDtypeStruct((b, h, t), F32),
        compiler_params=_params(("parallel",)),
        name="fox_cumsum",
    )(x)


def _fox_attn_kernel(q_ref, k_ref, v_ref, f_ref, ft_ref, o_ref, m_ref, l_ref, acc_ref, *, tq, tk, nh):
    qi = pl.program_id(1)
    ki = pl.program_id(2)
    nk = pl.num_programs(2)

    @pl.when(ki == 0)
    def _():
        m_ref[...] = jnp.full(m_ref.shape, NEG_BIG, F32)
        l_ref[...] = jnp.zeros(l_ref.shape, F32)
        acc_ref[...] = jnp.zeros(acc_ref.shape, F32)

    @pl.when(ki * tk <= qi * tq + tq - 1)
    def _():
        rows = qi * tq + lax.broadcasted_iota(I32, (tq, tk), 0)
        cols = ki * tk + lax.broadcasted_iota(I32, (tq, tk), 1)
        mask = cols <= rows
        lo = lax.broadcasted_iota(I32, (tq, LANES), 1) < 64
        fq = f_ref[0]
        fk = ft_ref[0]
        for hp in range(nh // 2):
            q2 = q_ref[0, :, 128 * hp:128 * (hp + 1)]
            k2 = k_ref[0, :, 128 * hp:128 * (hp + 1)]
            v2 = v_ref[0, :, 128 * hp:128 * (hp + 1)]
            zero = jnp.zeros_like(q2)
            pvs, alphas = [], []
            for e in range(2):
                h = 2 * hp + e
                qm = jnp.where(lo, q2, zero) if e == 0 else jnp.where(lo, zero, q2)
                s = _dot_nt(qm, k2) + (fq[:, h:h + 1] - fk[h:h + 1, :])
                s = jnp.where(mask, s, NEG_BIG)
                p, alpha, m_new, l_new = _softmax_step(s, m_ref[h], l_ref[h])
                m_ref[h] = m_new
                l_ref[h] = l_new
                pvs.append(_dot(p.astype(BF), v2))
                alphas.append(alpha)
            a2 = jnp.where(lo, alphas[0], alphas[1])
            acc_ref[:, 128 * hp:128 * (hp + 1)] = (a2 * acc_ref[:, 128 * hp:128 * (hp + 1)]
                                                   + jnp.where(lo, pvs[0], pvs[1]))

    @pl.when(ki == nk - 1)
    def _():
        lo = lax.broadcasted_iota(I32, (tq, LANES), 1) < 64
        for hp in range(nh // 2):
            l2 = jnp.where(lo, l_ref[2 * hp], l_ref[2 * hp + 1])
            o_ref[0, :, 128 * hp:128 * (hp + 1)] = (acc_ref[:, 128 * hp:128 * (hp + 1)] / l2).astype(BF)


def _fox_attn(fq, fk, fv, f, ft, nh, tq=256, tk=256):
    b, t, w = fq.shape
    nq, nk = t // tq, t // tk
    kclamp = lambda qi, ki: jnp.minimum(ki, (qi * tq + tq - 1) // tk)
    return pl.pallas_call(
        functools.partial(_fox_attn_kernel, tq=tq, tk=tk, nh=nh),
        grid=(b, nq, nk),
        in_specs=[pl.BlockSpec((1, tq, w), lambda bi, qi, ki: (bi, qi, 0)),
                  pl.BlockSpec((1, tk, w), lambda bi, qi, ki: (bi, kclamp(qi, ki), 0)),
                  pl.BlockSpec((1, tk, w), lambda bi, qi, ki: (bi, kclamp(qi, ki), 0)),
                  pl.BlockSpec((1, tq, nh), lambda bi, qi, ki: (bi, qi, 0)),
                  pl.BlockSpec((1, nh, tk), lambda bi, qi, ki: (bi, 0, kclamp(qi, ki)))],
        out_specs=pl.BlockSpec((1, tq, w), lambda bi, qi, ki: (bi, qi, 0)),
        out_shape=jax.ShapeDtypeStruct((b, t, w), BF),
        scratch_shapes=[pltpu.VMEM((nh, tq, 1), F32), pltpu.VMEM((nh, tq, 1), F32),
                        pltpu.VMEM((tq, w), F32)],
        compiler_params=_params(("parallel", "parallel", "arbitrary")),
        name="fox_attn",
    )(fq, fk, fv, f, ft)


def _post_kernel(*refs, n_o, tf, final):
    x_ref = refs[0]
    o_refs = refs[1:1 + n_o]
    wo_refs = refs[1 + n_o:2 + n_o]
    g_ref, wg_ref, wu_ref, wd_ref = refs[2 + n_o:6 + n_o]
    nxt = 6 + n_o
    if final:
        gf_ref = refs[nxt]
        nxt += 1
    out_ref = refs[nxt]
    o = o_refs[0][...] if n_o == 1 else jnp.concatenate([r[...] for r in o_refs], axis=1)
    x1 = x_ref[...] + _dot(o, wo_refs[0][...])
    h = _rms(x1, g_ref[...]).astype(BF)
    acc = x1
    nf = wg_ref.shape[1] // tf
    for f in range(nf):
        a = _dot(h, wg_ref[:, f * tf:(f + 1) * tf])
        u = _dot(h, wu_ref[:, f * tf:(f + 1) * tf])
        t = (a * _sigmoid(a) * u).astype(BF)
        acc = acc + _dot(t, wd_ref[f * tf:(f + 1) * tf, :])
    if final:
        out_ref[...] = _rms(acc, gf_ref[...])
    else:
        out_ref[...] = acc


def _post(x, os_, wo, g, wg, wu, wd, gf, tm):
    m, d = x.shape
    n_o = len(os_)
    final = gf is not None
    row = lambda wd_: pl.BlockSpec((tm, wd_), lambda i: (i, 0))
    single = lambda a: pl.BlockSpec(a.shape, lambda i: (0,) * a.ndim, pipeline_mode=pl.Buffered(1))
    args = [x] + list(os_) + [wo, g, wg, wu, wd] + ([gf] if final else [])
    in_specs = ([row(d)] + [row(o.shape[1]) for o in os_] + [single(wo)]
                + [_const_spec(g.shape), single(wg), single(wu), single(wd)]
                + ([_const_spec(gf.shape)] if final else []))
    return pl.pallas_call(
        functools.partial(_post_kernel, n_o=n_o, tf=256, final=final),
        grid=(m // tm,),
        in_specs=in_specs,
        out_specs=row(d),
        out_shape=jax.ShapeDtypeStruct((m, d), F32),
        compiler_params=_params(("parallel",)),
        name="post_ffn",
    )(*args)


def _c_proj_kernel(x_ref, g_ref, w_ref, q_ref, kc_ref, vc_ref, ks_ref, vs_ref, kw_ref, vw_ref,
                   ksb_ref, vsb_ref, kwb_ref, vwb_ref, gt_ref, *, scale, nq, nkv):
    h = _rms(x_ref[...], g_ref[...]).astype(BF)

    def mm(c0, c1):
        return _dot(h, w_ref[:, c0:c1])

    q_ref[...] = (mm(0, nq) * scale).astype(BF)
    c = nq
    kc_ref[...] = mm(c, c + nkv)
    vc_ref[...] = mm(c + nkv, c + 2 * nkv)
    ks = mm(c + 2 * nkv, c + 3 * nkv)
    ks_ref[...] = ks
    ksb_ref[...] = ks.astype(BF)
    vs = mm(c + 3 * nkv, c + 4 * nkv)
    vs_ref[...] = vs
    vsb_ref[...] = vs.astype(BF)
    kw = mm(c + 4 * nkv, c + 5 * nkv)
    kw_ref[...] = kw
    kwb_ref[...] = kw.astype(BF)
    vw = mm(c + 5 * nkv, c + 6 * nkv)
    vw_ref[...] = vw
    vwb_ref[...] = vw.astype(BF)
    gt_ref[...] = _sigmoid(mm(c + 6 * nkv, c + 6 * nkv + LANES))


def _c_proj(x, g, w, tm, nq, nkv, scale):
    m, d = x.shape
    row = lambda wd: pl.BlockSpec((tm, wd), lambda i: (i, 0))
    outs = ([((m, nq), BF)] + [((m, nkv), F32)] * 6 + [((m, nkv), BF)] * 4 + [((m, LANES), F32)])
    return pl.pallas_call(
        functools.partial(_c_proj_kernel, scale=scale, nq=nq, nkv=nkv),
        grid=(m // tm,),
        in_specs=[row(d), _const_spec(g.shape), _const_spec(w.shape)],
        out_specs=[row(s[1]) for s, _ in outs],
        out_shape=[jax.ShapeDtypeStruct(s, dt) for s, dt in outs],
        compiler_params=_params(("parallel",)),
        name="c_proj",
    )(x, g, w)


def _c_proj_prompt_kernel(x_ref, g_ref, w_ref, q_ref, kc_ref, vc_ref, kct_ref, vct_ref, kst_ref, vst_ref,
                          kwt_ref, vwt_ref, ksh_ref, kwh_ref, vsb_ref, vwb_ref, gtt_ref,
                          *, scale, nh, nkv, hd):
    h = _rms(x_ref[...], g_ref[...]).astype(BF)
    nq, nk = nh * hd, nkv * hd

    def mm(c0, c1):
        return _dot(h, w_ref[:, c0:c1])

    q = mm(0, nq) * scale
    for i in range(nh):
        q_ref[0, i] = q[:, i * hd:(i + 1) * hd].astype(BF)
    c = nq
    for j, (r_ref, t_ref, h_ref, b_ref) in enumerate((
            (kc_ref, kct_ref, None, None), (vc_ref, vct_ref, None, None),
            (None, kst_ref, ksh_ref, None), (None, vst_ref, None, vsb_ref),
            (None, kwt_ref, kwh_ref, None), (None, vwt_ref, None, vwb_ref))):
        a = mm(c + j * nk, c + (j + 1) * nk)
        at = a.T
        t_ref[0] = at
        if r_ref is not None:
            r_ref[...] = a
        if h_ref is not None:
            for i in range(nkv):
                h_ref[0, i] = a[:, i * hd:(i + 1) * hd].astype(BF)
        if b_ref is not None:
            b_ref[0] = at.astype(BF)
    gtt_ref[0] = _sigmoid(mm(c + 6 * nk, c + 6 * nk + LANES)).T


def _c_proj_prompt(x, g, w, bsz, t, tm, nh, nkv, hd, scale):
    m, d = x.shape
    nq, nk = nh * hd, nkv * hd
    nt = t // tm
    row = lambda wd: pl.BlockSpec((tm, wd), lambda i: (i, 0))
    hm = lambda n: pl.BlockSpec((1, n, tm, hd), lambda i: (i // nt, 0, i % nt, 0))
    fm = lambda n: pl.BlockSpec((1, n, tm), lambda i: (i // nt, 0, i % nt))
    out_specs = [hm(nh)] + [row(nk)] * 2 + [fm(nk)] * 6 + [hm(nkv), hm(nkv), fm(nk), fm(nk), fm(LANES)]
    out_shape = ([jax.ShapeDtypeStruct((bsz, nh, t, hd), BF)] + [jax.ShapeDtypeStruct((m, nk), F32)] * 2
                 + [jax.ShapeDtypeStruct((bsz, nk, t), F32)] * 6
                 + [jax.ShapeDtypeStruct((bsz, nkv, t, hd), BF)] * 2
                 + [jax.ShapeDtypeStruct((bsz, nk, t), BF)] * 2 + [jax.ShapeDtypeStruct((bsz, LANES, t), F32)])
    return pl.pallas_call(
        functools.partial(_c_proj_prompt_kernel, scale=scale, nh=nh, nkv=nkv, hd=hd),
        grid=(m // tm,),
        in_specs=[row(d), _const_spec(g.shape), _const_spec(w.shape)],
        out_specs=out_specs, out_shape=out_shape,
        compiler_params=_params(("parallel",)),
        name="c_proj_prompt",
    )(x, g, w)


def _gelu(x):
    return 0.5 * x * (1.0 + jnp.tanh(math.sqrt(2.0 / math.pi) * (x + 0.044715 * (x * x * x))))


def _compress_tail(y, pe_ref, w1_ref, w2_ref, width):
    nchunk = y.shape[0]
    pe_term = _dot(pe_ref[...], w1_ref[...])[0:1, :]
    second = pltpu.roll(y[:, width:2 * width], nchunk - 1, axis=0)
    hid = _gelu(y[:, 0:width] + second + pe_term).astype(BF)
    return _dot(hid, w2_ref[...])


def _compress_prompt_kernel(kc0_ref, kc1_ref, vc0_ref, vc1_ref, wk_ref, wv_ref, pek_ref, pev_ref,
                            w1k_ref, w1v_ref, w2k_ref, w2v_ref, ko_ref, vo_ref, *, nchunk, width, nkv):
    outs = []
    for x_refs, w_ref, pe_ref, w1_ref, w2_ref in (
            ((kc0_ref, kc1_ref), wk_ref, pek_ref, w1k_ref, w2k_ref),
            ((vc0_ref, vc1_ref), wv_ref, pev_ref, w1v_ref, w2v_ref)):
        y = jnp.zeros((nchunk, 2 * width), F32)
        for r in range(CMP_STRIDE):
            for half, x_ref in enumerate(x_refs):
                xr = x_ref[0, pl.ds(r, nchunk, stride=CMP_STRIDE), :].astype(BF)
                y = y + _dot(xr, w_ref[r, half * LANES:(half + 1) * LANES, :])
        outs.append(_compress_tail(y, pe_ref, w1_ref, w2_ref, width))
    hd = width // nkv
    for i in range(nkv):
        ko_ref[0, i] = outs[0][:, i * hd:(i + 1) * hd].astype(BF)
    vo_ref[0] = outs[1].T.astype(BF)


def _compress_prompt(kc, vc, wk, wv, pek, pev, w1k, w1v, w2k, w2v, nkv):
    b, t, width = kc.shape
    assert width == 2 * LANES
    nchunk = t // CMP_STRIDE
    half = lambda j: pl.BlockSpec((1, t, LANES), lambda i: (i, 0, j))
    consts = [wk, wv, pek, pev, w1k, w1v, w2k, w2v]
    return pl.pallas_call(
        functools.partial(_compress_prompt_kernel, nchunk=nchunk, width=width, nkv=nkv),
        grid=(b,),
        in_specs=[half(0), half(1), half(0), half(1)] + [_const_spec(c.shape) for c in consts],
        out_specs=[pl.BlockSpec((1, nkv, nchunk, width // nkv), lambda i: (i, 0, 0, 0)),
                   pl.BlockSpec((1, width, nchunk), lambda i: (i, 0, 0))],
        out_shape=[jax.ShapeDtypeStruct((b, nkv, nchunk, width // nkv), BF),
                   jax.ShapeDtypeStruct((b, width, nchunk), BF)],
        compiler_params=_params(("parallel",)),
        name="nsa_compress_prompt",
    )(kc, kc, vc, vc, *consts)


def _t5_tiles_kernel(tbl_ref, tiles_ref, bc_ref, *, t, ncmp):
    h = pl.program_id(0)
    val = lambda j: tbl_ref[j, h]
    r = lax.broadcasted_iota(I32, (LANES, LANES), 0)
    c = lax.broadcasted_iota(I32, (LANES, LANES), 1)
    tiles_ref[0, 0] = _t5_bias(r - c, val)
    tiles_ref[0, 1] = _t5_bias(LANES + r - c, val)
    tt = lax.broadcasted_iota(I32, (t, ncmp), 0)
    n = lax.broadcasted_iota(I32, (t, ncmp), 1)
    bc_ref[0] = _t5_bias(tt - (n * CMP_STRIDE + CMP_BLOCK - 1), val)


def _t5_tiles(t5_table, t, ncmp):
    nh = t5_table.shape[1]
    return pl.pallas_call(
        functools.partial(_t5_tiles_kernel, t=t, ncmp=ncmp),
        grid=(nh,),
        in_specs=[pl.BlockSpec(memory_space=pltpu.SMEM)],
        out_specs=[pl.BlockSpec((1, 2, LANES, LANES), lambda i: (i, 0, 0, 0)),
                   pl.BlockSpec((1, t, ncmp), lambda i: (i, 0, 0))],
        out_shape=[jax.ShapeDtypeStruct((nh, 2, LANES, LANES), F32),
                   jax.ShapeDtypeStruct((nh, t, ncmp), F32)],
        compiler_params=_params(("arbitrary",)),
        name="t5_tiles",
    )(t5_table)


def _topk_steps(score, k):
    lane = lax.broadcasted_iota(I32, score.shape, 1).astype(F32)
    big = float(score.shape[1])
    out = []
    for n in range(k):
        m = jnp.max(score, axis=1, keepdims=True)
        idx = jnp.min(jnp.where(score == m, lane, big), axis=1, keepdims=True)
        hit = lane == idx
        out.append((n, hit, idx, m))
        score = jnp.where(hit, -jnp.inf, score)
    return out


def _topk_mask(score, k):
    sel = jnp.zeros(score.shape, F32)
    for _, hit, _, _ in _topk_steps(score, k):
        sel = jnp.where(hit, 1.0, sel)
    return sel > 0.5


def _nsa_attn_kernel(q_ref, kcmp_ref, vcmp_ref, ks_ref, vs_ref, kw_ref, vw_ref, gt_ref, bc_ref,
                     tiles_ref, ov_ref, ex_ref, tc_ref, o_ref, km_ref, *, tq, t, nkv, grp, hd):
    qi = pl.program_id(1)
    ncmp = kcmp_ref.shape[1]
    nsel = t // SEL_BLOCK
    m4 = grp * tq
    trow = qi * tq + lax.broadcasted_iota(I32, (tq, LANES), 0)
    lane = lax.broadcasted_iota(I32, (tq, LANES), 1)
    gates = gt_ref[0]

    def stack(fn):
        return jnp.concatenate([fn(g) for g in range(grp)], axis=0)

    r4 = lax.broadcasted_iota(I32, (m4, LANES), 0) % tq
    c4 = lax.broadcasted_iota(I32, (m4, LANES), 1)

    for k in range(nkv):
        q4 = stack(lambda g: q_ref[0, :, (k * grp + g) * hd:(k * grp + g + 1) * hd])
        kc = kcmp_ref[0, :, k * hd:(k + 1) * hd]
        vc = vcmp_ref[0, :, k * hd:(k + 1) * hd]
        bias_c = stack(lambda g: bc_ref[k * grp + g])
        t4 = qi * tq + r4
        mask_c = (c4 * CMP_STRIDE + CMP_BLOCK - 1) <= t4
        s = jnp.where(mask_c, _dot_nt(q4, kc) + bias_c, NEG_BIG)
        mx = jnp.max(s, axis=1, keepdims=True)
        e = jnp.where(mask_c, jnp.exp(s - mx), 0.0)
        p = e / jnp.maximum(jnp.sum(e, axis=1, keepdims=True), 1e-30)
        pb = p.astype(BF)
        o_cmp = _dot(pb, vc)
        imp4 = _dot(pb, ov_ref[...])
        imp = imp4[0:tq]
        for g in range(1, grp):
            imp = imp + imp4[g * tq:(g + 1) * tq]
        cur = trow // SEL_BLOCK
        forced = (lane == 0) | (lane == cur) | (lane == cur - 1)
        score = jnp.where(forced, FORCE_SCORE, imp)
        score = jnp.where(lane * SEL_BLOCK <= trow, score, NEG_SCORE)
        score = jnp.where(lane < nsel, score, -jnp.inf)
        sel = _topk_mask(score, min(SEL_TOP_N, nsel)) & (score > 0.5 * NEG_SCORE)
        km_ref[...] = _dot(jnp.where(sel, 1.0, 0.0).astype(BF), ex_ref[...])

        def attend(ki, carry, kref, vref, bias, mask_fn):
            m_p, l_p, acc = carry
            start = pl.multiple_of(ki * LANES, LANES)
            kt = kref[0, pl.ds(start, LANES), k * hd:(k + 1) * hd]
            vt = vref[0, pl.ds(start, LANES), k * hd:(k + 1) * hd]
            sc = _dot_nt(q4, kt) + bias
            sc = jnp.where(mask_fn(ki, start), sc, NEG_BIG)
            p_, alpha, m_n, l_n = _softmax_step(sc, m_p, l_p)
            return m_n, l_n, alpha * acc + _dot(p_.astype(BF), vt)

        def const_bias():
            return stack(lambda g: jnp.zeros((tq, LANES), F32) + tc_ref[k * grp + g])

        def tile_bias(which):
            return stack(lambda g: tiles_ref[k * grp + g, which])

        def sel_mask(ki, start):
            km = km_ref[:, pl.ds(start, LANES)]
            km4 = jnp.concatenate([km] * grp, axis=0)
            return (km4 > 0.5) & ((ki * LANES + c4) <= t4)

        def win_mask(ki, start):
            dist = t4 - (ki * LANES + c4)
            return (dist >= 0) & (dist <= WINDOW)

        init = (jnp.full((m4, 1), NEG_BIG, F32), jnp.zeros((m4, 1), F32), jnp.zeros((m4, hd), F32))

        def run(kref, vref, mask_fn, first):
            far = lax.fori_loop(first, jnp.maximum(qi - 1, first),
                                lambda ki, c: attend(ki, c, kref, vref, const_bias(), mask_fn), init)
            near = lax.cond(qi >= 1,
                            lambda c: attend(qi - 1, c, kref, vref, tile_bias(1), mask_fn),
                            lambda c: c, far)
            m_f, l_f, acc = attend(qi, near, kref, vref, tile_bias(0), mask_fn)
            return acc / l_f

        o_slc = run(ks_ref, vs_ref, sel_mask, 0)
        o_win = run(kw_ref, vw_ref, win_mask, jnp.maximum(qi - WINDOW // LANES, 0))
        for g in range(grp):
            hh = k * grp + g
            og = (gates[:, 3 * hh:3 * hh + 1] * o_cmp[g * tq:(g + 1) * tq]
                  + gates[:, 3 * hh + 1:3 * hh + 2] * o_slc[g * tq:(g + 1) * tq]
                  + gates[:, 3 * hh + 2:3 * hh + 3] * o_win[g * tq:(g + 1) * tq])
            o_ref[0, :, hh * hd:(hh + 1) * hd] = og.astype(BF)


def _nsa_attn(q, kcmp, vcmp, ksb, vsb, kwb, vwb, gates, bias_c, tiles, ov, ex, tconst, nkv, grp, hd,
              tq=128):
    b, t, wq = q.shape
    nq = t // tq
    nh = nkv * grp
    full_b = lambda a: pl.BlockSpec((1,) + a.shape[1:], lambda bi, qi: (bi,) + (0,) * (a.ndim - 1))
    return pl.pallas_call(
        functools.partial(_nsa_attn_kernel, tq=tq, t=t, nkv=nkv, grp=grp, hd=hd),
        grid=(b, nq),
        in_specs=[pl.BlockSpec((1, tq, wq), lambda bi, qi: (bi, qi, 0)),
                  full_b(kcmp), full_b(vcmp), full_b(ksb), full_b(vsb), full_b(kwb), full_b(vwb),
                  pl.BlockSpec((1, tq, LANES), lambda bi, qi: (bi, qi, 0)),
                  pl.BlockSpec((nh, tq, LANES), lambda bi, qi: (0, qi, 0)),
                  _const_spec(tiles.shape), _const_spec(ov.shape), _const_spec(ex.shape),
                  pl.BlockSpec(memory_space=pltpu.SMEM)],
        out_specs=pl.BlockSpec((1, tq, wq), lambda bi, qi: (bi, qi, 0)),
        out_shape=jax.ShapeDtypeStruct((b, t, wq), BF),
        scratch_shapes=[pltpu.VMEM((tq, t), F32)],
        compiler_params=_params(("parallel", "arbitrary")),
        name="nsa_attn",
    )(q, kcmp, vcmp, ksb, vsb, kwb, vwb, gates, bias_c, tiles, ov, ex, tconst)


def _page_copies(pt_ref, b, first_page, npages, pools, bufs, sems, layer):
    out = []
    for j in range(npages):
        page = pt_ref[b, first_page + j]
        for pool, buf, sem in zip(pools, bufs, sems):
            out.append(pltpu.make_async_copy(pool.at[layer, page], buf(j), sem))
    return out


def _paged_pipeline(pt_ref, pools, bufs_of_slot, sem_of_slot, layer, npages):
    b = pl.program_id(0)
    s = pl.program_id(1)
    nb = pl.num_programs(0)
    ns = pl.num_programs(1)
    g = b * ns + s
    slot = g % 2

    def copies(bb, ss, sl):
        return _page_copies(pt_ref, bb, ss * npages, npages, pools, bufs_of_slot(sl), sem_of_slot(sl), layer)

    @pl.when(g == 0)
    def _():
        for c in copies(b, s, slot):
            c.start()

    @pl.when(g + 1 < nb * ns)
    def _():
        wrap = s + 1 == ns
        for c in copies(jnp.where(wrap, b + 1, b), jnp.where(wrap, 0, s + 1), 1 - slot):
            c.start()

    for c in copies(b, s, slot):
        c.wait()
    return slot


def _mla_dec_kernel(pt_ref, q_ref, knew_ref, wuv_ref, ckv_hbm, kr_hbm, o_ref,
                    cbuf, rbuf, sem, m_ref, l_ref, acc_ref, *, layer, npages, nh, scale, vdim):
    s_id = pl.program_id(1)
    ns = pl.num_programs(1)
    slot = _paged_pipeline(
        pt_ref, (ckv_hbm, kr_hbm),
        lambda sl: (lambda j: cbuf.at[sl, j], lambda j: rbuf.at[sl, j]),
        lambda sl: (sem.at[0, sl], sem.at[1, sl]), layer, npages)

    @pl.when(s_id == 0)
    def _():
        m_ref[...] = jnp.full(m_ref.shape, NEG_BIG, F32)
        l_ref[...] = jnp.zeros(l_ref.shape, F32)
        acc_ref[...] = jnp.zeros(acc_ref.shape, F32)

    q = q_ref[0]
    page = cbuf.shape[2]
    kp = cbuf[slot].reshape(npages * page, cbuf.shape[3]).astype(BF)
    rp = rbuf[slot].reshape(npages * page, rbuf.shape[3]).astype(BF)
    sc = (_dot_nt(q[:, 0:128].astype(BF), kp) + _dot_nt(q[:, 128:128 + MLA_ROPE].astype(BF), rp)) * scale
    p, alpha, m_new, l_new = _softmax_step(sc, m_ref[...], l_ref[...])
    acc_ref[...] = alpha * acc_ref[...] + _dot(p.astype(BF), kp)
    m_ref[...] = m_new
    l_ref[...] = l_new

    @pl.when(s_id == ns - 1)
    def _():
        kn = knew_ref[0]
        s_new = jnp.sum(q * kn, axis=1, keepdims=True) * scale
        m_p = m_ref[...]
        m_n = jnp.maximum(m_p, s_new)
        a = jnp.exp(m_p - m_n)
        p_new = jnp.exp(s_new - m_n)
        l_f = a * l_ref[...] + p_new
        acc = a * acc_ref[...] + p_new.astype(BF).astype(F32) * kn[:, 0:128]
        olat = (acc / l_f).astype(BF)
        r = _dot(olat, wuv_ref[...])
        row = lax.broadcasted_iota(I32, r.shape, 0)
        col = lax.broadcasted_iota(I32, r.shape, 1)
        o_ref[0] = jnp.sum(jnp.where(col // vdim == row, r, 0.0), axis=0, keepdims=True).astype(BF)


def _mla_dec(page_table, q, knew, wuv_all, ckv_pool, kr_pool, layer, nh, scale, npages=16):
    db, n_pages = page_table.shape
    page = ckv_pool.shape[2]
    vdim = wuv_all.shape[1] // nh
    grid_spec = pltpu.PrefetchScalarGridSpec(
        num_scalar_prefetch=1, grid=(db, n_pages // npages),
        in_specs=[pl.BlockSpec((1, nh, 256), lambda b, s, pt: (b, 0, 0)),
                  pl.BlockSpec((1, 1, 256), lambda b, s, pt: (b, 0, 0)),
                  _const_spec(wuv_all.shape),
                  pl.BlockSpec(memory_space=pl.ANY), pl.BlockSpec(memory_space=pl.ANY)],
        out_specs=pl.BlockSpec((1, 1, wuv_all.shape[1]), lambda b, s, pt: (b, 0, 0)),
        scratch_shapes=[pltpu.VMEM((2, npages, page, ckv_pool.shape[3]), F32),
                        pltpu.VMEM((2, npages, page, kr_pool.shape[3]), F32),
                        pltpu.SemaphoreType.DMA((2, 2)),
                        pltpu.VMEM((nh, 1), F32), pltpu.VMEM((nh, 1), F32), pltpu.VMEM((nh, 128), F32)])
    return pl.pallas_call(
        functools.partial(_mla_dec_kernel, layer=layer, npages=npages, nh=nh, scale=scale, vdim=vdim),
        grid_spec=grid_spec,
        out_shape=jax.ShapeDtypeStruct((db, 1, wuv_all.shape[1]), BF),
        compiler_params=_params(("arbitrary", "arbitrary")),
        name="mla_dec",
    )(page_table, q, knew, wuv_all, ckv_pool, kr_pool)


def _fox_bias_kernel(pt_ref, lfn_ref, lf_hbm, o_ref, lbuf, sem, *, layer, n_pages):
    b = pl.program_id(0)
    page = lbuf.shape[1]

    def copy(p):
        return pltpu.make_async_copy(lf_hbm.at[layer, pt_ref[b, p]], lbuf.at[p], sem.at[0])

    def start(p, c):
        copy(p).start()
        return c

    def wait(p, c):
        copy(p).wait()
        return c

    lax.fori_loop(0, n_pages, start, 0)
    lax.fori_loop(0, n_pages, wait, 0)

    def xpose(p, c):
        o_ref[0, :, pl.ds(pl.multiple_of(p * page, page), page)] = lbuf[p].T
        return c

    lax.fori_loop(0, n_pages, xpose, 0)
    x = o_ref[0]
    n = x.shape[1]
    lane = lax.broadcasted_iota(I32, x.shape, 1)
    y = x
    k = 1
    while k < n:
        y = y + jnp.where(lane + k < n, pltpu.roll(y, n - k, axis=1), 0.0)
        k *= 2
    o_ref[0] = (y - x) + lfn_ref[0]


def _fox_bias(page_table, lf_new, lf_pool, layer):
    db, n_pages = page_table.shape
    page, nh = lf_pool.shape[2], lf_pool.shape[3]
    grid_spec = pltpu.PrefetchScalarGridSpec(
        num_scalar_prefetch=1, grid=(db,),
        in_specs=[pl.BlockSpec((1, nh, 1), lambda b, pt: (b, 0, 0)), pl.BlockSpec(memory_space=pl.ANY)],
        out_specs=pl.BlockSpec((1, nh, n_pages * page), lambda b, pt: (b, 0, 0)),
        scratch_shapes=[pltpu.VMEM((n_pages, page, nh), F32), pltpu.SemaphoreType.DMA((1,))])
    return pl.pallas_call(
        functools.partial(_fox_bias_kernel, layer=layer, n_pages=n_pages),
        grid_spec=grid_spec,
        out_shape=jax.ShapeDtypeStruct((db, nh, n_pages * page), F32),
        compiler_params=_params(("arbitrary",)),
        name="fox_bias",
    )(page_table, lf_new, lf_pool)


def _fox_dec_kernel(pt_ref, q_ref, kn_ref, vn_ref, bias_ref, k_hbm, v_hbm, o_ref,
                    kbuf, vbuf, sem, m_ref, l_ref, acc_ref, *, layer, npages, nh):
    s_id = pl.program_id(1)
    ns = pl.num_programs(1)
    rows = kbuf.shape[1] // npages
    slot = _paged_pipeline(
        pt_ref, (k_hbm, v_hbm),
        lambda sl: (lambda j: kbuf.at[sl, pl.ds(j * rows, rows), :], lambda j: vbuf.at[sl, pl.ds(j * rows, rows), :]),
        lambda sl: (sem.at[0, sl], sem.at[1, sl]), layer, npages)

    @pl.when(s_id == 0)
    def _():
        m_ref[...] = jnp.full(m_ref.shape, NEG_BIG, F32)
        l_ref[...] = jnp.zeros(l_ref.shape, F32)
        acc_ref[...] = jnp.zeros(acc_ref.shape, F32)

    q = q_ref[0]
    nkeys = kbuf.shape[1] // nh
    hrow = lax.broadcasted_iota(I32, q.shape, 0)
    sc = bias_ref[0]
    for h in range(nh):
        kh = kbuf[slot, pl.ds(h, nkeys, stride=nh), :].astype(BF)
        sc = sc + _dot_nt(jnp.where(hrow == h, q, 0.0).astype(BF), kh)
    p, alpha, m_new, l_new = _softmax_step(sc, m_ref[...], l_ref[...])
    prow = lax.broadcasted_iota(I32, p.shape, 0)
    acc = alpha * acc_ref[...]
    for h in range(nh):
        vh = vbuf[slot, pl.ds(h, nkeys, stride=nh), :].astype(BF)
        acc = acc + _dot(jnp.where(prow == h, p, 0.0).astype(BF), vh)
    acc_ref[...] = acc
    m_ref[...] = m_new
    l_ref[...] = l_new

    @pl.when(s_id == ns - 1)
    def _():
        kn = kn_ref[0].astype(BF).astype(F32)
        vn = vn_ref[0].astype(BF).astype(F32)
        s_new = jnp.sum(q * kn, axis=1, keepdims=True)
        m_p = m_ref[...]
        m_n = jnp.maximum(m_p, s_new)
        a = jnp.exp(m_p - m_n)
        p_new = jnp.exp(s_new - m_n)
        l_f = a * l_ref[...] + p_new
        o_ref[0] = ((a * acc_ref[...] + p_new.astype(BF).astype(F32) * vn) / l_f).astype(BF)


def _fox_dec(page_table, q, kn, vn, bias, k_pool, v_pool, layer, nh, npages=8):
    db, n_pages = page_table.shape
    rows, hd = k_pool.shape[2], k_pool.shape[3]
    page = rows // nh
    grid_spec = pltpu.PrefetchScalarGridSpec(
        num_scalar_prefetch=1, grid=(db, n_pages // npages),
        in_specs=[pl.BlockSpec((1, nh, hd), lambda b, s, pt: (b, 0, 0)),
                  pl.BlockSpec((1, nh, hd), lambda b, s, pt: (b, 0, 0)),
                  pl.BlockSpec((1, nh, hd), lambda b, s, pt: (b, 0, 0)),
                  pl.BlockSpec((1, nh, npages * page), lambda b, s, pt: (b, 0, s)),
                  pl.BlockSpec(memory_space=pl.ANY), pl.BlockSpec(memory_space=pl.ANY)],
        out_specs=pl.BlockSpec((1, nh, hd), lambda b, s, pt: (b, 0, 0)),
        scratch_shapes=[pltpu.VMEM((2, npages * rows, hd), F32), pltpu.VMEM((2, npages * rows, hd), F32),
                        pltpu.SemaphoreType.DMA((2, 2)),
                        pltpu.VMEM((nh, 1), F32), pltpu.VMEM((nh, 1), F32), pltpu.VMEM((nh, hd), F32)])
    return pl.pallas_call(
        functools.partial(_fox_dec_kernel, layer=layer, npages=npages, nh=nh),
        grid_spec=grid_spec,
        out_shape=jax.ShapeDtypeStruct((db, nh, hd), BF),
        compiler_params=_params(("arbitrary", "arbitrary")),
        name="fox_dec",
    )(page_table, q, kn, vn, bias, k_pool, v_pool)


def _rows(fn, n):
    return jnp.concatenate([fn(g) for g in range(n)], axis=0)


def _nsa_dec_a_kernel(pt_ref, q_ref, w1k_ref, w1v_ref, pek_ref, pev_ref, w1fk_ref, w1fv_ref, w2k_ref,
                      w2v_ref, ov_ref, tbl_ref, kc_hbm, vc_hbm, oc_ref, ti_ref, ok_ref,
                      kbuf, vbuf, sem, fsk_ref, fsv_ref, *, layer, npages, nkv, grp, hd, pos, nsel):
    s_id = pl.program_id(1)
    ns = pl.num_programs(1)
    rows = kbuf.shape[1] // npages
    slot = _paged_pipeline(
        pt_ref, (kc_hbm, vc_hbm),
        lambda sl: (lambda j: kbuf.at[sl, pl.ds(j * rows, rows), :], lambda j: vbuf.at[sl, pl.ds(j * rows, rows), :]),
        lambda sl: (sem.at[0, sl], sem.at[1, sl]), layer, npages)
    nchunk = kbuf.shape[1] // (nkv * CMP_STRIDE)
    base = pl.multiple_of(s_id * nchunk, nchunk)
    for buf, w_ref, fs_ref in ((kbuf, w1k_ref, fsk_ref), (vbuf, w1v_ref, fsv_ref)):
        for k in range(nkv):
            y = jnp.zeros((nchunk, 2 * hd), F32)
            for r in range(CMP_STRIDE):
                xr = buf[slot, pl.ds(nkv * r + k, nchunk, stride=nkv * CMP_STRIDE), :].astype(BF)
                y = y + _dot(xr, w_ref[r])
            fs_ref[k, pl.ds(base, nchunk), :] = y

    @pl.when(s_id == ns - 1)
    def _():
        ncmp = fsk_ref.shape[1]
        n_valid = (pos - (CMP_BLOCK - 1)) // CMP_STRIDE + 1
        lane = lax.broadcasted_iota(I32, (1, ncmp), 1)
        dist_c = pos - (lane * CMP_STRIDE + CMP_BLOCK - 1)
        pe_k = _dot(pek_ref[...], w1fk_ref[...])[0:1, :]
        pe_v = _dot(pev_ref[...], w1fv_ref[...])[0:1, :]
        imps = []
        for k in range(nkv):
            def cmp_rows(fs_ref, pe, w2_ref):
                fs = fs_ref[k]
                second = pltpu.roll(fs, ncmp - 1, axis=0)[:, hd:2 * hd]
                hid = _gelu(fs[:, 0:hd] + second + pe).astype(BF)
                return _dot(hid, w2_ref[...]).astype(BF)

            kcmp = cmp_rows(fsk_ref, pe_k, w2k_ref)
            vcmp = cmp_rows(fsv_ref, pe_v, w2v_ref)
            qk = q_ref[0, k * grp:(k + 1) * grp, :].astype(BF)
            bias = _rows(lambda g: _t5_bias(dist_c, lambda j: tbl_ref[j, k * grp + g]), grp)
            valid = lane < n_valid
            s = jnp.where(valid, _dot_nt(qk, kcmp) + bias, NEG_BIG)
            mx = jnp.max(s, axis=1, keepdims=True)
            e = jnp.where(valid, jnp.exp(s - mx), 0.0)
            p = (e / jnp.maximum(jnp.sum(e, axis=1, keepdims=True), 1e-30)).astype(BF)
            oc_ref[0, k * grp:(k + 1) * grp, :] = _dot(p, vcmp)
            imps.append(jnp.sum(_dot(p, ov_ref[...]), axis=0, keepdims=True))
        imp = jnp.concatenate(imps, axis=0)
        jl = lax.broadcasted_iota(I32, imp.shape, 1)
        cur = pos // SEL_BLOCK
        forced = (jl == 0) | (jl == cur) | (jl == cur - 1)
        score = jnp.where(forced, FORCE_SCORE, imp)
        score = jnp.where(jl * SEL_BLOCK <= pos, score, NEG_SCORE)
        score = jnp.where(jl < nsel, score, -jnp.inf)
        ol = lax.broadcasted_iota(I32, (nkv, LANES), 1)
        ti = jnp.zeros((nkv, LANES), I32)
        ok = jnp.zeros((nkv, LANES), I32)
        for n, _, idx, val in _topk_steps(score, min(SEL_TOP_N, nsel)):
            ti = jnp.where(ol == n, idx.astype(I32), ti)
            ok = jnp.where(ol == n, (val > 0.5 * NEG_SCORE).astype(I32), ok)
        ti_ref[0] = ti
        ok_ref[0] = ok


def _nsa_dec_a(page_table, q, w1k, w1v, pek, pev, w1fk, w1fv, w2k, w2v, ov, t5_table, kc_pool, vc_pool,
               layer, nkv, grp, hd, pos, nsel, npages=16):
    db, n_pages = page_table.shape
    rows = kc_pool.shape[2]
    page = rows // nkv
    ncmp = n_pages * page // CMP_STRIDE
    consts = [w1k, w1v, pek, pev, w1fk, w1fv, w2k, w2v, ov]
    grid_spec = pltpu.PrefetchScalarGridSpec(
        num_scalar_prefetch=1, grid=(db, n_pages // npages),
        in_specs=([pl.BlockSpec((1, nkv * grp, hd), lambda b, s, pt: (b, 0, 0))]
                  + [_const_spec(c.shape) for c in consts]
                  + [pl.BlockSpec(memory_space=pltpu.SMEM),
                     pl.BlockSpec(memory_space=pl.ANY), pl.BlockSpec(memory_space=pl.ANY)]),
        out_specs=[pl.BlockSpec((1, nkv * grp, hd), lambda b, s, pt: (b, 0, 0)),
                   pl.BlockSpec((1, nkv, LANES), lambda b, s, pt: (b, 0, 0)),
                   pl.BlockSpec((1, nkv, LANES), lambda b, s, pt: (b, 0, 0))],
        scratch_shapes=[pltpu.VMEM((2, npages * rows, hd), F32), pltpu.VMEM((2, npages * rows, hd), F32),
                        pltpu.SemaphoreType.DMA((2, 2)),
                        pltpu.VMEM((nkv, ncmp, 2 * hd), F32), pltpu.VMEM((nkv, ncmp, 2 * hd), F32)])
    return pl.pallas_call(
        functools.partial(_nsa_dec_a_kernel, layer=layer, npages=npages, nkv=nkv, grp=grp, hd=hd,
                          pos=pos, nsel=nsel),
        grid_spec=grid_spec,
        out_shape=[jax.ShapeDtypeStruct((db, nkv * grp, hd), F32),
                   jax.ShapeDtypeStruct((db, nkv, LANES), I32),
                   jax.ShapeDtypeStruct((db, nkv, LANES), I32)],
        compiler_params=_params(("arbitrary", "arbitrary")),
        name="nsa_dec_cmp",
    )(page_table, q, *consts, t5_table, kc_pool, vc_pool)


def _nsa_dec_b_kernel(pt_ref, ti_ref, ok_ref, q_ref, ksn_ref, vsn_ref, kwn_ref, vwn_ref, wk_ref, wv_ref,
                      gt_ref, oc_ref, tbl_ref, ks_hbm, vs_hbm, o_ref, kb, vb, sem,
                      *, layer, nkv, grp, hd, pos, npast_sel, blk_rows, blocks_per_page, nwin):
    b = pl.program_id(0)
    ntop = kb.shape[1] // blk_rows

    def sel_dma(k, n, wait):
        blk = ti_ref[b, k * ntop + n]

        @pl.when(blk < npast_sel)
        def _():
            pg = pt_ref[b, blk // blocks_per_page]
            off = pl.multiple_of((blk % blocks_per_page) * blk_rows, blk_rows)
            for pool, buf, sm in ((ks_hbm, kb, sem.at[0]), (vs_hbm, vb, sem.at[1])):
                c = pltpu.make_async_copy(pool.at[layer, pg, pl.ds(off, blk_rows), :],
                                          buf.at[k, pl.ds(n * blk_rows, blk_rows), :], sm)
                if wait:
                    c.wait()
                else:
                    c.start()

        if not wait:
            @pl.when(blk >= npast_sel)
            def _():
                for buf, new_ref in ((kb, ksn_ref), (vb, vsn_ref)):
                    buf[k, pl.ds(n * blk_rows, blk_rows), :] = jnp.zeros((blk_rows, hd), F32)
                    buf[k, pl.ds(n * blk_rows + k, 1), :] = new_ref[0, k:k + 1, :]

    for k in range(nkv):
        for n in range(ntop):
            sel_dma(k, n, False)

    def head_col(fn):
        r = lax.broadcasted_iota(I32, (grp, 1), 0)
        out = jnp.zeros((grp, 1), F32)
        for g in range(grp):
            out = jnp.where(r == g, fn(g), out)
        return out

    def bf_round(x):
        return x.astype(BF).astype(F32)

    o_wins = []
    jw = lax.broadcasted_iota(I32, (1, nwin), 1)
    dist_w = nwin - jw
    for k in range(nkv):
        qk = q_ref[0, k * grp:(k + 1) * grp, :]
        kw = wk_ref[0, 0, pl.ds(k, nwin, stride=nkv), :].astype(BF)
        vw = wv_ref[0, 0, pl.ds(k, nwin, stride=nkv), :].astype(BF)
        bias = _rows(lambda g: _t5_bias(dist_w, lambda j: tbl_ref[j, k * grp + g]), grp)
        s = _dot_nt(qk.astype(BF), kw) + bias
        s_new = (jnp.sum(qk * bf_round(kwn_ref[0, k:k + 1, :]), axis=1, keepdims=True)
                 + head_col(lambda g: tbl_ref[0, k * grp + g]))
        mx = jnp.maximum(jnp.max(s, axis=1, keepdims=True), s_new)
        e = jnp.exp(s - mx)
        e_new = jnp.exp(s_new - mx)
        l = jnp.sum(e, axis=1, keepdims=True) + e_new
        o_wins.append((_dot(e.astype(BF), vw) + bf_round(e_new) * bf_round(vwn_ref[0, k:k + 1, :])) / l)

    for k in range(nkv):
        for n in range(ntop):
            sel_dma(k, n, True)

    nkeys = ntop * SEL_BLOCK
    lane = lax.broadcasted_iota(I32, (1, nkeys), 1)
    slot_id = lane // SEL_BLOCK
    for k in range(nkv):
        qk = q_ref[0, k * grp:(k + 1) * grp, :]
        ks = kb[k, pl.ds(k, nkeys, stride=nkv), :].astype(BF)
        vs = vb[k, pl.ds(k, nkeys, stride=nkv), :].astype(BF)
        blkv = jnp.zeros((1, nkeys), I32)
        okv = jnp.zeros((1, nkeys), I32)
        for n in range(ntop):
            blkv = jnp.where(slot_id == n, ti_ref[b, k * ntop + n], blkv)
            okv = jnp.where(slot_id == n, ok_ref[b, k * ntop + n], okv)
        dist = pos - (blkv * SEL_BLOCK + lane % SEL_BLOCK)
        mask = (okv > 0) & (dist >= 0)
        bias = _rows(lambda g: _t5_bias(dist, lambda j: tbl_ref[j, k * grp + g]), grp)
        s = jnp.where(mask, _dot_nt(qk.astype(BF), ks) + bias, NEG_BIG)
        mx = jnp.max(s, axis=1, keepdims=True)
        e = jnp.where(mask, jnp.exp(s - mx), 0.0)
        l = jnp.maximum(jnp.sum(e, axis=1, keepdims=True), 1e-30)
        o_slc = _dot(e.astype(BF), vs) / l
        gt = gt_ref[0, k * grp:(k + 1) * grp, :]
        o = (gt[:, 0:1] * oc_ref[0, k * grp:(k + 1) * grp, :] + gt[:, 1:2] * o_slc + gt[:, 2:3] * o_wins[k])
        o_ref[0, k * grp:(k + 1) * grp, :] = o.astype(BF)


def _nsa_dec_b(page_table, top_i, top_ok, q, ksn, vsn, kwn, vwn, win_k, win_v, gates, o_cmp, t5_table,
               ks_pool, vs_pool, layer, nkv, grp, hd, pos, npast_sel):
    db = q.shape[0]
    nh = nkv * grp
    ntop = top_i.shape[1] // nkv
    page_rows = ks_pool.shape[2]
    blk_rows = SEL_BLOCK * nkv
    nwin = win_k.shape[2] // nkv
    m3 = lambda b, *_: (b, 0, 0)
    grid_spec = pltpu.PrefetchScalarGridSpec(
        num_scalar_prefetch=3, grid=(db,),
        in_specs=[pl.BlockSpec((1, nh, hd), m3)] + [pl.BlockSpec((1, nkv, hd), m3)] * 4
                 + [pl.BlockSpec((1, 1) + win_k.shape[2:], lambda b, *_: (layer, b, 0, 0))] * 2
                 + [pl.BlockSpec((1, nh, 3), m3), pl.BlockSpec((1, nh, hd), m3),
                    pl.BlockSpec(memory_space=pltpu.SMEM),
                    pl.BlockSpec(memory_space=pl.ANY), pl.BlockSpec(memory_space=pl.ANY)],
        out_specs=pl.BlockSpec((1, nh, hd), m3),
        scratch_shapes=[pltpu.VMEM((nkv, ntop * blk_rows, hd), F32),
                        pltpu.VMEM((nkv, ntop * blk_rows, hd), F32),
                        pltpu.SemaphoreType.DMA((2,))])
    return pl.pallas_call(
        functools.partial(_nsa_dec_b_kernel, layer=layer, nkv=nkv, grp=grp, hd=hd, pos=pos,
                          npast_sel=npast_sel, blk_rows=blk_rows,
                          blocks_per_page=page_rows // blk_rows, nwin=nwin),
        grid_spec=grid_spec,
        out_shape=jax.ShapeDtypeStruct((db, nh, hd), BF),
        compiler_params=_params(("arbitrary",)),
        name="nsa_dec_sel_win",
    )(page_table, top_i, top_ok, q, ksn, vsn, kwn, vwn, win_k, win_v, gates, o_cmp, t5_table,
      ks_pool, vs_pool)


KM_TQ = 128
KM_TK = 256


LOG2E = 1.4426950408889634


def _t5_km_kernel(tbl_ref, tiles_ref, bc_ref, *, t, ncmp):
    h = pl.program_id(0)
    far = tbl_ref[T5_BUCKETS - 1, h]
    rel = lambda j: (tbl_ref[j, h] - far) * LOG2E
    c = lax.broadcasted_iota(I32, (KM_TK, KM_TQ), 0)
    r = lax.broadcasted_iota(I32, (KM_TK, KM_TQ), 1)
    for i in range(3):
        dist = i * KM_TQ + r - c
        tiles_ref[0, i] = jnp.where(dist >= 0, _t5_bias(dist, rel), NEG_BIG)
    tiles_ref[0, 3] = jnp.zeros((KM_TK, KM_TQ), F32)
    n = lax.broadcasted_iota(I32, (ncmp, t), 0)
    tt = lax.broadcasted_iota(I32, (ncmp, t), 1)
    bc_ref[0] = _t5_bias(tt - (n * CMP_STRIDE + CMP_BLOCK - 1), lambda j: tbl_ref[j, h] * LOG2E)


def _t5_km(t5_table, t, ncmp):
    nh = t5_table.shape[1]
    return pl.pallas_call(
        functools.partial(_t5_km_kernel, t=t, ncmp=ncmp),
        grid=(nh,),
        in_specs=[pl.BlockSpec(memory_space=pltpu.SMEM)],
        out_specs=[pl.BlockSpec((1, 4, KM_TK, KM_TQ), lambda i: (i, 0, 0, 0)),
                   pl.BlockSpec((1, ncmp, t), lambda i: (i, 0, 0))],
        out_shape=[jax.ShapeDtypeStruct((nh, 4, KM_TK, KM_TQ), F32),
                   jax.ShapeDtypeStruct((nh, ncmp, t), F32)],
        compiler_params=_params(("arbitrary",)),
        name="t5_tiles",
    )(t5_table)


def _lanes(fn, n):
    return jnp.concatenate([fn(g) for g in range(n)], axis=1)


def _nsa_km_kernel(q_ref, kcmp_ref, vcmpt_ref, ks_ref, vst_ref, kw_ref, vwt_ref, gtt_ref, bc_ref,
                   tiles_ref, ovt_ref, o_ref, selt_ref, *, t, nkv, grp, hd):
    qi = pl.program_id(1)
    tq, tk = KM_TQ, KM_TK
    nq4 = grp * tq
    ncmp = kcmp_ref.shape[2]
    nsel = t // SEL_BLOCK
    per_tile = tk // SEL_BLOCK
    tcol = qi * tq + lax.broadcasted_iota(I32, (1, nq4), 1) % tq
    krow = lax.broadcasted_iota(I32, (tk, 1), 0)
    last = qi // 2
    odd = qi % 2

    q4s = [q_ref[0, k * grp:(k + 1) * grp].reshape(nq4, hd) for k in range(nkv)]
    o_cmps = []
    for k in range(nkv):
        q4 = q4s[k]

        nrow = lax.broadcasted_iota(I32, (ncmp, 1), 0)
        mask_c = (nrow * CMP_STRIDE + CMP_BLOCK - 1) <= tcol
        s = _dot_nt(kcmp_ref[0, k], q4) + _lanes(lambda g: bc_ref[k * grp + g], grp)
        s = jnp.where(mask_c, s, NEG_BIG)
        e = jnp.where(mask_c, jnp.exp2(s - jnp.max(s, axis=0, keepdims=True)), 0.0)
        pb = (e / jnp.maximum(jnp.sum(e, axis=0, keepdims=True), 1e-30)).astype(BF)
        o_cmp = _dot(vcmpt_ref[0, k * hd:(k + 1) * hd, :], pb)
        imp4 = _dot(ovt_ref[...], pb)
        imp = imp4[:, 0:tq]
        for g in range(1, grp):
            imp = imp + imp4[:, g * tq:(g + 1) * tq]

        jrow = lax.broadcasted_iota(I32, imp.shape, 0)
        trow = qi * tq + lax.broadcasted_iota(I32, imp.shape, 1)
        cur = trow // SEL_BLOCK
        forced = (jrow == 0) | (jrow == cur) | (jrow == cur - 1)
        score = jnp.where(forced, FORCE_SCORE, imp)
        score = jnp.where(jrow * SEL_BLOCK <= trow, score, NEG_SCORE)
        score = jnp.where(jrow < nsel, score, -jnp.inf)
        ok = score > 0.5 * NEG_SCORE
        jf = jrow.astype(F32)
        sel = jnp.zeros(imp.shape, F32)
        for _ in range(min(SEL_TOP_N, nsel)):
            mx = jnp.max(score, axis=0, keepdims=True)
            idx = jnp.min(jnp.where(score == mx, jf, float(imp.shape[0])), axis=0, keepdims=True)
            hit = jf == idx
            sel = jnp.where(hit, 1.0, sel)
            score = jnp.where(hit, -jnp.inf, score)
        selt_ref[k] = jnp.where(ok & (sel > 0.5), 0.0, NEG_BIG)
        o_cmps.append(o_cmp)

    def tile(k, ki, carry, kref, vtref, bias):
        m_p, l_p, acc = carry
        start = pl.multiple_of(ki * tk, tk)
        st = _dot_nt(kref[0, k, pl.ds(start, tk), :], q4s[k]) + bias
        m_n = jnp.maximum(m_p, jnp.max(st, axis=0, keepdims=True))
        p = jnp.exp2(st - m_n)
        alpha = jnp.exp2(m_p - m_n)
        l_n = alpha * l_p + jnp.sum(p, axis=0, keepdims=True)
        vt = vtref[0, k * hd:(k + 1) * hd, pl.ds(start, tk)]
        return m_n, l_n, alpha * acc + _dot(vt, p.astype(BF))

    def near_bias(k, which):
        return _lanes(lambda g: tiles_ref[k * grp + g, which], grp)

    def sel_bias(k, ki):
        rows = [jnp.broadcast_to(selt_ref[k, pl.ds(ki * per_tile + j, 1), :], (SEL_BLOCK, tq))
                for j in range(per_tile)]
        return jnp.concatenate([jnp.concatenate(rows, axis=0)] * grp, axis=1)

    init = (jnp.full((1, nq4), NEG_BIG, F32), jnp.zeros((1, nq4), F32), jnp.zeros((hd, nq4), F32))

    far = lax.fori_loop(
        0, jnp.maximum(last - 1, 0),
        lambda ki, cs: tuple(tile(k, ki, cs[k], ks_ref, vst_ref, sel_bias(k, ki)) for k in range(nkv)),
        (init,) * nkv)

    k1 = jnp.maximum(last - 1, 0)
    k2 = jnp.maximum(last - 2, 0)
    pen1 = jnp.where(last >= 1, 0.0, NEG_BIG)
    win_far = jnp.where((tcol - (k2 * tk + krow) <= WINDOW) & (last >= 2), 0.0, NEG_BIG)
    for k in range(nkv):
        c = tile(k, k1, far[k], ks_ref, vst_ref, near_bias(k, 2 + odd) + sel_bias(k, k1) + pen1)
        _, l_s, acc_s = tile(k, last, c, ks_ref, vst_ref, near_bias(k, odd) + sel_bias(k, last))
        o_slc = acc_s / l_s

        c = tile(k, k2, init, kw_ref, vwt_ref, win_far)
        c = tile(k, k1, c, kw_ref, vwt_ref, near_bias(k, 2 + odd) + pen1)
        _, l_w, acc_w = tile(k, last, c, kw_ref, vwt_ref, near_bias(k, odd))
        o_win = acc_w / l_w

        gate = lambda br: _lanes(lambda g: gtt_ref[0, pl.ds(3 * (k * grp + g) + br, 1), :], grp)
        o_t = gate(0) * o_cmps[k] + gate(1) * o_slc + gate(2) * o_win
        for g in range(grp):
            hh = k * grp + g
            o_ref[0, :, hh * hd:(hh + 1) * hd] = o_t[:, g * tq:(g + 1) * tq].T.astype(BF)


def _nsa_attn_km(q, kcmp, vcmpt, ksh, vst, kwh, vwt, gatest, bias_ct, tiles, ovt, nkv, grp, hd):
    b, nh, t, _ = q.shape
    tq = KM_TQ
    assert WINDOW == 2 * KM_TK and t % KM_TK == 0
    full_b = lambda a: pl.BlockSpec((1,) + a.shape[1:], lambda bi, qi: (bi,) + (0,) * (a.ndim - 1))
    single = lambda a: pl.BlockSpec(a.shape, lambda bi, qi: (0,) * a.ndim, pipeline_mode=pl.Buffered(1))
    return pl.pallas_call(
        functools.partial(_nsa_km_kernel, t=t, nkv=nkv, grp=grp, hd=hd),
        grid=(b, t // tq),
        in_specs=[pl.BlockSpec((1, nh, tq, hd), lambda bi, qi: (bi, 0, qi, 0)),
                  full_b(kcmp), full_b(vcmpt), full_b(ksh), full_b(vst), full_b(kwh), full_b(vwt),
                  pl.BlockSpec((1, LANES, tq), lambda bi, qi: (bi, 0, qi)),
                  pl.BlockSpec((nh, bias_ct.shape[1], tq), lambda bi, qi: (0, 0, qi)),
                  single(tiles), _const_spec(ovt.shape)],
        out_specs=pl.BlockSpec((1, tq, nh * hd), lambda bi, qi: (bi, qi, 0)),
        out_shape=jax.ShapeDtypeStruct((b, t, nh * hd), BF),
        scratch_shapes=[pltpu.VMEM((nkv, LANES, tq), F32)],
        compiler_params=_params(("parallel", "arbitrary")),
        name="nsa_attn",
    )(q, kcmp, vcmpt, ksh, vst, kwh, vwt, gatest, bias_ct, tiles, ovt)


def _step_pipeline(copies):
    b = pl.program_id(0)
    s = pl.program_id(1)
    nb = pl.num_programs(0)
    ns = pl.num_programs(1)
    g = b * ns + s
    slot = g % 2

    @pl.when(g == 0)
    def _():
        for c in copies(b, s, slot):
            c.start()

    @pl.when(g + 1 < nb * ns)
    def _():
        wrap = s + 1 == ns
        for c in copies(jnp.where(wrap, b + 1, b), jnp.where(wrap, 0, s + 1), 1 - slot):
            c.start()

    for c in copies(b, s, slot):
        c.wait()
    return slot


def _diag_rows(r, width):
    row = lax.broadcasted_iota(I32, r.shape, 0)
    col = lax.broadcasted_iota(I32, r.shape, 1)
    return jnp.sum(jnp.where(col // width == row, r, 0.0), axis=0, keepdims=True)


def _bf_round(x):
    return x.astype(BF).astype(F32)


def _mla_dec2_kernel(pt_ref, q_ref, knew_ref, wuv_ref, ckv_hbm, krt_hbm, o_ref,
                     cbuf, rbuf, sem, m_ref, l_ref, acc_ref, *, layer, npages, scale, vdim):
    s_id = pl.program_id(1)
    ns = pl.num_programs(1)
    page = cbuf.shape[2]

    def copies(bb, ss, sl):
        out = []
        for j in range(npages):
            pg = pt_ref[bb, ss * npages + j]
            out.append(pltpu.make_async_copy(ckv_hbm.at[layer, pg], cbuf.at[sl, j], sem.at[0, sl]))
            out.append(pltpu.make_async_copy(krt_hbm.at[layer, pg], rbuf.at[sl, :, pl.ds(j * page, page)],
                                             sem.at[1, sl]))
        return out

    slot = _step_pipeline(copies)

    @pl.when(s_id == 0)
    def _():
        m_ref[...] = jnp.full(m_ref.shape, NEG_BIG, F32)
        l_ref[...] = jnp.zeros(l_ref.shape, F32)
        acc_ref[...] = jnp.zeros(acc_ref.shape, F32)

    q = q_ref[0]
    kp = cbuf[slot].reshape(npages * page, cbuf.shape[3]).astype(BF)
    sc = (_dot_nt(q[:, 0:128].astype(BF), kp)
          + _dot(q[:, 128:128 + MLA_ROPE].astype(BF), rbuf[slot].astype(BF))) * scale
    p, alpha, m_new, l_new = _softmax_step(sc, m_ref[...], l_ref[...])
    acc_ref[...] = alpha * acc_ref[...] + _dot(p.astype(BF), kp)
    m_ref[...] = m_new
    l_ref[...] = l_new

    @pl.when(s_id == ns - 1)
    def _():
        kn = knew_ref[0]
        s_new = jnp.sum(q * kn, axis=1, keepdims=True) * scale
        m_p = m_ref[...]
        m_n = jnp.maximum(m_p, s_new)
        a = jnp.exp(m_p - m_n)
        p_new = jnp.exp(s_new - m_n)
        l_f = a * l_ref[...] + p_new
        acc = a * acc_ref[...] + _bf_round(p_new) * kn[:, 0:128]
        olat = (acc / l_f).astype(BF)
        o_ref[0] = _diag_rows(_dot(olat, wuv_ref[...]), vdim).astype(BF)


def _mla_dec2(page_table, q, knew, wuv_all, ckv_pool, krt_pool, layer, nh, scale, npages=32):
    db, n_pages = page_table.shape
    page = ckv_pool.shape[2]
    vdim = wuv_all.shape[1] // nh
    grid_spec = pltpu.PrefetchScalarGridSpec(
        num_scalar_prefetch=1, grid=(db, n_pages // npages),
        in_specs=[pl.BlockSpec((1, nh, 256), lambda b, s, pt: (b, 0, 0)),
                  pl.BlockSpec((1, 1, 256), lambda b, s, pt: (b, 0, 0)),
                  _const_spec(wuv_all.shape),
                  pl.BlockSpec(memory_space=pl.ANY), pl.BlockSpec(memory_space=pl.ANY)],
        out_specs=pl.BlockSpec((1, 1, wuv_all.shape[1]), lambda b, s, pt: (b, 0, 0)),
        scratch_shapes=[pltpu.VMEM((2, npages, page, ckv_pool.shape[3]), F32),
                        pltpu.VMEM((2, krt_pool.shape[2], npages * page), F32),
                        pltpu.SemaphoreType.DMA((2, 2)),
                        pltpu.VMEM((nh, 1), F32), pltpu.VMEM((nh, 1), F32), pltpu.VMEM((nh, 128), F32)])
    return pl.pallas_call(
        functools.partial(_mla_dec2_kernel, layer=layer, npages=npages, scale=scale, vdim=vdim),
        grid_spec=grid_spec,
        out_shape=jax.ShapeDtypeStruct((db, 1, wuv_all.shape[1]), BF),
        compiler_params=_params(("arbitrary", "arbitrary")),
        name="mla_dec",
    )(page_table, q, knew, wuv_all, ckv_pool, krt_pool)


def _suffix_sum_lanes(x):
    n = x.shape[1]
    lane = lax.broadcasted_iota(I32, x.shape, 1)
    y = x
    k = 1
    while k < n:
        y = y + jnp.where(lane + k < n, pltpu.roll(y, n - k, axis=1), 0.0)
        k *= 2
    return y


def _fox_dec2_kernel(pt_ref, q_ref, kn_ref, vn_ref, lfn_ref, kt_hbm, vt_hbm, lft_hbm, o_ref,
                     kbuf, vbuf, lbuf, sem, m_ref, l_ref, acc_ref, carry_ref, *, layer, npages, hd):
    s_id = pl.program_id(1)
    ns = pl.num_programs(1)
    page = kbuf.shape[2] // npages

    def copies(bb, ss, sl):
        out = []
        first = (ns - 1 - ss) * npages
        for j in range(npages):
            pg = pt_ref[bb, first + j]
            win = pl.ds(j * page, page)
            out.append(pltpu.make_async_copy(kt_hbm.at[layer, pg], kbuf.at[sl, :, win], sem.at[0, sl]))
            out.append(pltpu.make_async_copy(vt_hbm.at[layer, pg], vbuf.at[sl, :, win], sem.at[1, sl]))
            out.append(pltpu.make_async_copy(lft_hbm.at[layer, pg], lbuf.at[sl, :, win], sem.at[2, sl]))
        return out

    slot = _step_pipeline(copies)

    @pl.when(s_id == 0)
    def _():
        m_ref[...] = jnp.full(m_ref.shape, NEG_BIG, F32)
        l_ref[...] = jnp.zeros(l_ref.shape, F32)
        acc_ref[...] = jnp.zeros(acc_ref.shape, F32)
        carry_ref[...] = lfn_ref[0]

    q = q_ref[0]
    lf = lbuf[slot]
    incl = _suffix_sum_lanes(lf)
    carry = carry_ref[...]
    sc = _dot(q.astype(BF), kbuf[slot].astype(BF)) + ((incl - lf) + carry)
    carry_ref[...] = carry + incl[:, 0:1]
    p, alpha, m_new, l_new = _softmax_step(sc, m_ref[...], l_ref[...])
    acc_ref[...] = alpha * acc_ref[...] + _dot_nt(p.astype(BF), vbuf[slot].astype(BF))
    m_ref[...] = m_new
    l_ref[...] = l_new

    @pl.when(s_id == ns - 1)
    def _():
        s_new = jnp.sum(q * _bf_round(kn_ref[0]), axis=1, keepdims=True)
        m_p = m_ref[...]
        m_n = jnp.maximum(m_p, s_new)
        a = jnp.exp(m_p - m_n)
        p_new = jnp.exp(s_new - m_n)
        l_f = a * l_ref[...] + p_new
        acc = a * acc_ref[...] + _bf_round(p_new) * _bf_round(vn_ref[0])
        o_ref[0] = _diag_rows(acc / l_f, hd).astype(BF)


def _fox_dec2(page_table, qblk, kn, vn, lf_new, kt_pool, vt_pool, lft_pool, layer, nh, hd, npages=16):
    db, n_pages = page_table.shape
    page = kt_pool.shape[3]
    w = nh * hd
    m3 = lambda b, s, pt: (b, 0, 0)
    grid_spec = pltpu.PrefetchScalarGridSpec(
        num_scalar_prefetch=1, grid=(db, n_pages // npages),
        in_specs=[pl.BlockSpec((1, nh, w), m3), pl.BlockSpec((1, 1, w), m3), pl.BlockSpec((1, 1, w), m3),
                  pl.BlockSpec((1, nh, 1), m3),
                  pl.BlockSpec(memory_space=pl.ANY), pl.BlockSpec(memory_space=pl.ANY),
                  pl.BlockSpec(memory_space=pl.ANY)],
        out_specs=pl.BlockSpec((1, 1, w), m3),
        scratch_shapes=[pltpu.VMEM((2, w, npages * page), F32), pltpu.VMEM((2, w, npages * page), F32),
                        pltpu.VMEM((2, nh, npages * page), F32), pltpu.SemaphoreType.DMA((3, 2)),
                        pltpu.VMEM((nh, 1), F32), pltpu.VMEM((nh, 1), F32), pltpu.VMEM((nh, w), F32),
                        pltpu.VMEM((nh, 1), F32)])
    return pl.pallas_call(
        functools.partial(_fox_dec2_kernel, layer=layer, npages=npages, hd=hd),
        grid_spec=grid_spec,
        out_shape=jax.ShapeDtypeStruct((db, 1, w), BF),
        compiler_params=_params(("arbitrary", "arbitrary")),
        name="fox_dec",
    )(page_table, qblk, kn, vn, lf_new, kt_pool, vt_pool, lft_pool)


def _head_rows(fn, nh):
    return jnp.concatenate([fn(h) for h in range(nh)], axis=0)


def _nsa_cmp_dec_kernel(pt_ref, q_ref, perm_ref, wk_ref, wv_ref, pek_ref, pev_ref, w1k_ref, w1v_ref,
                        w2k_ref, w2v_ref, ov_ref, tbl_ref, kc_hbm, vc_hbm, oc_ref, ti_ref, ok_ref,
                        kbuf, vbuf, sem, xpk_ref, xpv_ref, fsk_ref, fsv_ref,
                        *, layer, npages, nkv, grp, pos, nsel):
    s_id = pl.program_id(1)
    ns = pl.num_programs(1)
    width = kbuf.shape[2]
    per_page = kbuf.shape[3] // CMP_STRIDE
    nchunk = npages * per_page

    def copies(bb, ss, sl):
        out = []
        for j in range(npages):
            pg = pt_ref[bb, ss * npages + j]
            out.append(pltpu.make_async_copy(kc_hbm.at[layer, pg], kbuf.at[sl, j], sem.at[0, sl]))
            out.append(pltpu.make_async_copy(vc_hbm.at[layer, pg], vbuf.at[sl, j], sem.at[1, sl]))
        return out

    slot = _step_pipeline(copies)
    base = pl.multiple_of(s_id * nchunk, nchunk)
    for buf, xp_ref, w_ref, fs_ref in ((kbuf, xpk_ref, wk_ref, fsk_ref), (vbuf, xpv_ref, wv_ref, fsv_ref)):
        for j in range(npages):
            xp = _dot_nt(perm_ref[...], buf[slot, j].astype(BF))
            for r in range(CMP_STRIDE):
                xp_ref[r, j * per_page:(j + 1) * per_page, :] = xp[r * per_page:(r + 1) * per_page, :]
        y = jnp.zeros((nchunk, 2 * width), F32)
        for r in range(CMP_STRIDE):
            y = y + _dot(xp_ref[r].astype(BF), w_ref[r])
        fs_ref[pl.ds(base, nchunk), :] = y

    @pl.when(s_id == ns - 1)
    def _():
        ncmp = fsk_ref.shape[0]
        nh = nkv * grp
        n_valid = (pos - (CMP_BLOCK - 1)) // CMP_STRIDE + 1
        lane = lax.broadcasted_iota(I32, (1, ncmp), 1)
        dist_c = pos - (lane * CMP_STRIDE + CMP_BLOCK - 1)
        kcmp = _compress_tail(fsk_ref[...], pek_ref, w1k_ref, w2k_ref, width).astype(BF)
        vcmp = _compress_tail(fsv_ref[...], pev_ref, w1v_ref, w2v_ref, width).astype(BF)
        bias = _head_rows(lambda h: _t5_bias(dist_c, lambda j: tbl_ref[j, h]), nh)
        valid = lane < n_valid
        s = jnp.where(valid, _dot_nt(q_ref[0].astype(BF), kcmp) + bias, NEG_BIG)
        mx = jnp.max(s, axis=1, keepdims=True)
        e = jnp.where(valid, jnp.exp(s - mx), 0.0)
        p = (e / jnp.maximum(jnp.sum(e, axis=1, keepdims=True), 1e-30)).astype(BF)
        oc_ref[0] = _dot(p, vcmp)
        imp_h = _dot(p, ov_ref[...])
        imp = _head_rows(lambda k: sum(imp_h[k * grp + g:k * grp + g + 1, :] for g in range(grp)), nkv)
        jl = lax.broadcasted_iota(I32, imp.shape, 1)
        cur = pos // SEL_BLOCK
        forced = (jl == 0) | (jl == cur) | (jl == cur - 1)
        score = jnp.where(forced, FORCE_SCORE, imp)
        score = jnp.where(jl * SEL_BLOCK <= pos, score, NEG_SCORE)
        score = jnp.where(jl < nsel, score, -jnp.inf)
        ol = lax.broadcasted_iota(I32, (nkv, LANES), 1)
        ti = jnp.zeros((nkv, LANES), I32)
        ok = jnp.zeros((nkv, LANES), I32)
        for n, _, idx, val in _topk_steps(score, min(SEL_TOP_N, nsel)):
            ti = jnp.where(ol == n, idx.astype(I32), ti)
            ok = jnp.where(ol == n, (val > 0.5 * NEG_SCORE).astype(I32), ok)
        ti_ref[0] = ti
        ok_ref[0] = ok


def _nsa_cmp_dec(page_table, qblk, perm, wk, wv, pek, pev, w1k, w1v, w2k, w2v, ov, t5_table,
                 kc_pool, vc_pool, layer, nkv, grp, pos, nsel, npages=16):
    db, n_pages = page_table.shape
    width, page = kc_pool.shape[2], kc_pool.shape[3]
    nh = nkv * grp
    ncmp = n_pages * page // CMP_STRIDE
    nchunk = npages * page // CMP_STRIDE
    consts = [perm, wk, wv, pek, pev, w1k, w1v, w2k, w2v, ov]
    m3 = lambda b, s, pt: (b, 0, 0)
    grid_spec = pltpu.PrefetchScalarGridSpec(
        num_scalar_prefetch=1, grid=(db, n_pages // npages),
        in_specs=([pl.BlockSpec((1, nh, width), m3)] + [_const_spec(c.shape) for c in consts]
                  + [pl.BlockSpec(memory_space=pltpu.SMEM),
                     pl.BlockSpec(memory_space=pl.ANY), pl.BlockSpec(memory_space=pl.ANY)]),
        out_specs=[pl.BlockSpec((1, nh, width), m3), pl.BlockSpec((1, nkv, LANES), m3),
                   pl.BlockSpec((1, nkv, LANES), m3)],
        scratch_shapes=[pltpu.VMEM((2, npages, width, page), F32), pltpu.VMEM((2, npages, width, page), F32),
                        pltpu.SemaphoreType.DMA((2, 2)),
                        pltpu.VMEM((CMP_STRIDE, nchunk, width), F32), pltpu.VMEM((CMP_STRIDE, nchunk, width), F32),
                        pltpu.VMEM((ncmp, 2 * width), F32), pltpu.VMEM((ncmp, 2 * width), F32)])
    return pl.pallas_call(
        functools.partial(_nsa_cmp_dec_kernel, layer=layer, npages=npages, nkv=nkv, grp=grp, pos=pos, nsel=nsel),
        grid_spec=grid_spec,
        out_shape=[jax.ShapeDtypeStruct((db, nh, width), F32),
                   jax.ShapeDtypeStruct((db, nkv, LANES), I32),
                   jax.ShapeDtypeStruct((db, nkv, LANES), I32)],
        compiler_params=_params(("arbitrary", "arbitrary")),
        name="nsa_dec_cmp",
    )(page_table, qblk, *consts, t5_table, kc_pool, vc_pool)


def _nsa_sel_win_dec_kernel(pt_ref, ti_ref, ok_ref, q_ref, ksn_ref, vsn_ref, kwn_ref, vwn_ref, wk_ref, wv_ref,
                            gt_ref, oc_ref, tbl_ref, ks_hbm, vs_hbm, o_ref, kb, vb, sem,
                            *, layer, nkv, grp, pos, npast_sel, page, nwin):
    b = pl.program_id(0)
    nh = nkv * grp
    ntop = kb.shape[2] // page
    bpp = page // SEL_BLOCK

    hdim = kb.shape[1] // nkv

    @pl.when(b == 0)
    def _():
        kb[...] = jnp.zeros(kb.shape, F32)
        vb[...] = jnp.zeros(vb.shape, F32)

    def sel_dma(k, n, wait):
        blk = ti_ref[b, k * ntop + n]
        win = pl.ds(n * page, page)
        feat = pl.ds(k * hdim, hdim)

        @pl.when(blk < npast_sel)
        def _():
            pg = pt_ref[b, blk // bpp]
            for pool, buf, sm in ((ks_hbm, kb, sem.at[0]), (vs_hbm, vb, sem.at[1])):
                c = pltpu.make_async_copy(pool.at[layer, pg, feat, :], buf.at[k, feat, win], sm)
                if wait:
                    c.wait()
                else:
                    c.start()

        if not wait:
            @pl.when(blk >= npast_sel)
            def _():
                kb[k, feat, win] = jnp.zeros((hdim, page), F32)
                vb[k, feat, win] = jnp.zeros((hdim, page), F32)

    for k in range(nkv):
        for n in range(ntop):
            sel_dma(k, n, False)

    q = q_ref[0]
    qb = q.astype(BF)
    tbl0 = _head_rows(lambda h: jnp.zeros((1, 1), F32) + tbl_ref[0, h], nh)

    jw = lax.broadcasted_iota(I32, (1, nwin), 1)
    bias_w = _head_rows(lambda h: _t5_bias(nwin - jw, lambda j: tbl_ref[j, h]), nh)
    s = _dot(qb, wk_ref[0, 0].astype(BF)) + bias_w
    s_new = jnp.sum(q * _bf_round(kwn_ref[0]), axis=1, keepdims=True) + tbl0
    mx = jnp.maximum(jnp.max(s, axis=1, keepdims=True), s_new)
    e = jnp.exp(s - mx)
    e_new = jnp.exp(s_new - mx)
    l = jnp.sum(e, axis=1, keepdims=True) + e_new
    o_win = (_dot_nt(e.astype(BF), wv_ref[0, 0].astype(BF)) + _bf_round(e_new) * _bf_round(vwn_ref[0])) / l

    for k in range(nkv):
        for n in range(ntop):
            sel_dma(k, n, True)

    nkeys = ntop * page
    lane = lax.broadcasted_iota(I32, (1, nkeys), 1)
    slot_id = lane // page
    row = lane % page
    gates = gt_ref[0]
    for k in range(nkv):
        rows = slice(k * grp, (k + 1) * grp)
        blkv = jnp.zeros((1, nkeys), I32)
        okv = jnp.zeros((1, nkeys), I32)
        new_sel = jnp.int32(0)
        for n in range(ntop):
            blk = ti_ref[b, k * ntop + n]
            okn = ok_ref[b, k * ntop + n]
            blkv = jnp.where(slot_id == n, blk, blkv)
            okv = jnp.where(slot_id == n, okn, okv)
            new_sel = new_sel + jnp.where((blk >= npast_sel) & (okn > 0), 1, 0)
        kpos = (blkv // bpp) * page + row
        dist = pos - kpos
        mask = (okv > 0) & (blkv < npast_sel) & (row // SEL_BLOCK == blkv % bpp) & (dist >= 0)
        bias = _head_rows(lambda g: _t5_bias(dist, lambda j: tbl_ref[j, k * grp + g]), grp)
        sk = jnp.where(mask, _dot(qb[rows], kb[k].astype(BF)) + bias, NEG_BIG)
        sk_new = jnp.sum(q[rows] * _bf_round(ksn_ref[0]), axis=1, keepdims=True) + tbl0[rows]
        sk_new = jnp.where(new_sel > 0, sk_new, NEG_BIG)
        mx = jnp.maximum(jnp.max(sk, axis=1, keepdims=True), sk_new)
        e = jnp.where(mask, jnp.exp(sk - mx), 0.0)
        e_new = jnp.where(new_sel > 0, jnp.exp(sk_new - mx), 0.0)
        l = jnp.maximum(jnp.sum(e, axis=1, keepdims=True) + e_new, 1e-30)
        o_slc = (_dot_nt(e.astype(BF), vb[k].astype(BF)) + _bf_round(e_new) * _bf_round(vsn_ref[0])) / l
        g3 = gates[rows]
        o_ref[0, rows, :] = g3[:, 0:1] * oc_ref[0, rows, :] + g3[:, 1:2] * o_slc + g3[:, 2:3] * o_win[rows]


def _nsa_sel_win_dec(page_table, top_i, top_ok, qblk, ksn, vsn, kwn, vwn, win_kt, win_vt, gates, o_cmp,
                     t5_table, ks_pool, vs_pool, layer, nkv, grp, pos, npast_sel):
    db = qblk.shape[0]
    nh = nkv * grp
    ntop = top_i.shape[1] // nkv
    width, page = ks_pool.shape[2], ks_pool.shape[3]
    nwin = win_kt.shape[3]
    m3 = lambda b, *_: (b, 0, 0)
    grid_spec = pltpu.PrefetchScalarGridSpec(
        num_scalar_prefetch=3, grid=(db,),
        in_specs=[pl.BlockSpec((1, nh, width), m3)] + [pl.BlockSpec((1, 1, width), m3)] * 4
                 + [pl.BlockSpec((1, 1, width, nwin), lambda b, *_: (layer, b, 0, 0))] * 2
                 + [pl.BlockSpec((1, nh, 3), m3), pl.BlockSpec((1, nh, width), m3),
                    pl.BlockSpec(memory_space=pltpu.SMEM),
                    pl.BlockSpec(memory_space=pl.ANY), pl.BlockSpec(memory_space=pl.ANY)],
        out_specs=pl.BlockSpec((1, nh, width), m3),
        scratch_shapes=[pltpu.VMEM((nkv, width, ntop * page), F32), pltpu.VMEM((nkv, width, ntop * page), F32),
                        pltpu.SemaphoreType.DMA((2,))])
    return pl.pallas_call(
        functools.partial(_nsa_sel_win_dec_kernel, layer=layer, nkv=nkv, grp=grp, pos=pos,
                          npast_sel=npast_sel, page=page, nwin=nwin),
        grid_spec=grid_spec,
        out_shape=jax.ShapeDtypeStruct((db, nh, width), F32),
        compiler_params=_params(("arbitrary",)),
        name="nsa_dec_sel_win",
    )(page_table, top_i, top_ok, qblk, ksn, vsn, kwn, vwn, win_kt, win_vt, gates, o_cmp, t5_table,
      ks_pool, vs_pool)


def _pad_last(a, n):
    return jnp.pad(a, [(0, 0)] * (a.ndim - 1) + [(0, n - a.shape[-1])])


def _swap_halves(a):
    h = a.shape[-1] // 2
    return jnp.concatenate([a[..., h:], a[..., :h]], axis=-1)


def _split_cols(a, sizes):
    out, start = [], 0
    for s in sizes:
        out.append(a[..., start:start + s])
        start += s
    return out


def _block_diag(blocks):
    n, r, c = blocks.shape
    eye = jnp.eye(n, dtype=blocks.dtype)
    return (blocks[:, :, None, :] * eye[:, None, :, None]).reshape(n * r, n * c)


def _rope_tables(pos):
    half = MLA_ROPE // 2
    inv = ROPE_THETA ** (-jnp.arange(half, dtype=F32) / half)
    ang = pos.astype(F32)[:, None] * inv[None, :]
    cos, sin = jnp.cos(ang), jnp.sin(ang)
    z = jnp.zeros((pos.shape[0], LANES - MLA_ROPE), F32)
    return jnp.concatenate([cos, cos, z], axis=1), jnp.concatenate([-sin, sin, z], axis=1)


def _overlap_matrix(ncmp, nsel, ncols):
    c0 = np.arange(ncmp)[:, None] * CMP_STRIDE
    s0 = np.arange(ncols)[None, :] * SEL_BLOCK
    ov = np.clip(np.minimum(c0 + CMP_BLOCK, s0 + SEL_BLOCK) - np.maximum(c0, s0), 0, None) / CMP_BLOCK
    ov = np.where(np.arange(ncols)[None, :] < nsel, ov, 0.0)
    return jnp.asarray(ov, dtype=BF)


def kernel(x_prompt, x_sample, cache_mla_ckv, cache_mla_krope, cache_fox_k, cache_fox_v, cache_fox_logf,
           cache_nsa_kcmp, cache_nsa_vcmp, cache_nsa_kslc, cache_nsa_vslc, state_nsa_kwin, state_nsa_vwin,
           page_table, norm_attn, norm_ffn, norm_final, ab_w_in, ab_fox_bf, ab_mla_gq, ab_mla_gkv,
           ab_mla_wuq, ab_mla_wuk, ab_mla_wuv, ab_w_out, c_w_in, c_pe_k, c_w1_k, c_w2_k, c_pe_v, c_w1_v,
           c_w2_v, c_w_out, t5_table, ffn_w_gate, ffn_w_up, ffn_w_down):
    bsz, t, d = x_prompt.shape
    db, t_s, _ = x_sample.shape
    depth = norm_attn.shape[0]
    n_pool, page = cache_mla_ckv.shape[1], cache_mla_ckv.shape[2]
    n_pages = page_table.shape[1]
    past_len = n_pages * page
    q_rank, mla_h, _ = ab_mla_wuq.shape[1:]
    kv_rank = ab_mla_wuk.shape[1]
    mla_v = ab_mla_wuv.shape[3]
    fox_h = ab_fox_bf.shape[1]
    fox_d = cache_fox_k.shape[4]
    nkv, hd = cache_nsa_kcmp.shape[3], cache_nsa_kcmp.shape[4]
    nsa_h = t5_table.shape[1]
    grp = nsa_h // nkv
    nbuf = state_nsa_kwin.shape[2]
    assert t_s == 1 and (q_rank, kv_rank, fox_h * fox_d) == (256, 128, 512)
    assert past_len % CMP_STRIDE == 0 and nbuf <= WINDOW and past_len >= nbuf
    assert t % 256 == 0 and t >= WINDOW and LANES * CMP_STRIDE == t
    mla_scale = (MLA_NOPE + MLA_ROPE) ** -0.5
    nsa_scale = hd ** -0.5
    m_p = bsz * t
    tm = 512

    pos_p = jnp.arange(t)
    pos_s = jnp.tile(past_len + jnp.arange(t_s), db)
    cos_p, sin_p = _rope_tables(pos_p)
    cos_s, sin_s = _rope_tables(pos_s)

    xp = x_prompt.reshape(m_p, d)
    xs = x_sample.reshape(db * t_s, d)
    tiles, bias_c = _t5_km(t5_table, t, t // CMP_STRIDE)
    ov_p = _overlap_matrix(t // CMP_STRIDE, t // SEL_BLOCK, LANES).T
    nsel_s = -(-(past_len + t_s) // SEL_BLOCK)
    ov_s = _overlap_matrix(past_len // CMP_STRIDE, nsel_s, -(-nsel_s // LANES) * LANES)
    tconst = t5_table[T5_BUCKETS - 1]

    def feat_major(a):
        a = jnp.moveaxis(a, 2, -1)
        return a.reshape(a.shape[0], a.shape[1], -1, a.shape[-1])

    krt_pool = feat_major(cache_mla_krope)
    fkt_pool, fvt_pool, lft_pool = feat_major(cache_fox_k), feat_major(cache_fox_v), feat_major(cache_fox_logf)
    kc_pool, vc_pool = feat_major(cache_nsa_kcmp), feat_major(cache_nsa_vcmp)
    ks_pool, vs_pool = feat_major(cache_nsa_kslc), feat_major(cache_nsa_vslc)
    win_kt, win_vt = feat_major(state_nsa_kwin), feat_major(state_nsa_vwin)
    per_page = page // CMP_STRIDE
    perm_np = np.zeros((page, page), np.float32)
    for r_ in range(CMP_STRIDE):
        for n_ in range(per_page):
            perm_np[r_ * per_page + n_, n_ * CMP_STRIDE + r_] = 1.0
    perm = jnp.asarray(perm_np, dtype=BF)
    fox_sel = jnp.eye(fox_h, dtype=F32)[None, :, :, None]
    nsa_sel = jnp.asarray(np.arange(nsa_h)[:, None] // grp == np.arange(nkv)[None, :], dtype=F32)[None, :, :, None]

    ab_p, ab_s, c_p, c_s = [], [], [], []
    for l in range(depth):
        i = l // 2
        g_attn = norm_attn[l][None, :]
        g_ffn = norm_ffn[l][None, :]
        gf = norm_final[None, :] if l == depth - 1 else None
        wg, wu, wd = ffn_w_gate[l].astype(BF), ffn_w_up[l].astype(BF), ffn_w_down[l].astype(BF)
        if l % 2 == 0:
            cq_w, ckv_w, kr_w, fq_w, fk_w, fv_w, fl_w = _split_cols(
                ab_w_in[i], (q_rank, kv_rank, MLA_ROPE, fox_h * fox_d, fox_h * fox_d, fox_h * fox_d, fox_h))
            n_lat = q_rank + kv_rank
            n_fox = 3 * fox_h * fox_d
            w_in = jnp.concatenate([ab_w_in[i][:, :n_lat], _pad_last(kr_w, LANES),
                                    _pad_last(_swap_halves(kr_w), LANES),
                                    ab_w_in[i][:, n_lat + MLA_ROPE:n_lat + MLA_ROPE + n_fox],
                                    _pad_last(fl_w, LANES)], axis=1).astype(BF)
            wuq = ab_mla_wuq[i]
            rope_w = wuq[:, :, MLA_NOPE:]
            wuq_all = jnp.concatenate([
                wuq[:, :, :MLA_NOPE].reshape(q_rank, mla_h * MLA_NOPE),
                _pad_last(rope_w, LANES).reshape(q_rank, mla_h * LANES),
                _pad_last(_swap_halves(rope_w), LANES).reshape(q_rank, mla_h * LANES)], axis=1).astype(BF)
            wuk_bd = _block_diag(jnp.transpose(ab_mla_wuk[i], (1, 2, 0))).astype(BF)
            wuv_bd = _block_diag(jnp.transpose(ab_mla_wuv[i], (1, 0, 2))).astype(BF)
            wuv_all = ab_mla_wuv[i].reshape(kv_rank, mla_h * mla_v).astype(BF)
            w_out = ab_w_out[i].astype(BF)
            gq, gkv, bfox = ab_mla_gq[i][None, :], ab_mla_gkv[i][None, :], ab_fox_bf[i][None, :]

            cq, ckv, kr, kcat, fk, fv, lf, fqh, fkh, fvt, ckvt = _ab_proj_prompt(
                xp, g_attn, w_in, gkv, bfox, cos_p, sin_p, bsz, t, tm, fox_d ** -0.5 * LOG2E)
            q_hm = _mla_q_prompt(cq, gq, wuq_all, wuk_bd, cos_p, sin_p, bsz, t, tm, mla_h, mla_scale * LOG2E)
            o_mla = _mla_attn_km(q_hm, kcat.reshape(bsz, t, -1), ckvt, wuv_bd)
            lf3 = lf.reshape(bsz, t, fox_h)
            qx, kx = _fox_bias_cols(jnp.transpose(lf3, (0, 2, 1)), LANES - fox_d)
            o_fox = _fox_attn_km(jnp.concatenate([fqh, qx], axis=-1), jnp.concatenate([fkh, kx], axis=-1),
                                 fvt, fox_d)
            xp = _post(xp, [o_mla.reshape(m_p, -1), o_fox.reshape(m_p, -1)], w_out, g_ffn, wg, wu, wd, gf, tm)
            hm4 = lambda a: jnp.transpose(a.reshape(bsz, fox_h, fox_d, t), (0, 3, 1, 2))
            ab_p.append((ckv.reshape(bsz, t, -1), kr.reshape(bsz, t, -1), hm4(fk), hm4(fv), lf3))

            ms = db * t_s
            cq, ckv, kr, kcat, fq, fk, fv, fkb, fvb, lf = _ab_proj(xs, g_attn, w_in, gkv, bfox, cos_s, sin_s, ms)
            qcat = _mla_q(cq, gq, wuq_all, wuk_bd, cos_s, sin_s, ms, mla_h)
            o_mla = _mla_dec2(page_table, qcat.reshape(db, mla_h, 256).astype(F32),
                              kcat.reshape(db, 1, 256).astype(F32), wuv_all, cache_mla_ckv, krt_pool,
                              i, mla_h, mla_scale)
            qblk = (fox_sel * fq.reshape(db, 1, fox_h, fox_d).astype(F32)).reshape(db, fox_h, fox_h * fox_d)
            o_fox = _fox_dec2(page_table, qblk, fk.reshape(db, 1, -1), fv.reshape(db, 1, -1),
                              lf.reshape(db, fox_h, 1), fkt_pool, fvt_pool, lft_pool, i, fox_h, fox_d)
            xs = _post(xs, [o_mla.reshape(ms, -1), o_fox.reshape(ms, -1)], w_out, g_ffn, wg, wu, wd, gf, ms)
            ab_s.append((ckv.reshape(db, t_s, -1), kr.reshape(db, t_s, -1), fk.reshape(db, t_s, fox_h, fox_d),
                         fv.reshape(db, t_s, fox_h, fox_d), lf.reshape(db, t_s, fox_h)))
        else:
            nq, nk = nsa_h * hd, nkv * hd
            n_main = nq + 6 * nk
            w_in = jnp.concatenate([c_w_in[i][:, :n_main], _pad_last(c_w_in[i][:, n_main:], LANES)],
                                   axis=1).astype(BF)
            w_out = c_w_out[i].astype(BF)
            eye = jnp.eye(nkv, dtype=F32)

            def cmp_weights(pe, w1, w2):
                w1r = w1.reshape(2, CMP_STRIDE, hd, -1)
                w_dec = jnp.concatenate([w1r[0], w1r[1]], axis=-1).astype(BF)
                rep = lambda m: _block_diag(jnp.broadcast_to(m, (nkv,) + m.shape))
                kron = jax.vmap(rep)
                w_bd = jnp.concatenate([kron(w1r[0]), kron(w1r[1])], axis=-1).astype(BF)
                pe_flat = jnp.pad(pe.reshape(1, -1), ((0, 7), (0, 0))).astype(BF)
                return (w_dec, w_bd, pe_flat, w1.astype(BF), jnp.tile(w1, (1, nkv)).astype(BF),
                        w2.astype(BF), rep(w2).astype(BF))

            kw_ = cmp_weights(c_pe_k[i], c_w1_k[i], c_w2_k[i])
            vw_ = cmp_weights(c_pe_v[i], c_w1_v[i], c_w2_v[i])

            q, kc, vc, kct, vct, kst, vst32, kwt, vwt32, ksh, kwh, vst, vwt, gatest = _c_proj_prompt(
                xp, g_attn, w_in, bsz, t, tm, nsa_h, nkv, hd, nsa_scale * LOG2E)
            r3 = lambda a: a.reshape(bsz, t, -1)
            kcmp, vcmpt = _compress_prompt(r3(kc), r3(vc), kw_[1], vw_[1], kw_[2], vw_[2], kw_[4], vw_[4],
                                           kw_[6], vw_[6], nkv)
            o = _nsa_attn_km(q, kcmp, vcmpt, ksh, vst, kwh, vwt, gatest, bias_c, tiles, ov_p, nkv, grp, hd)
            xp = _post(xp, [o.reshape(m_p, -1)], w_out, g_ffn, wg, wu, wd, gf, tm)
            r4 = lambda a: jnp.transpose(a.reshape(bsz, nkv, hd, a.shape[-1]), (0, 3, 1, 2))
            keep = min(WINDOW, t)
            c_p.append((r4(kct), r4(vct), r4(kst), r4(vst32), r4(kwt[:, :, t - keep:]), r4(vwt32[:, :, t - keep:])))

            ms = db * t_s
            q, kc, vc, ks, vs, kw, vw, ksb, vsb, kwb, vwb, gates = _c_proj(xs, g_attn, w_in, ms, nq, nk, nsa_scale)
            qblk = (nsa_sel * q.reshape(db, nsa_h, 1, hd).astype(F32)).reshape(db, nsa_h, nkv * hd)
            o_cmp, top_i, top_ok = _nsa_cmp_dec(page_table, qblk, perm, kw_[1], vw_[1], kw_[2], vw_[2], kw_[4],
                                                vw_[4], kw_[6], vw_[6], ov_s, t5_table, kc_pool, vc_pool, i,
                                                nkv, grp, past_len, nsel_s)
            ntop = min(SEL_TOP_N, nsel_s)
            flat_top = lambda a: a[:, :, :ntop].reshape(db, nkv * ntop)
            r1 = lambda a: a.reshape(db, 1, -1)
            o_wide = _nsa_sel_win_dec(page_table, flat_top(top_i), flat_top(top_ok), qblk, r1(ks), r1(vs), r1(kw),
                                      r1(vw), win_kt, win_vt, gates[:, :3 * nsa_h].reshape(db, nsa_h, 3), o_cmp,
                                      t5_table, ks_pool, vs_pool, i, nkv, grp, past_len, past_len // SEL_BLOCK)
            o = jnp.stack([o_wide[:, k * grp:(k + 1) * grp, k * hd:(k + 1) * hd] for k in range(nkv)], axis=1)
            xs = _post(xs, [o.reshape(ms, -1).astype(BF)], w_out, g_ffn, wg, wu, wd, gf, ms)
            r4 = lambda a: a.reshape(db, t_s, nkv, hd)
            c_s.append((r4(kc), r4(vc), r4(ks), r4(vs),
                        jnp.concatenate([state_nsa_kwin[i], r4(kw)], axis=1)[:, t_s:],
                        jnp.concatenate([state_nsa_vwin[i], r4(vw)], axis=1)[:, t_s:]))

    def stack(rows, j):
        return jnp.stack([r[j] for r in rows])

    return (xp.reshape(bsz, t, d), xs.reshape(db, t_s, d),
            stack(ab_p, 0), stack(ab_s, 0), stack(ab_p, 1), stack(ab_s, 1),
            stack(ab_p, 2), stack(ab_s, 2), stack(ab_p, 3), stack(ab_s, 3), stack(ab_p, 4), stack(ab_s, 4),
            stack(c_p, 0), stack(c_s, 0), stack(c_p, 1), stack(c_s, 1), stack(c_p, 2), stack(c_s, 2),
            stack(c_p, 3), stack(c_s, 3), stack(c_p, 4), stack(c_s, 4), stack(c_p, 5), stack(c_s, 5))
```
